```python
import math, functools
import jax, jax.numpy as jnp
from jax import lax
import numpy as np

D_MODEL = 1024
BATCH = 32
SEQ = 2048
DEPTH = 1
DEC_BATCH = 1
DEC_SEQ = 16384
PAST_LEN = 128

HEAD_DIM = 128
HEADS_PER_GROUP = 4
DILATION_GROUPS = ((128, 1), (512, 4), (2048, 16))
N_GROUPS = len(DILATION_GROUPS)
N_ATTN_HEADS = N_GROUPS * HEADS_PER_GROUP
ATTN_WIDTH = N_ATTN_HEADS * HEAD_DIM
ATTN_OUT_WIDTH = HEADS_PER_GROUP * HEAD_DIM
BAND_BLOCK = 64
N_BUCKETS = 32
MAX_DISTANCE = 1024
HYENA_WIDTH = D_MODEL
HYENA_ORDER = 2
SHORT_CONV = 3
FILTER_BANDS = 16
FILTER_EMB_DIM = 2 * FILTER_BANDS + 1
FILTER_HIDDEN = 64
N_FILTER_CH = HYENA_ORDER * 2 * HYENA_WIDTH
FILTER_OUT_SCALE = 0.02
DECAY_TARGET = 1e-2
FAST_DECAY_PCT = 0.3
SLOW_DECAY_PCT = 1.5
N_BRANCHES = 2
D_FF = 4 * D_MODEL
IN_WIDTH = 3 * ATTN_WIDTH + (HYENA_ORDER + 1) * HYENA_WIDTH + N_BRANCHES * D_MODEL
NORM_EPS = 1e-6
MASK_VALUE = -1e30

kernel_name = 'hybrid_dilated_attn_hyena_encoder'


def rms_norm(x, g):
    xf = x.astype(jnp.float32)
    y = xf * lax.rsqrt(jnp.mean(xf * xf, axis=-1, keepdims=True) + NORM_EPS)
    return (y * g.astype(jnp.float32)).astype(x.dtype)


def t5_bucket(rel):
    nb = N_BUCKETS // 2
    max_exact = nb // 2
    side = jnp.where(rel > 0, nb, 0)
    n = jnp.abs(rel)
    nf = jnp.maximum(n, 1).astype(jnp.float32)
    large = max_exact + (jnp.log(nf / max_exact) / math.log(MAX_DISTANCE / max_exact)
                         * (nb - max_exact)).astype(jnp.int32)
    large = jnp.minimum(large, nb - 1)
    return side + jnp.where(n < max_exact, n, large)


def dilated_band_attention(q, k, v, bias_table, dil, radius):
    B, L, G, hd = q.shape
    M = L // dil
    nblk = -(-M // BAND_BLOCK)
    Mp = nblk * BAND_BLOCK

    def to_sub(t):
        return t.reshape(B, M, dil, G, hd).transpose(0, 2, 1, 3, 4)

    def windows(t):
        tp = jnp.pad(t, ((0, 0), (0, 0), (BAND_BLOCK, Mp - M + BAND_BLOCK), (0, 0), (0, 0)))
        tp = tp.reshape(B, dil, nblk + 2, BAND_BLOCK, G, hd)
        return jnp.concatenate([tp[:, :, :-2], tp[:, :, 1:-1], tp[:, :, 2:]], axis=3)

    qs = jnp.pad(to_sub(q), ((0, 0), (0, 0), (0, Mp - M), (0, 0), (0, 0)))
    qb = qs.reshape(B, dil, nblk, BAND_BLOCK, G, hd)
    kw = windows(to_sub(k))
    vw = windows(to_sub(v))

    scores = jnp.einsum('bdnqgh,bdnkgh->bdngqk', qb, kw,
                        preferred_element_type=jnp.float32) * (HEAD_DIM ** -0.5)
    qi = jnp.arange(BAND_BLOCK)[:, None]
    kj = jnp.arange(3 * BAND_BLOCK)[None, :]
    delta = kj - BAND_BLOCK - qi
    bias = bias_table[t5_bucket(delta * dil)].astype(jnp.float32)
    bias = bias.transpose(2, 0, 1)
    key_m = jnp.arange(nblk)[:, None, None] * BAND_BLOCK - BAND_BLOCK + kj[None]
    valid = (jnp.abs(delta) <= radius)[None] & (key_m >= 0) & (key_m < M)
    scores = jnp.where(valid[:, None], scores + bias, MASK_VALUE)

    mx = jnp.max(scores, axis=-1, keepdims=True)
    p = jnp.exp(scores - mx)
    den = jnp.sum(p, axis=-1, keepdims=True)
    out = jnp.einsum('bdngqk,bdnkgh->bdnqgh', (p / den).astype(v.dtype), vw)
    lse = (mx + jnp.log(den))[..., 0]

    out = out.reshape(B, dil, Mp, G, hd)[:, :, :M].transpose(0, 2, 1, 3, 4).reshape(B, L, G, hd)
    lse = lse.transpose(0, 1, 2, 4, 3).reshape(B, dil, Mp, G)[:, :, :M]
    lse = lse.transpose(0, 2, 1, 3).reshape(B, L, G)
    return out.astype(jnp.float32), lse


def short_conv3(u, w, b):
    up = jnp.pad(u, ((0, 0), (1, 1), (0, 0)))
    return up[:, :-2] * w[0] + up[:, 1:-1] * w[1] + up[:, 2:] * w[2] + b


def hyena_filters(L, w1, b1, w2, b2, w3, b3, w4, freq):
    f32 = jnp.float32
    t = jnp.linspace(0.0, 1.0, L, dtype=f32)[:, None]
    ang = 2.0 * math.pi * jnp.arange(L, dtype=f32)[:, None] / L
    bands = jnp.linspace(1e-4, FILTER_BANDS - 1, FILTER_BANDS, dtype=f32)[None, :]
    z = jnp.concatenate([t, jnp.cos(bands * ang), -jnp.sin(bands * ang)], axis=-1)
    fr = freq.astype(f32)
    h = jnp.sin(fr * (z @ w1.astype(f32) + b1.astype(f32)))
    h = jnp.sin(fr * (h @ w2.astype(f32) + b2.astype(f32)))
    h = jnp.sin(fr * (h @ w3.astype(f32) + b3.astype(f32)))
    h = (h @ w4.astype(f32)).reshape(L, HYENA_ORDER, 2, HYENA_WIDTH)
    min_decay = math.log(DECAY_TARGET) / SLOW_DECAY_PCT
    max_decay = math.log(DECAY_TARGET) / FAST_DECAY_PCT
    deltas = jnp.linspace(min_decay, max_decay, HYENA_WIDTH, dtype=f32)
    decay = jnp.exp(-t * jnp.abs(deltas)[None, :])
    return h * decay[:, None, None, :]


def bidir_long_conv(z, h_fwd, h_bwd, skip):
    L = z.shape[1]
    k2 = jnp.concatenate([h_fwd, jnp.zeros_like(h_fwd[:1]), h_bwd[:0:-1]], axis=0)
    zf = z.astype(jnp.float32)
    spec = jnp.fft.rfft(zf, n=2 * L, axis=1) * jnp.fft.rfft(k2, axis=0)[None]
    y = jnp.fft.irfft(spec, n=2 * L, axis=1)[:, :L]
    return (y + zf * skip.astype(jnp.float32)).astype(z.dtype)


def encoder_layer(x, rel_bias, g_mix, w_in, g_q, g_k, w_attn_branch, w_short, b_short,
                  filt_w1, filt_b1, filt_w2, filt_b2, filt_w3, filt_b3, filt_w4, filt_freq,
                  filt_skip, w_hyena_branch, w_out, g_mlp, w_ff1, w_ff2):
    B, L, _ = x.shape
    h = rms_norm(x, g_mix)
    proj = h @ w_in
    q, k, v, hy, gate_logits = jnp.split(
        proj, [ATTN_WIDTH, 2 * ATTN_WIDTH, 3 * ATTN_WIDTH,
               3 * ATTN_WIDTH + (HYENA_ORDER + 1) * HYENA_WIDTH], axis=-1)

    q = rms_norm(q.reshape(B, L, N_ATTN_HEADS, HEAD_DIM), g_q)
    k = rms_norm(k.reshape(B, L, N_ATTN_HEADS, HEAD_DIM), g_k)
    v = v.reshape(B, L, N_ATTN_HEADS, HEAD_DIM)
    outs, lses = [], []
    for gi, (window, dil) in enumerate(DILATION_GROUPS):
        hs = slice(gi * HEADS_PER_GROUP, (gi + 1) * HEADS_PER_GROUP)
        o, s = dilated_band_attention(q[:, :, hs], k[:, :, hs], v[:, :, hs],
                                      rel_bias[:, hs], dil, window // (2 * dil))
        outs.append(o)
        lses.append(s)
    wgt = jax.nn.softmax(jnp.stack(lses, axis=2), axis=2)
    attn = jnp.einsum('blngh,blng->blgh', jnp.stack(outs, axis=2), wgt)
    attn_branch = attn.reshape(B, L, ATTN_OUT_WIDTH).astype(x.dtype) @ w_attn_branch

    hy = short_conv3(hy, w_short, b_short)
    z, *hy_gates = jnp.split(hy, HYENA_ORDER + 1, axis=-1)
    filt = hyena_filters(L, filt_w1, filt_b1, filt_w2, filt_b2, filt_w3, filt_b3, filt_w4, filt_freq)
    for n in range(HYENA_ORDER):
        z = hy_gates[n] * bidir_long_conv(z, filt[:, n, 0], filt[:, n, 1], filt_skip[n])
    hyena_branch = z @ w_hyena_branch

    g_attn, g_hyena = jnp.split(gate_logits, N_BRANCHES, axis=-1)
    merged = jax.nn.sigmoid(g_attn) * attn_branch + jax.nn.sigmoid(g_hyena) * hyena_branch
    x = x + merged @ w_out

    hm = rms_norm(x, g_mlp)
    return x + jnp.square(jax.nn.relu(hm @ w_ff1)) @ w_ff2


def setup_inputs(seed: int = 0) -> dict:
    key = jax.random.key(seed)
    ks = jax.random.split(key, 32)

    def nrm(k, shape, scale):
        return jax.random.normal(k, shape, jnp.float32) * scale

    HW = HYENA_WIDTH
    return {
        'x_prompt': nrm(ks[0], (BATCH, SEQ, D_MODEL), 1.0),
        'x_sample': nrm(ks[1], (DEC_BATCH, DEC_SEQ, D_MODEL), 1.0),
        'rel_bias': nrm(ks[2], (N_BUCKETS, N_ATTN_HEADS), 0.1),
        'g_mix': 1.0 + nrm(ks[3], (DEPTH, D_MODEL), 0.1),
        'w_in': nrm(ks[4], (DEPTH, D_MODEL, IN_WIDTH), D_MODEL ** -0.5),
        'g_q': 1.0 + nrm(ks[5], (DEPTH, HEAD_DIM), 0.1),
        'g_k': 1.0 + nrm(ks[6], (DEPTH, HEAD_DIM), 0.1),
        'w_attn_branch': nrm(ks[7], (DEPTH, ATTN_OUT_WIDTH, D_MODEL), ATTN_OUT_WIDTH ** -0.5),
        'w_short': nrm(ks[8], (DEPTH, SHORT_CONV, (HYENA_ORDER + 1) * HW), SHORT_CONV ** -0.5),
        'b_short': nrm(ks[9], (DEPTH, (HYENA_ORDER + 1) * HW), 0.02),
        'filt_w1': nrm(ks[10], (DEPTH, FILTER_EMB_DIM, FILTER_HIDDEN), FILTER_EMB_DIM ** -0.5),
        'filt_b1': nrm(ks[11], (DEPTH, FILTER_HIDDEN), 0.1),
        'filt_w2': nrm(ks[12], (DEPTH, FILTER_HIDDEN, FILTER_HIDDEN), FILTER_HIDDEN ** -0.5),
        'filt_b2': nrm(ks[13], (DEPTH, FILTER_HIDDEN), 0.1),
        'filt_w3': nrm(ks[14], (DEPTH, FILTER_HIDDEN, FILTER_HIDDEN), FILTER_HIDDEN ** -0.5),
        'filt_b3': nrm(ks[15], (DEPTH, FILTER_HIDDEN), 0.1),
        'filt_w4': nrm(ks[16], (DEPTH, FILTER_HIDDEN, N_FILTER_CH), FILTER_OUT_SCALE * FILTER_HIDDEN ** -0.5),
        'filt_freq': 1.0 + nrm(ks[17], (DEPTH, FILTER_HIDDEN), 0.1),
        'filt_skip': 1.0 + nrm(ks[18], (DEPTH, HYENA_ORDER, HW), 0.1),
        'w_hyena_branch': nrm(ks[19], (DEPTH, HW, D_MODEL), HW ** -0.5),
        'w_out': nrm(ks[20], (DEPTH, D_MODEL, D_MODEL), D_MODEL ** -0.5),
        'g_mlp': 1.0 + nrm(ks[21], (DEPTH, D_MODEL), 0.1),
        'w_ff1': nrm(ks[22], (DEPTH, D_MODEL, D_FF), D_MODEL ** -0.5),
        'w_ff2': nrm(ks[23], (DEPTH, D_FF, D_MODEL), D_FF ** -0.5),
    }


def reference(x_prompt, x_sample, rel_bias, g_mix, w_in, g_q, g_k, w_attn_branch, w_short,
              b_short, filt_w1, filt_b1, filt_w2, filt_b2, filt_w3, filt_b3, filt_w4,
              filt_freq, filt_skip, w_hyena_branch, w_out, g_mlp, w_ff1, w_ff2):
    y_prompt = x_prompt
    y_sample = x_sample
    for l in range(DEPTH):
        layer = functools.partial(
            encoder_layer, rel_bias=rel_bias, g_mix=g_mix[l], w_in=w_in[l], g_q=g_q[l],
            g_k=g_k[l], w_attn_branch=w_attn_branch[l], w_short=w_short[l],
            b_short=b_short[l], filt_w1=filt_w1[l], filt_b1=filt_b1[l], filt_w2=filt_w2[l],
            filt_b2=filt_b2[l], filt_w3=filt_w3[l], filt_b3=filt_b3[l], filt_w4=filt_w4[l],
            filt_freq=filt_freq[l], filt_skip=filt_skip[l],
            w_hyena_branch=w_hyena_branch[l], w_out=w_out[l], g_mlp=g_mlp[l],
            w_ff1=w_ff1[l], w_ff2=w_ff2[l])
        y_prompt = layer(y_prompt)
        y_sample = layer(y_sample)
    return (y_prompt, y_sample)
```

```python
import functools
import math

import numpy as np
import jax
import jax.numpy as jnp
from jax import lax
from jax.experimental import pallas as pl
from jax.experimental.pallas import tpu as pltpu

F32 = jnp.float32
BF16 = jnp.bfloat16

D_MODEL = 1024
HEAD_DIM = 128
HEADS_PER_GROUP = 4
DILATION_GROUPS = ((128, 1), (512, 4), (2048, 16))
N_GROUPS = len(DILATION_GROUPS)
ATTN_WIDTH = N_GROUPS * HEADS_PER_GROUP * HEAD_DIM
GROUP_WIDTH = HEADS_PER_GROUP * HEAD_DIM
BAND = 64
N_BUCKETS = 32
MAX_DISTANCE = 1024
HYENA_WIDTH = D_MODEL
HYENA_ORDER = 2
FILTER_BANDS = 16
FILTER_HIDDEN = 64
FILTER_OUT_SCALE = 0.02
DECAY_TARGET = 1e-2
FAST_DECAY_PCT = 0.3
SLOW_DECAY_PCT = 1.5
D_FF = 4 * D_MODEL
QKV_WIDTH = 3 * ATTN_WIDTH
HYG_WIDTH = 3 * HYENA_WIDTH + 2 * D_MODEL
NORM_EPS = 1e-6
MASK_VALUE = -1e30

FFT_N = 4096
FFT_N1 = 32
FFT_N2 = 128
SUB = 8
N_CHUNK = FFT_N2 // SUB
SPEC_ROWS = 2 * FFT_N
RADIX8 = 8
N_ALPHA = 5

VMEM_LIMIT = 56 * 1024 * 1024


def _cparams(sem):
    return pltpu.CompilerParams(dimension_semantics=sem, vmem_limit_bytes=VMEM_LIMIT)


def _norm_matmul_kernel(x_ref, g_ref, w_ref, hg_ref, o_ref, xn_ref, *, n_head_norm_tiles):
    j = pl.program_id(1)

    @pl.when(j == 0)
    def _():
        x = x_ref[...]
        ms = jnp.mean(x * x, axis=-1, keepdims=True)
        xn_ref[...] = (x * lax.rsqrt(ms + NORM_EPS) * g_ref[...]).astype(BF16)

    acc = jnp.dot(xn_ref[...], w_ref[...], preferred_element_type=F32)

    if n_head_norm_tiles == 0:
        o_ref[...] = acc.astype(o_ref.dtype)
        return

    @pl.when(j < n_head_norm_tiles)
    def _():
        hg = hg_ref[...]
        parts = []
        for h in range(acc.shape[1] // HEAD_DIM):
            a = acc[:, h * HEAD_DIM:(h + 1) * HEAD_DIM]
            ms = jnp.mean(a * a, axis=-1, keepdims=True)
            parts.append(a * lax.rsqrt(ms + NORM_EPS) * hg[:, h * HEAD_DIM:(h + 1) * HEAD_DIM])
        o_ref[...] = jnp.concatenate(parts, axis=1).astype(o_ref.dtype)

    @pl.when(j >= n_head_norm_tiles)
    def _():
        o_ref[...] = acc.astype(o_ref.dtype)


def _norm_matmul(x2d, g, w_bf16, head_gain, n_head_norm_tiles, out_dtype, tm=1024, tn=512):
    t, d = x2d.shape
    n = w_bf16.shape[1]
    return pl.pallas_call(
        functools.partial(_norm_matmul_kernel, n_head_norm_tiles=n_head_norm_tiles),
        grid=(t // tm, n // tn),
        in_specs=[
            pl.BlockSpec((tm, d), lambda i, j: (i, 0)),
            pl.BlockSpec((1, d), lambda i, j: (0, 0)),
            pl.BlockSpec((d, tn), lambda i, j: (0, j)),
            pl.BlockSpec((1, tn), lambda i, j: (0, j)),
        ],
        out_specs=pl.BlockSpec((tm, tn), lambda i, j: (i, j)),
        out_shape=jax.ShapeDtypeStruct((t, n), out_dtype),
        scratch_shapes=[pltpu.VMEM((tm, d), BF16)],
        compiler_params=_cparams(("parallel", "arbitrary")),
        name="norm_matmul",
    )(x2d, g, w_bf16, head_gain)


def _t5_bucket_np(rel):
    nb = N_BUCKETS // 2
    max_exact = nb // 2
    side = np.where(rel > 0, nb, 0)
    n = np.abs(rel)
    nf = np.maximum(n, 1).astype(np.float32)
    large = max_exact + (np.log(nf / np.float32(max_exact)) / np.float32(math.log(MAX_DISTANCE / max_exact))
                         * np.float32(nb - max_exact)).astype(np.int32)
    large = np.minimum(large, nb - 1)
    return side + np.where(n < max_exact, n, large)


def _band_bucket_index(dil):
    qi = np.arange(2 * BAND)[:, None]
    kj = np.arange(4 * BAND)[None, :]
    delta = kj - BAND - qi
    idx = _t5_bucket_np(delta * dil).astype(np.int32)
    return np.where(np.abs(delta) <= BAND, idx, -1).astype(np.int32)


def _attn_kernel(idx_ref, tbl_ref, q_ref, kp_ref, kc_ref, kn_ref, vp_ref, vc_ref, vn_ref,
                 o_ref, lse_ref, kw_ref, vw_ref, bias_ref, *, tq, m_len):
    qb = 2 * BAND
    wb = 4 * BAND
    first = (pl.program_id(0) == 0) & (pl.program_id(1) == 0) & (pl.program_id(2) == 0)

    @pl.when(first)
    def _():
        idx = idx_ref[...]
        for h in range(HEADS_PER_GROUP):
            acc = jnp.full((qb, wb), MASK_VALUE, F32)
            for b in range(N_BUCKETS):
                acc = jnp.where(idx == b, tbl_ref[h, b], acc)
            bias_ref[h] = acc

    kw_ref[0:BAND, :] = kp_ref[0]
    kw_ref[BAND:BAND + tq, :] = kc_ref[0]
    kw_ref[BAND + tq:, :] = kn_ref[0]
    vw_ref[0:BAND, :] = vp_ref[0]
    vw_ref[BAND:BAND + tq, :] = vc_ref[0]
    vw_ref[BAND + tq:, :] = vn_ref[0]

    i = pl.program_id(2)
    col = lax.broadcasted_iota(jnp.int32, (qb, wb), 1)
    lane = lax.broadcasted_iota(jnp.int32, (qb, HEAD_DIM), 1)
    for s in range(tq // qb):
        lo = BAND - i * tq - qb * s
        valid = (col >= lo) & (col < m_len + lo)
        lse_tile = jnp.zeros((qb, HEAD_DIM), F32)
        for h in range(HEADS_PER_GROUP):
            hs = slice(h * HEAD_DIM, (h + 1) * HEAD_DIM)
            qs = q_ref[0, s * qb:(s + 1) * qb, hs]
            kk = kw_ref[s * qb:s * qb + wb, hs]
            sc = lax.dot_general(qs, kk, (((1,), (1,)), ((), ())), preferred_element_type=F32)
            sc = jnp.where(valid, sc + bias_ref[h], MASK_VALUE)
            mx = jnp.max(sc, axis=-1, keepdims=True)
            p = jnp.exp(sc - mx)
            den = jnp.sum(p, axis=-1, keepdims=True)
            vv = vw_ref[s * qb:s * qb + wb, hs]
            o = jnp.dot(p.astype(BF16), vv, preferred_element_type=F32) / den
            o_ref[0, s * qb:(s + 1) * qb, hs] = o.astype(o_ref.dtype)
            lse_tile = jnp.where(lane == h, mx + jnp.log(den), lse_tile)
        lse_ref[0, s * qb:(s + 1) * qb, :] = lse_tile


def _attn_group(qkv, rel_bias, gi, bsz, seq):
    _, dil = DILATION_GROUPS[gi]
    m_len = seq // dil
    tq = min(m_len, 512)
    nblk = tq // BAND
    n_halo = m_len // BAND
    view = qkv.reshape(bsz, m_len, dil * QKV_WIDTH)
    cpb = QKV_WIDTH // GROUP_WIDTH

    def cur(off):
        return pl.BlockSpec((1, tq, GROUP_WIDTH), lambda b, r, i: (b, i, r * cpb + off + gi))

    def prev(off):
        return pl.BlockSpec((1, BAND, GROUP_WIDTH),
                            lambda b, r, i: (b, jnp.maximum(i * nblk - 1, 0), r * cpb + off + gi))

    def nxt(off):
        return pl.BlockSpec((1, BAND, GROUP_WIDTH),
                            lambda b, r, i: (b, jnp.minimum((i + 1) * nblk, n_halo - 1), r * cpb + off + gi))

    idx = jnp.asarray(_band_bucket_index(dil))
    tbl = rel_bias[:, gi * HEADS_PER_GROUP:(gi + 1) * HEADS_PER_GROUP].T.astype(F32)
    o, lse = pl.pallas_call(
        functools.partial(_attn_kernel, tq=tq, m_len=m_len),
        grid=(bsz, dil, m_len // tq),
        in_specs=[
            pl.BlockSpec((2 * BAND, 4 * BAND), lambda b, r, i: (0, 0)),
            pl.BlockSpec(memory_space=pltpu.SMEM),
            cur(0), prev(3), cur(3), nxt(3), prev(6), cur(6), nxt(6),
        ],
        out_specs=[
            pl.BlockSpec((1, tq, GROUP_WIDTH), lambda b, r, i: (b, i, r)),
            pl.BlockSpec((1, tq, HEAD_DIM), lambda b, r, i: (b, i, r)),
        ],
        out_shape=[
            jax.ShapeDtypeStruct((bsz, m_len, dil * GROUP_WIDTH), F32),
            jax.ShapeDtypeStruct((bsz, m_len, dil * HEAD_DIM), F32),
        ],
        scratch_shapes=[
            pltpu.VMEM((tq + 2 * BAND, GROUP_WIDTH), BF16),
            pltpu.VMEM((tq + 2 * BAND, GROUP_WIDTH), BF16),
            pltpu.VMEM((HEADS_PER_GROUP, 2 * BAND, 4 * BAND), F32),
        ],
        compiler_params=_cparams(("arbitrary", "arbitrary", "arbitrary")),
        name=f"band_attn_g{gi}",
    )(idx, tbl, view, view, view, view, view, view, view)
    return o.reshape(bsz * seq, GROUP_WIDTH), lse.reshape(bsz * seq, HEAD_DIM)


def _short_conv_kernel(c_ref, p_ref, n_ref, w_ref, b_ref, o_ref, *, tl):
    i = pl.program_id(1)
    last = pl.num_programs(1) - 1
    cur = c_ref[0].astype(F32)
    prev_row = jnp.where(i == 0, 0.0, p_ref[0, SUB - 1:SUB, :].astype(F32))
    next_row = jnp.where(i == last, 0.0, n_ref[0, 0:1, :].astype(F32))
    row = lax.broadcasted_iota(jnp.int32, cur.shape, 0)
    up = jnp.where(row == 0, prev_row, pltpu.roll(cur, 1, 0))
    dn = jnp.where(row == tl - 1, next_row, pltpu.roll(cur, tl - 1, 0))
    w = w_ref[...]
    o_ref[0] = up * w[0:1, :] + cur * w[1:2, :] + dn * w[2:3, :] + b_ref[...]


def _short_conv(hyg3, w_short, b_short, tl=512, ct=512):
    bsz, seq, _ = hyg3.shape
    width = w_short.shape[1]
    nsub = tl // SUB
    return pl.pallas_call(
        functools.partial(_short_conv_kernel, tl=tl),
        grid=(bsz, seq // tl, width // ct),
        in_specs=[
            pl.BlockSpec((1, tl, ct), lambda b, i, c: (b, i, c)),
            pl.BlockSpec((1, SUB, ct), lambda b, i, c: (b, jnp.maximum(i * nsub - 1, 0), c)),
            pl.BlockSpec((1, SUB, ct), lambda b, i, c: (b, jnp.minimum((i + 1) * nsub, seq // SUB - 1), c)),
            pl.BlockSpec((3, ct), lambda b, i, c: (0, c)),
            pl.BlockSpec((1, ct), lambda b, i, c: (0, c)),
        ],
        out_specs=pl.BlockSpec((1, tl, ct), lambda b, i, c: (b, i, c)),
        out_shape=jax.ShapeDtypeStruct((bsz, seq, width), F32),
        compiler_params=_cparams(("parallel", "parallel", "parallel")),
        name="short_conv",
    )(hyg3, hyg3, hyg3, w_short, b_short)


def _filter_kernel(w1_ref, b1_ref, w2_ref, b2_ref, w3_ref, b3_ref, fr_ref, w4f_ref, w4b_ref,
                   ad_ref, o_ref, *, seq, tr):
    hp = lax.Precision.HIGHEST
    n = pl.program_id(0) * tr + lax.broadcasted_iota(jnp.int32, (tr, 128), 0)
    lag = jnp.where(n < seq, n, 2 * seq - n).astype(F32)
    t = lag / (seq - 1)
    ang = (2.0 * math.pi) * lag / seq
    f = lax.broadcasted_iota(jnp.int32, (tr, 128), 1)
    band_idx = jnp.where(f <= FILTER_BANDS, f - 1, f - 1 - FILTER_BANDS).astype(F32)
    bands = 1e-4 + band_idx * ((FILTER_BANDS - 1 - 1e-4) / (FILTER_BANDS - 1))
    arg = bands * ang
    z = jnp.where(f == 0, t,
                  jnp.where(f <= FILTER_BANDS, jnp.cos(arg),
                            jnp.where(f <= 2 * FILTER_BANDS, -jnp.sin(arg), 0.0)))
    fr = fr_ref[...]
    h = jnp.sin(fr * (jnp.dot(z, w1_ref[...], precision=hp, preferred_element_type=F32) + b1_ref[...]))
    h = jnp.sin(fr * (jnp.dot(h, w2_ref[...], precision=hp, preferred_element_type=F32) + b2_ref[...]))
    h = jnp.sin(fr * (jnp.dot(h, w3_ref[...], precision=hp, preferred_element_type=F32) + b3_ref[...]))
    of = jnp.dot(h, w4f_ref[...], precision=hp, preferred_element_type=F32)
    ob = jnp.dot(h, w4b_ref[...], precision=hp, preferred_element_type=F32)
    decay = jnp.exp(-t[:, 0:1] * ad_ref[...])
    nrow = n[:, 0:1]
    o_ref[...] = jnp.where(nrow < seq, of, jnp.where(nrow > seq, ob, 0.0)) * decay


def _filter_taps(seq, w1, b1, w2, b2, w3, b3, w4, freq, tr=256):
    c = HYENA_WIDTH
    hid = FILTER_HIDDEN
    w1p = jnp.zeros((128, hid), F32).at[:w1.shape[0]].set(w1)
    w4r = w4.reshape(hid, HYENA_ORDER, 2, c)
    w4f = w4r[:, :, 0, :].reshape(hid, HYENA_ORDER * c)
    w4b = w4r[:, :, 1, :].reshape(hid, HYENA_ORDER * c)
    min_decay = math.log(DECAY_TARGET) / SLOW_DECAY_PCT
    max_decay = math.log(DECAY_TARGET) / FAST_DECAY_PCT
    ad = jnp.abs(jnp.linspace(min_decay, max_decay, c, dtype=F32))
    ad = jnp.tile(ad, HYENA_ORDER)[None, :]
    row = lambda v: v.reshape(1, -1)
    full = lambda shape: pl.BlockSpec(shape, lambda i: (0,) * len(shape))
    return pl.pallas_call(
        functools.partial(_filter_kernel, seq=seq, tr=tr),
        grid=(2 * seq // tr,),
        in_specs=[full((128, hid)), full((1, hid)), full((hid, hid)), full((1, hid)),
                  full((hid, hid)), full((1, hid)), full((1, hid)),
                  full((hid, HYENA_ORDER * c)), full((hid, HYENA_ORDER * c)), full((1, HYENA_ORDER * c))],
        out_specs=pl.BlockSpec((tr, HYENA_ORDER * c), lambda i: (i, 0)),
        out_shape=jax.ShapeDtypeStruct((2 * seq, HYENA_ORDER * c), F32),
        compiler_params=_cparams(("parallel",)),
        name="hyena_filter_taps",
    )(w1p, row(b1), w2, row(b2), w3, row(b3), row(freq), w4f, w4b, ad)


def _dft_matrices(n1_in, n1_out, n_alpha):
    big = RADIX8 * FFT_N
    a = jnp.arange(n_alpha, dtype=jnp.int32).reshape(-1, 1, 1, 1, 1)
    j = jnp.arange(N_CHUNK, dtype=jnp.int32).reshape(1, -1, 1, 1, 1)
    k1 = jnp.arange(FFT_N1, dtype=jnp.int32).reshape(1, 1, -1, 1, 1)
    t = jnp.arange(SUB, dtype=jnp.int32).reshape(1, 1, 1, 1, -1)

    def stage_a(n1_count, inverse):
        n1 = jnp.arange(n1_count, dtype=jnp.int32).reshape(1, 1, 1, -1, 1)
        n2 = SUB * j + t
        e = (RADIX8 * FFT_N2 * n1 * k1 + RADIX8 * n2 * k1 + a * (FFT_N2 * n1 + n2)) % big
        ang = e.astype(F32) * (2.0 * math.pi / big)
        cr = jnp.cos(ang)
        ci = jnp.sin(ang) if inverse else -jnp.sin(ang)
        if inverse:
            cr = cr / FFT_N
            ci = ci / FFT_N
        eye = jnp.eye(SUB, dtype=F32)
        if not inverse:
            blk = lambda m: jnp.einsum('ajknt,ts->ajksnt', m, eye)
            top = jnp.concatenate([blk(cr), blk(-ci)], axis=4)
            bot = jnp.concatenate([blk(ci), blk(cr)], axis=4)
            m = jnp.stack([top, bot], axis=2)
            return m.reshape(n_alpha, N_CHUNK, 2 * FFT_N1 * SUB, 2 * n1_count * SUB).astype(BF16)
        blk = lambda m: jnp.einsum('ajknt,ts->ajnskt', m, eye)
        top = jnp.concatenate([blk(cr), blk(-ci)], axis=4)
        bot = jnp.concatenate([blk(ci), blk(cr)], axis=4)
        m = jnp.stack([top, bot], axis=2)
        return m.reshape(n_alpha, N_CHUNK, 2 * n1_count * SUB, 2 * FFT_N1 * SUB).astype(BF16)

    kf = stage_a(n1_in, False)
    ki = stage_a(n1_out, True)
    n2 = jnp.arange(FFT_N2, dtype=jnp.int32)
    e = (n2[:, None] * n2[None, :]) % FFT_N2
    ang = e.astype(F32) * (2.0 * math.pi / FFT_N2)
    gr, gi = jnp.cos(ang), -jnp.sin(ang)
    g = jnp.block([[gr, -gi], [gi, gr]]).astype(BF16)
    ginv = jnp.block([[gr, gi], [-gi, gr]]).astype(BF16)
    return kf, g, ginv, ki


def _fft_stage_a(z_ref, kf, s1_ref, n1_in):
    def body(j, carry):
        off = pl.multiple_of(j * SUB, SUB)
        chunks = [z_ref[p, pl.ds(FFT_N2 * n1 + off, SUB), :].astype(F32)
                  for p in range(2) for n1 in range(n1_in)]
        slab = jnp.concatenate(chunks, axis=0).astype(BF16)
        a = jnp.dot(kf[j], slab, preferred_element_type=F32)
        for q in range(2):
            for k1 in range(FFT_N1):
                r = (q * FFT_N1 + k1) * SUB
                s1_ref[pl.ds(k1 * 2 * FFT_N2 + q * FFT_N2 + off, SUB), :] = a[r:r + SUB, :]
        return carry
    lax.fori_loop(0, N_CHUNK, body, 0)


def _fft_core_kernel(z_ref, kf_ref, g_ref, gi_ref, ki_ref, h_ref, o_ref, s1_ref, *, n1_in, n1_out):
    _fft_stage_a(z_ref, kf_ref, s1_ref, n1_in)

    def stage_b(k1, carry):
        r0 = pl.multiple_of(k1 * 2 * FFT_N2, 2 * FFT_N2)
        slab = s1_ref[pl.ds(r0, 2 * FFT_N2), :].astype(BF16)
        x = jnp.dot(g_ref[...], slab, preferred_element_type=F32)
        xr, xi = x[:FFT_N2], x[FFT_N2:]
        hr = h_ref[pl.ds(r0, FFT_N2), :]
        hi = h_ref[pl.ds(r0 + FFT_N2, FFT_N2), :]
        y = jnp.concatenate([xr * hr - xi * hi, xr * hi + xi * hr], axis=0).astype(BF16)
        s1_ref[pl.ds(r0, 2 * FFT_N2), :] = jnp.dot(gi_ref[...], y, preferred_element_type=F32)
        return carry
    lax.fori_loop(0, FFT_N1, stage_b, 0)

    def stage_c(j, carry):
        off = pl.multiple_of(j * SUB, SUB)
        chunks = [s1_ref[pl.ds(k1 * 2 * FFT_N2 + q * FFT_N2 + off, SUB), :]
                  for q in range(2) for k1 in range(FFT_N1)]
        slab = jnp.concatenate(chunks, axis=0).astype(BF16)
        y = jnp.dot(ki_ref[j], slab, preferred_element_type=F32)
        for p in range(2):
            for n1 in range(n1_out):
                r = (p * n1_out + n1) * SUB
                o_ref[p, pl.ds(FFT_N2 * n1 + off, SUB), :] = y[r:r + SUB, :]
        return carry
    lax.fori_loop(0, N_CHUNK, stage_c, 0)


def _fft_forward_kernel(z_ref, kf_ref, g_ref, o_ref, s1_ref, *, n1_in):
    _fft_stage_a(z_ref, kf_ref, s1_ref, n1_in)

    def stage_b(k1, carry):
        r0 = pl.multiple_of(k1 * 2 * FFT_N2, 2 * FFT_N2)
        slab = s1_ref[pl.ds(r0, 2 * FFT_N2), :].astype(BF16)
        o_ref[pl.ds(r0, 2 * FFT_N2), :] = jnp.dot(g_ref[...], slab, preferred_element_type=F32)
        return carry
    lax.fori_loop(0, FFT_N1, stage_b, 0)


def _const_spec(shape, index_map):
    return pl.BlockSpec(shape, index_map, pipeline_mode=pl.Buffered(1))


def _fft_conv_pairs(z3, col0, spec, mats, ct):
    kf, g, ginv, ki = mats
    bsz, seq, _ = z3.shape
    c = spec.shape[1]
    n1 = seq // FFT_N2
    cb0 = col0 // ct
    return pl.pallas_call(
        functools.partial(_fft_core_kernel, n1_in=n1, n1_out=n1),
        grid=(c // ct, bsz // 2),
        in_specs=[
            pl.BlockSpec((2, seq, ct), lambda cc, p: (p, 0, cb0 + cc)),
            _const_spec((None,) + kf.shape[1:], lambda cc, p: (0, 0, 0, 0)),
            _const_spec(g.shape, lambda cc, p: (0, 0)),
            _const_spec(ginv.shape, lambda cc, p: (0, 0)),
            _const_spec((None,) + ki.shape[1:], lambda cc, p: (0, 0, 0, 0)),
            _const_spec((SPEC_ROWS, ct), lambda cc, p: (0, cc)),
        ],
        out_specs=pl.BlockSpec((2, seq, ct), lambda cc, p: (p, 0, cc)),
        out_shape=jax.ShapeDtypeStruct((bsz, seq, c), F32),
        scratch_shapes=[pltpu.VMEM((SPEC_ROWS, ct), F32)],
        compiler_params=_cparams(("arbitrary", "arbitrary")),
        name="fft_conv_pairs",
    )(z3, kf, g, ginv, ki, spec)


def _fft_conv_residues(u, spec, mats, ct):
    kf, g, ginv, ki = mats
    c = u.shape[-1]
    return pl.pallas_call(
        functools.partial(_fft_core_kernel, n1_in=FFT_N1, n1_out=FFT_N1),
        grid=(N_ALPHA, c // ct),
        in_specs=[
            pl.BlockSpec((None, 2, FFT_N, ct), lambda a, cc: (a, 0, 0, cc)),
            _const_spec((None,) + kf.shape[1:], lambda a, cc: (a, 0, 0, 0)),
            _const_spec(g.shape, lambda a, cc: (0, 0)),
            _const_spec(ginv.shape, lambda a, cc: (0, 0)),
            _const_spec((None,) + ki.shape[1:], lambda a, cc: (a, 0, 0, 0)),
            pl.BlockSpec((None, SPEC_ROWS, ct), lambda a, cc: (a, 0, cc)),
        ],
        out_specs=pl.BlockSpec((None, 2, FFT_N, ct), lambda a, cc: (a, 0, 0, cc)),
        out_shape=jax.ShapeDtypeStruct(u.shape, F32),
        scratch_shapes=[pltpu.VMEM((SPEC_ROWS, ct), F32)],
        compiler_params=_cparams(("arbitrary", "arbitrary")),
        name="fft_conv_residues",
    )(u, kf, g, ginv, ki, spec)


def _fft_forward(u, kf, g, ct):
    n_a, _, _, c = u.shape
    return pl.pallas_call(
        functools.partial(_fft_forward_kernel, n1_in=FFT_N1),
        grid=(n_a, c // ct),
        in_specs=[
            pl.BlockSpec((None, 2, FFT_N, ct), lambda a, cc: (a, 0, 0, cc)),
            _const_spec((None,) + kf.shape[1:], lambda a, cc: (a, 0, 0, 0)),
            _const_spec(g.shape, lambda a, cc: (0, 0)),
        ],
        out_specs=pl.BlockSpec((None, SPEC_ROWS, ct), lambda a, cc: (a, 0, cc)),
        out_shape=jax.ShapeDtypeStruct((n_a, SPEC_ROWS, c), F32),
        scratch_shapes=[pltpu.VMEM((SPEC_ROWS, ct), F32)],
        compiler_params=_cparams(("arbitrary", "arbitrary")),
        name="fft_forward",
    )(u, kf, g)


def _radix8_coefs(n_blocks):
    a = np.arange(n_blocks)[None, :]
    al = np.arange(N_ALPHA)[:, None]
    ang = 2.0 * np.pi * ((a * al) % RADIX8) / RADIX8
    return np.round(np.cos(ang), 12), np.round(-np.sin(ang), 12)


def _radix8_split_kernel(x_ref, o_ref, *, n_blocks):
    cr, ci = _radix8_coefs(n_blocks)
    xs = [x_ref[a] for a in range(n_blocks)]
    for al in range(N_ALPHA):
        for part, coef in ((0, cr), (1, ci)):
            acc = None
            for a in range(n_blocks):
                w = float(coef[al, a])
                if w == 0.0:
                    continue
                term = xs[a] if w == 1.0 else (-xs[a] if w == -1.0 else xs[a] * w)
                acc = term if acc is None else acc + term
            o_ref[al, part] = jnp.zeros_like(xs[0]) if acc is None else acc


def _radix8_split(x, col0, width, n_blocks, tr=256, ct=512):
    xv = x.reshape(n_blocks, FFT_N, x.shape[1])
    cb0 = col0 // ct
    return pl.pallas_call(
        functools.partial(_radix8_split_kernel, n_blocks=n_blocks),
        grid=(FFT_N // tr, width // ct),
        in_specs=[pl.BlockSpec((n_blocks, tr, ct), lambda i, c: (0, i, cb0 + c))],
        out_specs=pl.BlockSpec((N_ALPHA, 2, tr, ct), lambda i, c: (0, 0, i, c)),
        out_shape=jax.ShapeDtypeStruct((N_ALPHA, 2, FFT_N, width), F32),
        compiler_params=_cparams(("parallel", "parallel")),
        name="radix8_split",
    )(xv)


def _radix8_merge_gate_kernel(v_ref, z_ref, xg_ref, sk_ref, o_ref, *, n_blocks):
    sk = sk_ref[...]
    for a in range(n_blocks):
        acc = None
        for al in range(N_ALPHA):
            cw = (1.0 if al in (0, RADIX8 // 2) else 2.0) / RADIX8
            ang = 2.0 * np.pi * ((a * al) % RADIX8) / RADIX8
            wr = float(np.round(np.cos(ang), 12)) * cw
            wi = float(np.round(np.sin(ang), 12)) * cw
            for w, part in ((wr, 0), (-wi, 1)):
                if w == 0.0:
                    continue
                term = v_ref[al, part] * w
                acc = term if acc is None else acc + term
        o_ref[a] = xg_ref[a] * (acc + sk * z_ref[a])


def _radix8_merge_gate(v, z, zcol0, xg, gcol0, skip_row, n_blocks, tr=256, ct=512):
    c = v.shape[-1]
    zv = z.reshape(n_blocks, FFT_N, z.shape[1])
    gv = xg.reshape(n_blocks, FFT_N, xg.shape[1])
    zb, gb = zcol0 // ct, gcol0 // ct
    out = pl.pallas_call(
        functools.partial(_radix8_merge_gate_kernel, n_blocks=n_blocks),
        grid=(FFT_N // tr, c // ct),
        in_specs=[
            pl.BlockSpec((N_ALPHA, 2, tr, ct), lambda i, cc: (0, 0, i, cc)),
            pl.BlockSpec((n_blocks, tr, ct), lambda i, cc: (0, i, zb + cc)),
            pl.BlockSpec((n_blocks, tr, ct), lambda i, cc: (0, i, gb + cc)),
            pl.BlockSpec((1, ct), lambda i, cc: (0, cc)),
        ],
        out_specs=pl.BlockSpec((n_blocks, tr, ct), lambda i, cc: (0, i, cc)),
        out_shape=jax.ShapeDtypeStruct((n_blocks, FFT_N, c), F32),
        compiler_params=_cparams(("parallel", "parallel")),
        name="radix8_merge_gate",
    )(v, zv, gv, skip_row)
    return out.reshape(n_blocks * FFT_N, c)


def _gate_kernel(y_ref, z_ref, xg_ref, sk_ref, o_ref):
    o_ref[...] = xg_ref[...] * (y_ref[...] + sk_ref[...] * z_ref[...])


def _gate(y, z, zcol0, xg, gcol0, skip_row, tr=512, ct=512):
    t, c = y.shape
    zb, gb = zcol0 // ct, gcol0 // ct
    return pl.pallas_call(
        _gate_kernel,
        grid=(t // tr, c // ct),
        in_specs=[
            pl.BlockSpec((tr, ct), lambda i, cc: (i, cc)),
            pl.BlockSpec((tr, ct), lambda i, cc: (i, zb + cc)),
            pl.BlockSpec((tr, ct), lambda i, cc: (i, gb + cc)),
            pl.BlockSpec((1, ct), lambda i, cc: (0, cc)),
        ],
        out_specs=pl.BlockSpec((tr, ct), lambda i, cc: (i, cc)),
        out_shape=jax.ShapeDtypeStruct((t, c), F32),
        compiler_params=_cparams(("parallel", "parallel")),
        name="hyena_gate",
    )(y, z, xg, skip_row)


def _hyena(hyg3, w_short, b_short, filt, filt_skip):
    bsz, seq, _ = hyg3.shape
    c = HYENA_WIDTH
    hc = _short_conv(hyg3, w_short, b_short)
    taps = _filter_taps(seq, *filt)
    hc2 = hc.reshape(bsz * seq, 3 * c)
    if 2 * seq == FFT_N:
        mats = _dft_matrices(seq // FFT_N2, seq // FFT_N2, 1)
        mats_full = _dft_matrices(FFT_N1, FFT_N1, 1)
        u = jnp.stack([taps, jnp.zeros_like(taps)], axis=0)[None]
        spec = _fft_forward(u, mats_full[0], mats_full[1], 256)[0]
        z = hc
        zcol = 0
        for n in range(HYENA_ORDER):
            y = _fft_conv_pairs(z, zcol, spec[:, n * c:(n + 1) * c], mats, 256)
            z2 = _gate(y.reshape(bsz * seq, c), z.reshape(bsz * seq, -1), zcol, hc2, (n + 1) * c,
                       filt_skip[n][None, :])
            z = z2.reshape(bsz, seq, c)
            zcol = 0
        return z.reshape(bsz * seq, c)
    assert bsz == 1 and 2 * seq == RADIX8 * FFT_N
    n_blocks = seq // FFT_N
    mats = _dft_matrices(FFT_N1, FFT_N1, N_ALPHA)
    tap_res = _radix8_split(taps, 0, HYENA_ORDER * c, RADIX8)
    spec = _fft_forward(tap_res, mats[0], mats[1], 256)
    z = hc2
    zcol = 0
    for n in range(HYENA_ORDER):
        u = _radix8_split(z, zcol, c, n_blocks)
        v = _fft_conv_residues(u, spec[:, :, n * c:(n + 1) * c], mats, 128)
        z = _radix8_merge_gate(v, z, zcol, hc2, (n + 1) * c, filt_skip[n][None, :], n_blocks)
        zcol = 0
    return z


def _merge_kernel(o0_ref, o1_ref, o2_ref, l0_ref, l1_ref, l2_ref, zh_ref, ga_ref, gh_ref, x_ref,
                  wa_ref, wh_ref, wo_ref, out_ref):
    l0, l1, l2 = l0_ref[...], l1_ref[...], l2_ref[...]
    m = jnp.maximum(jnp.maximum(l0, l1), l2)
    e0, e1, e2 = jnp.exp(l0 - m), jnp.exp(l1 - m), jnp.exp(l2 - m)
    tot = e0 + e1 + e2
    w0, w1, w2 = e0 / tot, e1 / tot, e2 / tot
    heads = []
    for h in range(HEADS_PER_GROUP):
        hs = slice(h * HEAD_DIM, (h + 1) * HEAD_DIM)
        heads.append(o0_ref[:, hs] * w0[:, h:h + 1] + o1_ref[:, hs] * w1[:, h:h + 1]
                     + o2_ref[:, hs] * w2[:, h:h + 1])
    attn = jnp.concatenate(heads, axis=1).astype(BF16)
    ab = jnp.dot(attn, wa_ref[...], preferred_element_type=F32)
    hb = jnp.dot(zh_ref[...].astype(BF16), wh_ref[...], preferred_element_type=F32)
    merged = jax.nn.sigmoid(ga_ref[...]) * ab + jax.nn.sigmoid(gh_ref[...]) * hb
    out_ref[...] = x_ref[...] + jnp.dot(merged.astype(BF16), wo_ref[...], preferred_element_type=F32)


def _merge(outs, lses, zh, hyg, x2d, wa, wh, wo, tm=512):
    t, d = x2d.shape
    tok = lambda w: pl.BlockSpec((tm, w), lambda i: (i, 0))
    full = lambda a: pl.BlockSpec(a.shape, lambda i: (0, 0))
    gcol = 3 * HYENA_WIDTH // d
    return pl.pallas_call(
        _merge_kernel,
        grid=(t // tm,),
        in_specs=[tok(GROUP_WIDTH)] * 3 + [tok(HEAD_DIM)] * 3 + [
            tok(HYENA_WIDTH),
            pl.BlockSpec((tm, d), lambda i: (i, gcol)),
            pl.BlockSpec((tm, d), lambda i: (i, gcol + 1)),
            tok(d), full(wa), full(wh), full(wo)],
        out_specs=tok(d),
        out_shape=jax.ShapeDtypeStruct((t, d), F32),
        compiler_params=_cparams(("parallel",)),
        name="merge_out_proj",
    )(*outs, *lses, zh, hyg, hyg, x2d, wa, wh, wo)


def _mlp_kernel(x_ref, g_ref, w1_ref, w2_ref, o_ref, xn_ref, acc_ref):
    j = pl.program_id(1)

    @pl.when(j == 0)
    def _():
        x = x_ref[...]
        ms = jnp.mean(x * x, axis=-1, keepdims=True)
        xn_ref[...] = (x * lax.rsqrt(ms + NORM_EPS) * g_ref[...]).astype(BF16)
        acc_ref[...] = jnp.zeros_like(acc_ref)

    h = jnp.dot(xn_ref[...], w1_ref[...], preferred_element_type=F32)
    a = jnp.square(jnp.maximum(h, 0.0)).astype(BF16)
    acc_ref[...] += jnp.dot(a, w2_ref[...], preferred_element_type=F32)

    @pl.when(j == pl.num_programs(1) - 1)
    def _():
        o_ref[...] = x_ref[...] + acc_ref[...]


def _mlp(x2d, g, w1, w2, tm=1024, tf=512):
    t, d = x2d.shape
    f = w1.shape[1]
    return pl.pallas_call(
        _mlp_kernel,
        grid=(t // tm, f // tf),
        in_specs=[
            pl.BlockSpec((tm, d), lambda i, j: (i, 0)),
            pl.BlockSpec((1, d), lambda i, j: (0, 0)),
            pl.BlockSpec((d, tf), lambda i, j: (0, j)),
            pl.BlockSpec((tf, d), lambda i, j: (j, 0)),
        ],
        out_specs=pl.BlockSpec((tm, d), lambda i, j: (i, 0)),
        out_shape=jax.ShapeDtypeStruct((t, d), F32),
        scratch_shapes=[pltpu.VMEM((tm, d), BF16), pltpu.VMEM((tm, d), F32)],
        compiler_params=_cparams(("parallel", "arbitrary")),
        name="mlp",
    )(x2d, g, w1, w2)


def _encoder_layer(x, p):
    bsz, seq, d = x.shape
    x2d = x.reshape(bsz * seq, d)
    qkv = _norm_matmul(x2d, p['g_mix'], p['w_qkv'], p['head_gain'], 2 * ATTN_WIDTH // 512, BF16)
    hyg = _norm_matmul(x2d, p['g_mix'], p['w_hyg'], p['hyg_gain'], 0, F32)
    outs, lses = [], []
    for gi in range(N_GROUPS):
        o, s = _attn_group(qkv, p['rel_bias'], gi, bsz, seq)
        outs.append(o)
        lses.append(s)
    zh = _hyena(hyg.reshape(bsz, seq, HYG_WIDTH), p['w_short'], p['b_short'], p['filt'], p['filt_skip'])
    x1 = _merge(outs, lses, zh, hyg, x2d, p['wa'], p['wh'], p['wo'])
    y = _mlp(x1, p['g_mlp'], p['w1'], p['w2'])
    return y.reshape(bsz, seq, d)


def kernel(x_prompt, x_sample, rel_bias, g_mix, w_in, g_q, g_k, w_attn_branch, w_short, b_short,
           filt_w1, filt_b1, filt_w2, filt_b2, filt_w3, filt_b3, filt_w4, filt_freq, filt_skip,
           w_hyena_branch, w_out, g_mlp, w_ff1, w_ff2):
    y_prompt, y_sample = x_prompt, x_sample
    n_heads = N_GROUPS * HEADS_PER_GROUP
    for l in range(g_mix.shape[0]):
        w_in_b = w_in[l].astype(BF16)
        head_gain = jnp.concatenate([
            jnp.tile(g_q[l].astype(F32) * (HEAD_DIM ** -0.5), n_heads),
            jnp.tile(g_k[l].astype(F32), n_heads),
            jnp.ones((ATTN_WIDTH,), F32)])[None, :]
        p = dict(
            g_mix=g_mix[l][None, :].astype(F32),
            w_qkv=w_in_b[:, :QKV_WIDTH], w_hyg=w_in_b[:, QKV_WIDTH:],
            head_gain=head_gain, hyg_gain=jnp.ones((1, HYG_WIDTH), F32),
            rel_bias=rel_bias,
            w_short=w_short[l], b_short=b_short[l][None, :],
            filt=(filt_w1[l], filt_b1[l], filt_w2[l], filt_b2[l], filt_w3[l], filt_b3[l], filt_w4[l],
                  filt_freq[l]),
            filt_skip=filt_skip[l],
            wa=w_attn_branch[l].astype(BF16), wh=w_hyena_branch[l].astype(BF16),
            wo=w_out[l].astype(BF16),
            g_mlp=g_mlp[l][None, :].astype(F32), w1=w_ff1[l].astype(BF16), w2=w_ff2[l].astype(BF16))
        y_prompt = _encoder_layer(y_prompt, p)
        y_sample = _encoder_layer(y_sample, p)
    return (y_prompt, y_sample)
```

```python
import functools
import math

import numpy as np
import jax
import jax.numpy as jnp
from jax import lax
from jax.experimental import pallas as pl
from jax.experimental.pallas import tpu as pltpu

F32 = jnp.float32
BF16 = jnp.bfloat16

D_MODEL = 1024
HEAD_DIM = 128
HEADS_PER_GROUP = 4
DILATION_GROUPS = ((128, 1), (512, 4), (2048, 16))
N_GROUPS = len(DILATION_GROUPS)
ATTN_WIDTH = N_GROUPS * HEADS_PER_GROUP * HEAD_DIM
GROUP_WIDTH = HEADS_PER_GROUP * HEAD_DIM
GROUP_QKV = 3 * GROUP_WIDTH
BAND = 64
N_BUCKETS = 32
MAX_DISTANCE = 1024
HYENA_WIDTH = D_MODEL
HYENA_ORDER = 2
FILTER_BANDS = 16
FILTER_HIDDEN = 64
FILTER_OUT_SCALE = 0.02
DECAY_TARGET = 1e-2
FAST_DECAY_PCT = 0.3
SLOW_DECAY_PCT = 1.5
D_FF = 4 * D_MODEL
QKV_WIDTH = 3 * ATTN_WIDTH
HYG_WIDTH = 3 * HYENA_WIDTH + 2 * D_MODEL
NORM_EPS = 1e-6
MASK_VALUE = -1e30

FFT_N = 4096
FFT_N1 = 32
FFT_N2 = 128
SUB = 8
LANES = 128
N_CHUNK = FFT_N2 // SUB
SPEC_ROWS = 2 * FFT_N
RADIX8 = 8
N_ALPHA = 5

VMEM_LIMIT = 56 * 1024 * 1024


def _cparams(sem):
    return pltpu.CompilerParams(dimension_semantics=sem, vmem_limit_bytes=VMEM_LIMIT)


def _norm_matmul_kernel(x_ref, g_ref, w_ref, o_ref, xn_ref):
    @pl.when(pl.program_id(1) == 0)
    def _():
        x = x_ref[...]
        ms = jnp.mean(x * x, axis=-1, keepdims=True)
        xn_ref[...] = (x * lax.rsqrt(ms + NORM_EPS) * g_ref[...]).astype(BF16)

    o_ref[...] = jnp.dot(xn_ref[...], w_ref[...], preferred_element_type=F32).astype(o_ref.dtype)


def _norm_matmul(x2d, g, w_bf16, out_dtype, tm=1024, tn=512):
    t, d = x2d.shape
    n = w_bf16.shape[1]
    return pl.pallas_call(
        _norm_matmul_kernel,
        grid=(t // tm, n // tn),
        in_specs=[
            pl.BlockSpec((tm, d), lambda i, j: (i, 0)),
            pl.BlockSpec((1, d), lambda i, j: (0, 0)),
            pl.BlockSpec((d, tn), lambda i, j: (0, j)),
        ],
        out_specs=pl.BlockSpec((tm, tn), lambda i, j: (i, j)),
        out_shape=jax.ShapeDtypeStruct((t, n), out_dtype),
        scratch_shapes=[pltpu.VMEM((tm, d), BF16)],
        compiler_params=_cparams(("parallel", "arbitrary")),
        name="norm_matmul",
    )(x2d, g, w_bf16)


def _qkv_group_kernel(x_ref, g_ref, w_ref, hg_ref, o_ref, *scratch, tm, dil):
    rows = tm // dil
    x = x_ref[...]
    ms = jnp.mean(x * x, axis=-1, keepdims=True)
    xn = x * lax.rsqrt(ms + NORM_EPS) * g_ref[...]
    if dil == 1:
        xn = xn.astype(BF16)
    else:
        s_ref, = scratch
        n_slab = xn.shape[1] // LANES
        for c in range(n_slab):
            s_ref[c * tm:(c + 1) * tm, :] = xn[:, c * LANES:(c + 1) * LANES]
        parts = []
        for r in range(dil):
            slabs = [s_ref[pl.ds(c * tm + r, rows, stride=dil), :] for c in range(n_slab)]
            parts.append(jnp.concatenate(slabs, axis=1).astype(BF16))
        xn = jnp.concatenate(parts, axis=0)
    hg = hg_ref[...]
    for jt in range(3):
        acc = jnp.dot(xn, w_ref[:, jt * GROUP_WIDTH:(jt + 1) * GROUP_WIDTH], preferred_element_type=F32)
        if jt < 2:
            heads = []
            for h in range(HEADS_PER_GROUP):
                a = acc[:, h * HEAD_DIM:(h + 1) * HEAD_DIM]
                ms = jnp.mean(a * a, axis=-1, keepdims=True)
                c0 = jt * GROUP_WIDTH + h * HEAD_DIM
                heads.append(a * lax.rsqrt(ms + NORM_EPS) * hg[:, c0:c0 + HEAD_DIM])
            acc = jnp.concatenate(heads, axis=1)
        acc = acc.astype(BF16)
        for r in range(dil):
            c0 = r * GROUP_QKV + jt * GROUP_WIDTH
            o_ref[:, c0:c0 + GROUP_WIDTH] = acc[r * rows:(r + 1) * rows, :]


def _qkv_group(x2d, g, w_g, head_gain_g, dil, tm=1024):
    t, d = x2d.shape
    return pl.pallas_call(
        functools.partial(_qkv_group_kernel, tm=tm, dil=dil),
        grid=(t // tm,),
        in_specs=[
            pl.BlockSpec((tm, d), lambda i: (i, 0)),
            pl.BlockSpec((1, d), lambda i: (0, 0)),
            pl.BlockSpec((d, GROUP_QKV), lambda i: (0, 0)),
            pl.BlockSpec((1, GROUP_QKV), lambda i: (0, 0)),
        ],
        out_specs=pl.BlockSpec((tm // dil, dil * GROUP_QKV), lambda i: (i, 0)),
        out_shape=jax.ShapeDtypeStruct((t // dil, dil * GROUP_QKV), BF16),
        scratch_shapes=[] if dil == 1 else [pltpu.VMEM((tm * d // LANES, LANES), F32)],
        compiler_params=_cparams(("parallel",)),
        name=f"qkv_group_d{dil}",
    )(x2d, g, w_g, head_gain_g)


def _t5_bucket_np(rel):
    nb = N_BUCKETS // 2
    max_exact = nb // 2
    side = np.where(rel > 0, nb, 0)
    n = np.abs(rel)
    nf = np.maximum(n, 1).astype(np.float32)
    large = max_exact + (np.log(nf / np.float32(max_exact)) / np.float32(math.log(MAX_DISTANCE / max_exact))
                         * np.float32(nb - max_exact)).astype(np.int32)
    large = np.minimum(large, nb - 1)
    return side + np.where(n < max_exact, n, large)


def _band_bucket_index(dil):
    qi = np.arange(2 * BAND)[:, None]
    kj = np.arange(4 * BAND)[None, :]
    delta = kj - BAND - qi
    idx = _t5_bucket_np(delta * dil).astype(np.int32)
    return np.where(np.abs(delta) <= BAND, idx, -1).astype(np.int32)


def _attn_kernel(idx_ref, tbl_ref, c_ref, p_ref, n_ref, o_ref, lse_ref, bias_ref, *, tq, m_len, dil):
    qb = 2 * BAND
    wb = 4 * BAND
    nsub = tq // qb
    first = (pl.program_id(0) == 0) & (pl.program_id(1) == 0)

    @pl.when(first)
    def _():
        idx = idx_ref[...]
        for h in range(HEADS_PER_GROUP):
            acc = jnp.full((qb, wb), MASK_VALUE, F32)
            for b in range(N_BUCKETS):
                acc = jnp.where(idx == b, tbl_ref[h, b], acc)
            bias_ref[h] = acc

    i = pl.program_id(1)
    col = lax.broadcasted_iota(jnp.int32, (qb, wb), 1)
    lane = lax.broadcasted_iota(jnp.int32, (qb, HEAD_DIM), 1)

    def window(s, cols):
        lo, hi = s * qb - BAND, s * qb + qb + BAND
        parts = []
        if lo < 0:
            parts.append(p_ref[0, :, cols])
            lo = 0
        parts.append(c_ref[0, lo:min(hi, tq), cols])
        if hi > tq:
            parts.append(n_ref[0, :, cols])
        return parts[0] if len(parts) == 1 else jnp.concatenate(parts, axis=0)

    for s in range(nsub):
        lo = BAND - i * tq - qb * s
        valid = (col >= lo) & (col < m_len + lo)
        for r in range(dil):
            base = r * GROUP_QKV
            lse_tile = jnp.zeros((qb, HEAD_DIM), F32)
            rows = pl.ds(s * qb * dil + r, qb, stride=dil) if dil > 1 else pl.ds(s * qb, qb)
            for h in range(HEADS_PER_GROUP):
                qs = c_ref[0, s * qb:(s + 1) * qb, base + h * HEAD_DIM:base + (h + 1) * HEAD_DIM]
                kk = window(s, slice(base + GROUP_WIDTH + h * HEAD_DIM, base + GROUP_WIDTH + (h + 1) * HEAD_DIM))
                sc = lax.dot_general(qs, kk, (((1,), (1,)), ((), ())), preferred_element_type=F32)
                sc = jnp.where(valid, sc + bias_ref[h], MASK_VALUE)
                mx = jnp.max(sc, axis=-1, keepdims=True)
                p = jnp.exp(sc - mx)
                den = jnp.sum(p, axis=-1, keepdims=True)
                vv = window(s, slice(base + 2 * GROUP_WIDTH + h * HEAD_DIM,
                                     base + 2 * GROUP_WIDTH + (h + 1) * HEAD_DIM))
                o = jnp.dot(p.astype(BF16), vv, preferred_element_type=F32) / den
                o_ref[0, h, rows, :] = o
                lse_tile = jnp.where(lane == h, mx + jnp.log(den), lse_tile)
            lse_ref[0, rows, :] = lse_tile


def _attn_group(qkv_g, rel_bias, gi, bsz, seq):
    _, dil = DILATION_GROUPS[gi]
    m_len = seq // dil
    tq = min(m_len, 512 if dil == 1 else 128)
    nblk = tq // BAND
    n_halo = m_len // BAND
    width = dil * GROUP_QKV
    view = qkv_g.reshape(bsz, m_len, width)
    idx = jnp.asarray(_band_bucket_index(dil))
    tbl = rel_bias[:, gi * HEADS_PER_GROUP:(gi + 1) * HEADS_PER_GROUP].T.astype(F32)
    o, lse = pl.pallas_call(
        functools.partial(_attn_kernel, tq=tq, m_len=m_len, dil=dil),
        grid=(bsz, m_len // tq),
        in_specs=[
            pl.BlockSpec((2 * BAND, 4 * BAND), lambda b, i: (0, 0)),
            pl.BlockSpec(memory_space=pltpu.SMEM),
            pl.BlockSpec((1, tq, width), lambda b, i: (b, i, 0)),
            pl.BlockSpec((1, BAND, width), lambda b, i: (b, jnp.maximum(i * nblk - 1, 0), 0)),
            pl.BlockSpec((1, BAND, width), lambda b, i: (b, jnp.minimum((i + 1) * nblk, n_halo - 1), 0)),
        ],
        out_specs=[
            pl.BlockSpec((1, HEADS_PER_GROUP, tq * dil, HEAD_DIM), lambda b, i: (b, 0, i, 0)),
            pl.BlockSpec((1, tq * dil, HEAD_DIM), lambda b, i: (b, i, 0)),
        ],
        out_shape=[
            jax.ShapeDtypeStruct((bsz, HEADS_PER_GROUP, seq, HEAD_DIM), F32),
            jax.ShapeDtypeStruct((bsz, seq, HEAD_DIM), F32),
        ],
        scratch_shapes=[pltpu.VMEM((HEADS_PER_GROUP, 2 * BAND, 4 * BAND), F32)],
        compiler_params=_cparams(("arbitrary", "arbitrary")),
        name=f"band_attn_g{gi}",
    )(idx, tbl, view, view, view)
    return o, lse.reshape(bsz * seq, HEAD_DIM)


def _short_conv_kernel(c_ref, p_ref, n_ref, w_ref, b_ref, o_ref, *, tl):
    i = pl.program_id(1)
    last = pl.num_programs(1) - 1
    cur = c_ref[0].astype(F32)
    prev_row = jnp.where(i == 0, 0.0, p_ref[0, SUB - 1:SUB, :].astype(F32))
    next_row = jnp.where(i == last, 0.0, n_ref[0, 0:1, :].astype(F32))
    row = lax.broadcasted_iota(jnp.int32, cur.shape, 0)
    up = jnp.where(row == 0, prev_row, pltpu.roll(cur, 1, 0))
    dn = jnp.where(row == tl - 1, next_row, pltpu.roll(cur, tl - 1, 0))
    w = w_ref[...]
    o_ref[0] = up * w[0:1, :] + cur * w[1:2, :] + dn * w[2:3, :] + b_ref[...]


def _short_conv(hyg3, w_short, b_short, tl=512, ct=512):
    bsz, seq, _ = hyg3.shape
    width = w_short.shape[1]
    nsub = tl // SUB
    return pl.pallas_call(
        functools.partial(_short_conv_kernel, tl=tl),
        grid=(bsz, seq // tl, width // ct),
        in_specs=[
            pl.BlockSpec((1, tl, ct), lambda b, i, c: (b, i, c)),
            pl.BlockSpec((1, SUB, ct), lambda b, i, c: (b, jnp.maximum(i * nsub - 1, 0), c)),
            pl.BlockSpec((1, SUB, ct), lambda b, i, c: (b, jnp.minimum((i + 1) * nsub, seq // SUB - 1), c)),
            pl.BlockSpec((3, ct), lambda b, i, c: (0, c)),
            pl.BlockSpec((1, ct), lambda b, i, c: (0, c)),
        ],
        out_specs=pl.BlockSpec((1, tl, ct), lambda b, i, c: (b, i, c)),
        out_shape=jax.ShapeDtypeStruct((bsz, seq, width), F32),
        compiler_params=_cparams(("parallel", "parallel", "parallel")),
        name="short_conv",
    )(hyg3, hyg3, hyg3, w_short, b_short)


def _filter_kernel(w1_ref, b1_ref, w2_ref, b2_ref, w3_ref, b3_ref, fr_ref, w4_ref,
                   ad_ref, o_ref, *, seq, tr):
    hp = lax.Precision.HIGHEST
    n = pl.program_id(0) * tr + lax.broadcasted_iota(jnp.int32, (tr, 128), 0)
    lag = jnp.where(n < seq, n, 2 * seq - n).astype(F32)
    t = lag / (seq - 1)
    ang = (2.0 * math.pi) * lag / seq
    f = lax.broadcasted_iota(jnp.int32, (tr, 128), 1)
    band_idx = jnp.where(f <= FILTER_BANDS, f - 1, f - 1 - FILTER_BANDS).astype(F32)
    bands = 1e-4 + band_idx * ((FILTER_BANDS - 1 - 1e-4) / (FILTER_BANDS - 1))
    arg = bands * ang
    z = jnp.where(f == 0, t,
                  jnp.where(f <= FILTER_BANDS, jnp.cos(arg),
                            jnp.where(f <= 2 * FILTER_BANDS, -jnp.sin(arg), 0.0)))
    fr = fr_ref[...]
    h = jnp.sin(fr * (jnp.dot(z, w1_ref[...], precision=hp, preferred_element_type=F32) + b1_ref[...]))
    h = jnp.sin(fr * (jnp.dot(h, w2_ref[...], precision=hp, preferred_element_type=F32) + b2_ref[...]))
    h = jnp.sin(fr * (jnp.dot(h, w3_ref[...], precision=hp, preferred_element_type=F32) + b3_ref[...]))
    taps = jnp.dot(h, w4_ref[...], precision=hp, preferred_element_type=F32)
    decay = jnp.exp(-t[:, 0:1] * ad_ref[...])
    o_ref[...] = jnp.where(n[:, 0:1] == seq, 0.0, taps * decay)


def _filter_taps(seq, w1, b1, w2, b2, w3, b3, w4, freq, tr=256):
    c = HYENA_WIDTH
    hid = FILTER_HIDDEN
    w1p = jnp.zeros((128, hid), F32).at[:w1.shape[0]].set(w1)
    w4r = w4.reshape(hid, HYENA_ORDER, 2, c)
    w4d = jnp.transpose(w4r, (2, 0, 1, 3)).reshape(2, hid, HYENA_ORDER * c)
    nblk = 2 * seq // tr
    min_decay = math.log(DECAY_TARGET) / SLOW_DECAY_PCT
    max_decay = math.log(DECAY_TARGET) / FAST_DECAY_PCT
    ad = jnp.abs(jnp.linspace(min_decay, max_decay, c, dtype=F32))
    ad = jnp.tile(ad, HYENA_ORDER)[None, :]
    row = lambda v: v.reshape(1, -1)
    full = lambda shape: pl.BlockSpec(shape, lambda i: (0,) * len(shape))
    return pl.pallas_call(
        functools.partial(_filter_kernel, seq=seq, tr=tr),
        grid=(nblk,),
        in_specs=[full((128, hid)), full((1, hid)), full((hid, hid)), full((1, hid)),
                  full((hid, hid)), full((1, hid)), full((1, hid)),
                  pl.BlockSpec((None, hid, HYENA_ORDER * c), lambda i: (i // (nblk // 2), 0, 0)),
                  full((1, HYENA_ORDER * c))],
        out_specs=pl.BlockSpec((tr, HYENA_ORDER * c), lambda i: (i, 0)),
        out_shape=jax.ShapeDtypeStruct((2 * seq, HYENA_ORDER * c), F32),
        compiler_params=_cparams(("parallel",)),
        name="hyena_filter_taps",
    )(w1p, row(b1), w2, row(b2), w3, row(b3), row(freq), w4d, ad)


@functools.lru_cache(maxsize=None)
def _dft_matrices(n1_in, n1_out, n_alpha):
    big = RADIX8 * FFT_N
    a = np.arange(n_alpha).reshape(-1, 1, 1, 1, 1)
    j = np.arange(N_CHUNK).reshape(1, -1, 1, 1, 1)
    k1 = np.arange(FFT_N1).reshape(1, 1, -1, 1, 1)
    t = np.arange(SUB).reshape(1, 1, 1, 1, -1)
    bf16 = jnp.dtype(BF16)

    def stage_a(n1_count, inverse):
        n1 = np.arange(n1_count).reshape(1, 1, 1, -1, 1)
        n2 = SUB * j + t
        e = (RADIX8 * FFT_N2 * n1 * k1 + RADIX8 * n2 * k1 + a * (FFT_N2 * n1 + n2)) % big
        ang = e * (2.0 * np.pi / big)
        cr = np.cos(ang)
        ci = np.sin(ang) if inverse else -np.sin(ang)
        if inverse:
            cr = cr / FFT_N
            ci = ci / FFT_N
        eye = np.eye(SUB)
        if not inverse:
            blk = lambda m: np.einsum('ajknt,ts->ajksnt', m, eye)
            top = np.concatenate([blk(cr), blk(-ci)], axis=4)
            bot = np.concatenate([blk(ci), blk(cr)], axis=4)
            m = np.stack([top, bot], axis=2)
            return m.reshape(n_alpha, N_CHUNK, 2 * FFT_N1 * SUB, 2 * n1_count * SUB).astype(bf16)
        blk = lambda m: np.einsum('ajknt,ts->ajnskt', m, eye)
        top = np.concatenate([blk(cr), blk(-ci)], axis=4)
        bot = np.concatenate([blk(ci), blk(cr)], axis=4)
        m = np.stack([top, bot], axis=2)
        return m.reshape(n_alpha, N_CHUNK, 2 * n1_count * SUB, 2 * FFT_N1 * SUB).astype(bf16)

    kf = stage_a(n1_in, False)
    ki = stage_a(n1_out, True)
    n2 = np.arange(FFT_N2)
    ang = ((n2[:, None] * n2[None, :]) % FFT_N2) * (2.0 * np.pi / FFT_N2)
    gr, gi = np.cos(ang), -np.sin(ang)
    g = np.block([[gr, -gi], [gi, gr]]).astype(bf16)
    ginv = np.block([[gr, gi], [-gi, gr]]).astype(bf16)
    return kf, g, ginv, ki


def _fft_stage_a(z_ref, kf, s1_ref, n1_in):
    def body(j, carry):
        off = pl.multiple_of(j * SUB, SUB)
        chunks = [z_ref[p, pl.ds(FFT_N2 * n1 + off, SUB), :].astype(F32)
                  for p in range(2) for n1 in range(n1_in)]
        slab = jnp.concatenate(chunks, axis=0).astype(BF16)
        a = jnp.dot(kf[j], slab, preferred_element_type=F32)
        for q in range(2):
            for k1 in range(FFT_N1):
                r = (q * FFT_N1 + k1) * SUB
                s1_ref[pl.ds(k1 * 2 * FFT_N2 + q * FFT_N2 + off, SUB), :] = a[r:r + SUB, :]
        return carry
    lax.fori_loop(0, N_CHUNK, body, 0, unroll=2)


def _fft_core_kernel(z_ref, kf_ref, g_ref, gi_ref, ki_ref, h_ref, o_ref, s1_ref, *, n1_in, n1_out):
    _fft_stage_a(z_ref, kf_ref, s1_ref, n1_in)

    def stage_b(k1, carry):
        r0 = pl.multiple_of(k1 * 2 * FFT_N2, 2 * FFT_N2)
        slab = s1_ref[pl.ds(r0, 2 * FFT_N2), :].astype(BF16)
        x = jnp.dot(g_ref[...], slab, preferred_element_type=F32)
        xr, xi = x[:FFT_N2], x[FFT_N2:]
        hr = h_ref[pl.ds(r0, FFT_N2), :]
        hi = h_ref[pl.ds(r0 + FFT_N2, FFT_N2), :]
        y = jnp.concatenate([xr * hr - xi * hi, xr * hi + xi * hr], axis=0).astype(BF16)
        s1_ref[pl.ds(r0, 2 * FFT_N2), :] = jnp.dot(gi_ref[...], y, preferred_element_type=F32)
        return carry
    lax.fori_loop(0, FFT_N1, stage_b, 0, unroll=4)

    def stage_c(j, carry):
        off = pl.multiple_of(j * SUB, SUB)
        chunks = [s1_ref[pl.ds(k1 * 2 * FFT_N2 + q * FFT_N2 + off, SUB), :]
                  for q in range(2) for k1 in range(FFT_N1)]
        slab = jnp.concatenate(chunks, axis=0).astype(BF16)
        y = jnp.dot(ki_ref[j], slab, preferred_element_type=F32)
        for p in range(2):
            for n1 in range(n1_out):
                r = (p * n1_out + n1) * SUB
                o_ref[p, pl.ds(FFT_N2 * n1 + off, SUB), :] = y[r:r + SUB, :]
        return carry
    lax.fori_loop(0, N_CHUNK, stage_c, 0, unroll=2)


def _fft_forward_kernel(z_ref, kf_ref, g_ref, o_ref, s1_ref, *, n1_in):
    _fft_stage_a(z_ref, kf_ref, s1_ref, n1_in)

    def stage_b(k1, carry):
        r0 = pl.multiple_of(k1 * 2 * FFT_N2, 2 * FFT_N2)
        slab = s1_ref[pl.ds(r0, 2 * FFT_N2), :].astype(BF16)
        o_ref[pl.ds(r0, 2 * FFT_N2), :] = jnp.dot(g_ref[...], slab, preferred_element_type=F32)
        return carry
    lax.fori_loop(0, FFT_N1, stage_b, 0, unroll=4)


def _const_spec(shape, index_map):
    return pl.BlockSpec(shape, index_map, pipeline_mode=pl.Buffered(1))


def _fft_conv_pairs(z3, col0, spec, mats, ct):
    kf, g, ginv, ki = mats
    bsz, seq, _ = z3.shape
    c = spec.shape[1]
    n1 = seq // FFT_N2
    cb0 = col0 // ct
    return pl.pallas_call(
        functools.partial(_fft_core_kernel, n1_in=n1, n1_out=n1),
        grid=(c // ct, bsz // 2),
        in_specs=[
            pl.BlockSpec((2, seq, ct), lambda cc, p: (p, 0, cb0 + cc)),
            _const_spec((None,) + kf.shape[1:], lambda cc, p: (0, 0, 0, 0)),
            _const_spec(g.shape, lambda cc, p: (0, 0)),
            _const_spec(ginv.shape, lambda cc, p: (0, 0)),
            _const_spec((None,) + ki.shape[1:], lambda cc, p: (0, 0, 0, 0)),
            _const_spec((SPEC_ROWS, ct), lambda cc, p: (0, cc)),
        ],
        out_specs=pl.BlockSpec((2, seq, ct), lambda cc, p: (p, 0, cc)),
        out_shape=jax.ShapeDtypeStruct((bsz, seq, c), F32),
        scratch_shapes=[pltpu.VMEM((SPEC_ROWS, ct), F32)],
        compiler_params=_cparams(("arbitrary", "arbitrary")),
        name="fft_conv_pairs",
    )(z3, kf, g, ginv, ki, spec)


def _fft_conv_residues(u, spec, mats, ct):
    kf, g, ginv, ki = mats
    c = u.shape[-1]
    return pl.pallas_call(
        functools.partial(_fft_core_kernel, n1_in=FFT_N1, n1_out=FFT_N1),
        grid=(N_ALPHA, c // ct),
        in_specs=[
            pl.BlockSpec((None, 2, FFT_N, ct), lambda a, cc: (a, 0, 0, cc)),
            _const_spec((None,) + kf.shape[1:], lambda a, cc: (a, 0, 0, 0)),
            _const_spec(g.shape, lambda a, cc: (0, 0)),
            _const_spec(ginv.shape, lambda a, cc: (0, 0)),
            _const_spec((None,) + ki.shape[1:], lambda a, cc: (a, 0, 0, 0)),
            pl.BlockSpec((None, SPEC_ROWS, ct), lambda a, cc: (a, 0, cc)),
        ],
        out_specs=pl.BlockSpec((None, 2, FFT_N, ct), lambda a, cc: (a, 0, 0, cc)),
        out_shape=jax.ShapeDtypeStruct(u.shape, F32),
        scratch_shapes=[pltpu.VMEM((SPEC_ROWS, ct), F32)],
        compiler_params=_cparams(("arbitrary", "arbitrary")),
        name="fft_conv_residues",
    )(u, kf, g, ginv, ki, spec)


def _fft_forward(u, kf, g, ct):
    n_a, _, _, c = u.shape
    return pl.pallas_call(
        functools.partial(_fft_forward_kernel, n1_in=FFT_N1),
        grid=(n_a, c // ct),
        in_specs=[
            pl.BlockSpec((None, 2, FFT_N, ct), lambda a, cc: (a, 0, 0, cc)),
            _const_spec((None,) + kf.shape[1:], lambda a, cc: (a, 0, 0, 0)),
            _const_spec(g.shape, lambda a, cc: (0, 0)),
        ],
        out_specs=pl.BlockSpec((None, SPEC_ROWS, ct), lambda a, cc: (a, 0, cc)),
        out_shape=jax.ShapeDtypeStruct((n_a, SPEC_ROWS, c), F32),
        scratch_shapes=[pltpu.VMEM((SPEC_ROWS, ct), F32)],
        compiler_params=_cparams(("arbitrary", "arbitrary")),
        name="fft_forward",
    )(u, kf, g)


def _radix8_coefs(n_blocks):
    a = np.arange(n_blocks)[None, :]
    al = np.arange(N_ALPHA)[:, None]
    ang = 2.0 * np.pi * ((a * al) % RADIX8) / RADIX8
    return np.round(np.cos(ang), 12), np.round(-np.sin(ang), 12)


def _radix8_split_kernel(x_ref, o_ref, *, n_blocks):
    cr, ci = _radix8_coefs(n_blocks)
    xs = [x_ref[a] for a in range(n_blocks)]
    for al in range(N_ALPHA):
        for part, coef in ((0, cr), (1, ci)):
            acc = None
            for a in range(n_blocks):
                w = float(coef[al, a])
                if w == 0.0:
                    continue
                term = xs[a] if w == 1.0 else (-xs[a] if w == -1.0 else xs[a] * w)
                acc = term if acc is None else acc + term
            o_ref[al, part] = jnp.zeros_like(xs[0]) if acc is None else acc


def _radix8_split(x, col0, width, n_blocks, tr=256, ct=512):
    xv = x.reshape(n_blocks, FFT_N, x.shape[1])
    cb0 = col0 // ct
    return pl.pallas_call(
        functools.partial(_radix8_split_kernel, n_blocks=n_blocks),
        grid=(FFT_N // tr, width // ct),
        in_specs=[pl.BlockSpec((n_blocks, tr, ct), lambda i, c: (0, i, cb0 + c))],
        out_specs=pl.BlockSpec((N_ALPHA, 2, tr, ct), lambda i, c: (0, 0, i, c)),
        out_shape=jax.ShapeDtypeStruct((N_ALPHA, 2, FFT_N, width), F32),
        compiler_params=_cparams(("parallel", "parallel")),
        name="radix8_split",
    )(xv)


def _radix8_merge_gate_kernel(v_ref, z_ref, xg_ref, sk_ref, o_ref, *, n_blocks):
    sk = sk_ref[...]
    for a in range(n_blocks):
        acc = None
        for al in range(N_ALPHA):
            cw = (1.0 if al in (0, RADIX8 // 2) else 2.0) / RADIX8
            ang = 2.0 * np.pi * ((a * al) % RADIX8) / RADIX8
            wr = float(np.round(np.cos(ang), 12)) * cw
            wi = float(np.round(np.sin(ang), 12)) * cw
            for w, part in ((wr, 0), (-wi, 1)):
                if w == 0.0:
                    continue
                term = v_ref[al, part] * w
                acc = term if acc is None else acc + term
        o_ref[a] = xg_ref[a] * (acc + sk * z_ref[a])


def _radix8_merge_gate(v, z, zcol0, xg, gcol0, skip_row, n_blocks, tr=256, ct=512):
    c = v.shape[-1]
    zv = z.reshape(n_blocks, FFT_N, z.shape[1])
    gv = xg.reshape(n_blocks, FFT_N, xg.shape[1])
    zb, gb = zcol0 // ct, gcol0 // ct
    out = pl.pallas_call(
        functools.partial(_radix8_merge_gate_kernel, n_blocks=n_blocks),
        grid=(FFT_N // tr, c // ct),
        in_specs=[
            pl.BlockSpec((N_ALPHA, 2, tr, ct), lambda i, cc: (0, 0, i, cc)),
            pl.BlockSpec((n_blocks, tr, ct), lambda i, cc: (0, i, zb + cc)),
            pl.BlockSpec((n_blocks, tr, ct), lambda i, cc: (0, i, gb + cc)),
            pl.BlockSpec((1, ct), lambda i, cc: (0, cc)),
        ],
        out_specs=pl.BlockSpec((n_blocks, tr, ct), lambda i, cc: (0, i, cc)),
        out_shape=jax.ShapeDtypeStruct((n_blocks, FFT_N, c), F32),
        compiler_params=_cparams(("parallel", "parallel")),
        name="radix8_merge_gate",
    )(v, zv, gv, skip_row)
    return out.reshape(n_blocks * FFT_N, c)


def _gate_kernel(y_ref, z_ref, xg_ref, sk_ref, o_ref):
    o_ref[...] = xg_ref[...] * (y_ref[...] + sk_ref[...] * z_ref[...])


def _gate(y, z, zcol0, xg, gcol0, skip_row, tr=512, ct=512):
    t, c = y.shape
    zb, gb = zcol0 // ct, gcol0 // ct
    return pl.pallas_call(
        _gate_kernel,
        grid=(t // tr, c // ct),
        in_specs=[
            pl.BlockSpec((tr, ct), lambda i, cc: (i, cc)),
            pl.BlockSpec((tr, ct), lambda i, cc: (i, zb + cc)),
            pl.BlockSpec((tr, ct), lambda i, cc: (i, gb + cc)),
            pl.BlockSpec((1, ct), lambda i, cc: (0, cc)),
        ],
        out_specs=pl.BlockSpec((tr, ct), lambda i, cc: (i, cc)),
        out_shape=jax.ShapeDtypeStruct((t, c), F32),
        compiler_params=_cparams(("parallel", "parallel")),
        name="hyena_gate",
    )(y, z, xg, skip_row)


def _hyena(hyg3, w_short, b_short, filt, filt_skip):
    bsz, seq, _ = hyg3.shape
    c = HYENA_WIDTH
    hc = _short_conv(hyg3, w_short, b_short)
    taps = _filter_taps(seq, *filt)
    hc2 = hc.reshape(bsz * seq, 3 * c)
    if 2 * seq == FFT_N:
        mats = _dft_matrices(seq // FFT_N2, seq // FFT_N2, 1)
        mats_full = _dft_matrices(FFT_N1, FFT_N1, 1)
        u = jnp.stack([taps, jnp.zeros_like(taps)], axis=0)[None]
        spec = _fft_forward(u, mats_full[0], mats_full[1], 256)[0]
        z = hc
        zcol = 0
        for n in range(HYENA_ORDER):
            y = _fft_conv_pairs(z, zcol, spec[:, n * c:(n + 1) * c], mats, 256)
            z2 = _gate(y.reshape(bsz * seq, c), z.reshape(bsz * seq, -1), zcol, hc2, (n + 1) * c,
                       filt_skip[n][None, :])
            z = z2.reshape(bsz, seq, c)
            zcol = 0
        return z.reshape(bsz * seq, c)
    assert bsz == 1 and 2 * seq == RADIX8 * FFT_N
    n_blocks = seq // FFT_N
    mats = _dft_matrices(FFT_N1, FFT_N1, N_ALPHA)
    tap_res = _radix8_split(taps, 0, HYENA_ORDER * c, RADIX8)
    spec = _fft_forward(tap_res, mats[0], mats[1], 256)
    z = hc2
    zcol = 0
    for n in range(HYENA_ORDER):
        u = _radix8_split(z, zcol, c, n_blocks)
        v = _fft_conv_residues(u, spec[:, :, n * c:(n + 1) * c], mats, 128)
        z = _radix8_merge_gate(v, z, zcol, hc2, (n + 1) * c, filt_skip[n][None, :], n_blocks)
        zcol = 0
    return z


def _merge_kernel(o0_ref, o1_ref, o2_ref, l0_ref, l1_ref, l2_ref, zh_ref, ga_ref, gh_ref, x_ref,
                  wa_ref, wh_ref, wo_ref, out_ref):
    l0, l1, l2 = l0_ref[...], l1_ref[...], l2_ref[...]
    m = jnp.maximum(jnp.maximum(l0, l1), l2)
    e0, e1, e2 = jnp.exp(l0 - m), jnp.exp(l1 - m), jnp.exp(l2 - m)
    tot = e0 + e1 + e2
    w0, w1, w2 = e0 / tot, e1 / tot, e2 / tot
    heads = []
    for h in range(HEADS_PER_GROUP):
        heads.append(o0_ref[h] * w0[:, h:h + 1] + o1_ref[h] * w1[:, h:h + 1] + o2_ref[h] * w2[:, h:h + 1])
    attn = jnp.concatenate(heads, axis=1).astype(BF16)
    ab = jnp.dot(attn, wa_ref[...], preferred_element_type=F32)
    hb = jnp.dot(zh_ref[...].astype(BF16), wh_ref[...], preferred_element_type=F32)
    merged = jax.nn.sigmoid(ga_ref[...]) * ab + jax.nn.sigmoid(gh_ref[...]) * hb
    out_ref[...] = x_ref[...] + jnp.dot(merged.astype(BF16), wo_ref[...], preferred_element_type=F32)


def _merge(outs, lses, zh, hyg, x2d, wa, wh, wo, tm=512):
    t, d = x2d.shape
    seq = outs[0].shape[2]
    per_seq = seq // tm
    tok = lambda w: pl.BlockSpec((tm, w), lambda i: (i, 0))
    full = lambda a: pl.BlockSpec(a.shape, lambda i: (0, 0))
    heads = pl.BlockSpec((None, HEADS_PER_GROUP, tm, HEAD_DIM),
                         lambda i: (i // per_seq, 0, i % per_seq, 0))
    gcol = 3 * HYENA_WIDTH // d
    return pl.pallas_call(
        _merge_kernel,
        grid=(t // tm,),
        in_specs=[heads] * 3 + [tok(HEAD_DIM)] * 3 + [
            tok(HYENA_WIDTH),
            pl.BlockSpec((tm, d), lambda i: (i, gcol)),
            pl.BlockSpec((tm, d), lambda i: (i, gcol + 1)),
            tok(d), full(wa), full(wh), full(wo)],
        out_specs=tok(d),
        out_shape=jax.ShapeDtypeStruct((t, d), F32),
        compiler_params=_cparams(("parallel",)),
        name="merge_out_proj",
    )(*outs, *lses, zh, hyg, hyg, x2d, wa, wh, wo)


def _mlp_kernel(x_ref, g_ref, w1_ref, w2_ref, o_ref, xn_ref, acc_ref):
    j = pl.program_id(1)

    @pl.when(j == 0)
    def _():
        x = x_ref[...]
        ms = jnp.mean(x * x, axis=-1, keepdims=True)
        xn_ref[...] = (x * lax.rsqrt(ms + NORM_EPS) * g_ref[...]).astype(BF16)
        acc_ref[...] = jnp.zeros_like(acc_ref)

    h = jnp.dot(xn_ref[...], w1_ref[...], preferred_element_type=F32)
    a = jnp.square(jnp.maximum(h, 0.0)).astype(BF16)
    acc_ref[...] += jnp.dot(a, w2_ref[...], preferred_element_type=F32)

    @pl.when(j == pl.num_programs(1) - 1)
    def _():
        o_ref[...] = x_ref[...] + acc_ref[...]


def _mlp(x2d, g, w1, w2, tm=1024, tf=512):
    t, d = x2d.shape
    f = w1.shape[1]
    return pl.pallas_call(
        _mlp_kernel,
        grid=(t // tm, f // tf),
        in_specs=[
            pl.BlockSpec((tm, d), lambda i, j: (i, 0)),
            pl.BlockSpec((1, d), lambda i, j: (0, 0)),
            pl.BlockSpec((d, tf), lambda i, j: (0, j)),
            pl.BlockSpec((tf, d), lambda i, j: (j, 0)),
        ],
        out_specs=pl.BlockSpec((tm, d), lambda i, j: (i, 0)),
        out_shape=jax.ShapeDtypeStruct((t, d), F32),
        scratch_shapes=[pltpu.VMEM((tm, d), BF16), pltpu.VMEM((tm, d), F32)],
        compiler_params=_cparams(("parallel", "arbitrary")),
        name="mlp",
    )(x2d, g, w1, w2)


def _encoder_layer(x, p):
    bsz, seq, d = x.shape
    x2d = x.reshape(bsz * seq, d)
    hyg = _norm_matmul(x2d, p['g_mix'], p['w_hyg'], F32)
    outs, lses = [], []
    for gi, (_, dil) in enumerate(DILATION_GROUPS):
        qkv_g = _qkv_group(x2d, p['g_mix'], p['w_qkv'][gi], p['head_gain'], dil)
        o, s = _attn_group(qkv_g, p['rel_bias'], gi, bsz, seq)
        outs.append(o)
        lses.append(s)
    zh = _hyena(hyg.reshape(bsz, seq, HYG_WIDTH), p['w_short'], p['b_short'], p['filt'], p['filt_skip'])
    x1 = _merge(outs, lses, zh, hyg, x2d, p['wa'], p['wh'], p['wo'])
    y = _mlp(x1, p['g_mlp'], p['w1'], p['w2'])
    return y.reshape(bsz, seq, d)


def kernel(x_prompt, x_sample, rel_bias, g_mix, w_in, g_q, g_k, w_attn_branch, w_short, b_short,
           filt_w1, filt_b1, filt_w2, filt_b2, filt_w3, filt_b3, filt_w4, filt_freq, filt_skip,
           w_hyena_branch, w_out, g_mlp, w_ff1, w_ff2):
    y_prompt, y_sample = x_prompt, x_sample
    for l in range(g_mix.shape[0]):
        w_in_b = w_in[l].astype(BF16)
        w_qkv = [jnp.concatenate([w_in_b[:, s * ATTN_WIDTH + gi * GROUP_WIDTH:s * ATTN_WIDTH + (gi + 1) * GROUP_WIDTH]
                                  for s in range(3)], axis=1) for gi in range(N_GROUPS)]
        head_gain = jnp.concatenate([
            jnp.tile(g_q[l].astype(F32) * (HEAD_DIM ** -0.5), HEADS_PER_GROUP),
            jnp.tile(g_k[l].astype(F32), HEADS_PER_GROUP),
            jnp.ones((GROUP_WIDTH,), F32)])[None, :]
        p = dict(
            g_mix=g_mix[l][None, :].astype(F32),
            w_qkv=w_qkv, w_hyg=w_in_b[:, QKV_WIDTH:],
            head_gain=head_gain,
            rel_bias=rel_bias,
            w_short=w_short[l], b_short=b_short[l][None, :],
            filt=(filt_w1[l], filt_b1[l], filt_w2[l], filt_b2[l], filt_w3[l], filt_b3[l], filt_w4[l],
                  filt_freq[l]),
            filt_skip=filt_skip[l],
            wa=w_attn_branch[l].astype(BF16), wh=w_hyena_branch[l].astype(BF16),
            wo=w_out[l].astype(BF16),
            g_mlp=g_mlp[l][None, :].astype(F32), w1=w_ff1[l].astype(BF16), w2=w_ff2[l].astype(BF16))
        y_prompt = _encoder_layer(y_prompt, p)
        y_sample = _encoder_layer(y_sample, p)
    return (y_prompt, y_sample)
```

```python
import functools
import math

import numpy as np
import jax
import jax.numpy as jnp
from jax import lax
from jax.experimental import pallas as pl
from jax.experimental.pallas import tpu as pltpu

F32 = jnp.float32
BF16 = jnp.bfloat16

D_MODEL = 1024
HEAD_DIM = 128
HEADS_PER_GROUP = 4
DILATION_GROUPS = ((128, 1), (512, 4), (2048, 16))
N_GROUPS = len(DILATION_GROUPS)
ATTN_WIDTH = N_GROUPS * HEADS_PER_GROUP * HEAD_DIM
GROUP_WIDTH = HEADS_PER_GROUP * HEAD_DIM
GROUP_QKV = 3 * GROUP_WIDTH
BAND = 64
N_BUCKETS = 32
MAX_DISTANCE = 1024
HYENA_WIDTH = D_MODEL
HYENA_ORDER = 2
FILTER_BANDS = 16
FILTER_HIDDEN = 64
FILTER_OUT_SCALE = 0.02
DECAY_TARGET = 1e-2
FAST_DECAY_PCT = 0.3
SLOW_DECAY_PCT = 1.5
D_FF = 4 * D_MODEL
QKV_WIDTH = 3 * ATTN_WIDTH
HYG_WIDTH = 3 * HYENA_WIDTH + 2 * D_MODEL
NORM_EPS = 1e-6
MASK_VALUE = -1e30

FFT_N = 4096
FFT_N1 = 32
FFT_N2 = 128
SUB = 8
LANES = 128
HALO = 16
N_CHUNK = FFT_N2 // SUB
SPEC_ROWS = 2 * FFT_N
RADIX8 = 8
N_ALPHA = 5

VMEM_LIMIT = 56 * 1024 * 1024


def _cparams(sem):
    return pltpu.CompilerParams(dimension_semantics=sem, vmem_limit_bytes=VMEM_LIMIT)


def _norm_matmul_kernel(x_ref, g_ref, w_ref, o_ref, xn_ref):
    @pl.when(pl.program_id(1) == 0)
    def _():
        x = x_ref[...]
        ms = jnp.mean(x * x, axis=-1, keepdims=True)
        xn_ref[...] = (x * lax.rsqrt(ms + NORM_EPS) * g_ref[...]).astype(BF16)

    o_ref[...] = jnp.dot(xn_ref[...], w_ref[...], preferred_element_type=F32).astype(o_ref.dtype)


def _norm_matmul(x2d, g, w_bf16, out_dtype, tm=1024, tn=512):
    t, d = x2d.shape
    n = w_bf16.shape[1]
    return pl.pallas_call(
        _norm_matmul_kernel,
        grid=(t // tm, n // tn),
        in_specs=[
            pl.BlockSpec((tm, d), lambda i, j: (i, 0)),
            pl.BlockSpec((1, d), lambda i, j: (0, 0)),
            pl.BlockSpec((d, tn), lambda i, j: (0, j)),
        ],
        out_specs=pl.BlockSpec((tm, tn), lambda i, j: (i, j)),
        out_shape=jax.ShapeDtypeStruct((t, n), out_dtype),
        scratch_shapes=[pltpu.VMEM((tm, d), BF16)],
        compiler_params=_cparams(("parallel", "arbitrary")),
        name="norm_matmul",
    )(x2d, g, w_bf16)


def _qkv_group_kernel(x_ref, g_ref, w_ref, hg_ref, o_ref, *scratch, tm, dil):
    rows = tm // dil
    x = x_ref[...]
    ms = jnp.mean(x * x, axis=-1, keepdims=True)
    xn = x * lax.rsqrt(ms + NORM_EPS) * g_ref[...]
    if dil == 1:
        xn = xn.astype(BF16)
    else:
        s_ref, = scratch
        n_slab = xn.shape[1] // LANES
        for c in range(n_slab):
            s_ref[c * tm:(c + 1) * tm, :] = xn[:, c * LANES:(c + 1) * LANES]
        parts = []
        for r in range(dil):
            slabs = [s_ref[pl.ds(c * tm + r, rows, stride=dil), :] for c in range(n_slab)]
            parts.append(jnp.concatenate(slabs, axis=1).astype(BF16))
        xn = jnp.concatenate(parts, axis=0)
    hg = hg_ref[...]
    for jt in range(3):
        acc = jnp.dot(xn, w_ref[:, jt * GROUP_WIDTH:(jt + 1) * GROUP_WIDTH], preferred_element_type=F32)
        if jt < 2:
            heads = []
            for h in range(HEADS_PER_GROUP):
                a = acc[:, h * HEAD_DIM:(h + 1) * HEAD_DIM]
                ms = jnp.mean(a * a, axis=-1, keepdims=True)
                c0 = jt * GROUP_WIDTH + h * HEAD_DIM
                heads.append(a * lax.rsqrt(ms + NORM_EPS) * hg[:, c0:c0 + HEAD_DIM])
            acc = jnp.concatenate(heads, axis=1)
        acc = acc.astype(BF16)
        for r in range(dil):
            c0 = r * GROUP_QKV + jt * GROUP_WIDTH
            o_ref[:, c0:c0 + GROUP_WIDTH] = acc[r * rows:(r + 1) * rows, :]


def _qkv_group(x2d, g, w_g, head_gain_g, dil, tm=1024):
    t, d = x2d.shape
    return pl.pallas_call(
        functools.partial(_qkv_group_kernel, tm=tm, dil=dil),
        grid=(t // tm,),
        in_specs=[
            pl.BlockSpec((tm, d), lambda i: (i, 0)),
            pl.BlockSpec((1, d), lambda i: (0, 0)),
            pl.BlockSpec((d, GROUP_QKV), lambda i: (0, 0)),
            pl.BlockSpec((1, GROUP_QKV), lambda i: (0, 0)),
        ],
        out_specs=pl.BlockSpec((tm // dil, dil * GROUP_QKV), lambda i: (i, 0)),
        out_shape=jax.ShapeDtypeStruct((t // dil, dil * GROUP_QKV), BF16),
        scratch_shapes=[] if dil == 1 else [pltpu.VMEM((tm * d // LANES, LANES), F32)],
        compiler_params=_cparams(("parallel",)),
        name=f"qkv_group_d{dil}",
    )(x2d, g, w_g, head_gain_g)


def _t5_bucket_np(rel):
    nb = N_BUCKETS // 2
    max_exact = nb // 2
    side = np.where(rel > 0, nb, 0)
    n = np.abs(rel)
    nf = np.maximum(n, 1).astype(np.float32)
    large = max_exact + (np.log(nf / np.float32(max_exact)) / np.float32(math.log(MAX_DISTANCE / max_exact))
                         * np.float32(nb - max_exact)).astype(np.int32)
    large = np.minimum(large, nb - 1)
    return side + np.where(n < max_exact, n, large)


def _band_bucket_index(dil):
    qi = np.arange(2 * BAND)[:, None]
    kj = np.arange(4 * BAND)[None, :]
    delta = kj - BAND - qi
    idx = _t5_bucket_np(delta * dil).astype(np.int32)
    return np.where(np.abs(delta) <= BAND, idx, -1).astype(np.int32)


def _attn_kernel(idx_ref, tbl_ref, c_ref, p_ref, n_ref, o_ref, lse_ref, bias_ref, *, tq, m_len, dil):
    qb = 2 * BAND
    wb = 4 * BAND
    nsub = tq // qb
    first = (pl.program_id(0) == 0) & (pl.program_id(1) == 0)

    @pl.when(first)
    def _():
        idx = idx_ref[...]
        for h in range(HEADS_PER_GROUP):
            acc = jnp.full((qb, wb), MASK_VALUE, F32)
            for b in range(N_BUCKETS):
                acc = jnp.where(idx == b, tbl_ref[h, b], acc)
            bias_ref[h] = acc

    i = pl.program_id(1)
    col = lax.broadcasted_iota(jnp.int32, (qb, wb), 1)
    lane = lax.broadcasted_iota(jnp.int32, (qb, HEAD_DIM), 1)

    def window(s, cols):
        lo, hi = s * qb - BAND, s * qb + qb + BAND
        parts = []
        if lo < 0:
            parts.append(p_ref[0, :, cols])
            lo = 0
        parts.append(c_ref[0, lo:min(hi, tq), cols])
        if hi > tq:
            parts.append(n_ref[0, :, cols])
        return parts[0] if len(parts) == 1 else jnp.concatenate(parts, axis=0)

    for s in range(nsub):
        lo = BAND - i * tq - qb * s
        valid = (col >= lo) & (col < m_len + lo)
        for r in range(dil):
            base = r * GROUP_QKV
            lse_tile = jnp.zeros((qb, HEAD_DIM), F32)
            rows = pl.ds(s * qb * dil + r, qb, stride=dil) if dil > 1 else pl.ds(s * qb, qb)
            for h in range(HEADS_PER_GROUP):
                qs = c_ref[0, s * qb:(s + 1) * qb, base + h * HEAD_DIM:base + (h + 1) * HEAD_DIM]
                kk = window(s, slice(base + GROUP_WIDTH + h * HEAD_DIM, base + GROUP_WIDTH + (h + 1) * HEAD_DIM))
                sc = lax.dot_general(qs, kk, (((1,), (1,)), ((), ())), preferred_element_type=F32)
                sc = jnp.where(valid, sc + bias_ref[h], MASK_VALUE)
                mx = jnp.max(sc, axis=-1, keepdims=True)
                p = jnp.exp(sc - mx)
                den = jnp.sum(p, axis=-1, keepdims=True)
                vv = window(s, slice(base + 2 * GROUP_WIDTH + h * HEAD_DIM,
                                     base + 2 * GROUP_WIDTH + (h + 1) * HEAD_DIM))
                o = jnp.dot(p.astype(BF16), vv, preferred_element_type=F32) / den
                o_ref[0, h, rows, :] = o
                lse_tile = jnp.where(lane == h, mx + jnp.log(den), lse_tile)
            lse_ref[0, rows, :] = lse_tile


def _attn_group(qkv_g, rel_bias, gi, bsz, seq):
    _, dil = DILATION_GROUPS[gi]
    m_len = seq // dil
    tq = min(m_len, 512 if dil == 1 else 128)
    nblk = tq // BAND
    n_halo = m_len // BAND
    width = dil * GROUP_QKV
    view = qkv_g.reshape(bsz, m_len, width)
    idx = jnp.asarray(_band_bucket_index(dil))
    tbl = rel_bias[:, gi * HEADS_PER_GROUP:(gi + 1) * HEADS_PER_GROUP].T.astype(F32)
    o, lse = pl.pallas_call(
        functools.partial(_attn_kernel, tq=tq, m_len=m_len, dil=dil),
        grid=(bsz, m_len // tq),
        in_specs=[
            pl.BlockSpec((2 * BAND, 4 * BAND), lambda b, i: (0, 0)),
            pl.BlockSpec(memory_space=pltpu.SMEM),
            pl.BlockSpec((1, tq, width), lambda b, i: (b, i, 0)),
            pl.BlockSpec((1, BAND, width), lambda b, i: (b, jnp.maximum(i * nblk - 1, 0), 0)),
            pl.BlockSpec((1, BAND, width), lambda b, i: (b, jnp.minimum((i + 1) * nblk, n_halo - 1), 0)),
        ],
        out_specs=[
            pl.BlockSpec((1, HEADS_PER_GROUP, tq * dil, HEAD_DIM), lambda b, i: (b, 0, i, 0)),
            pl.BlockSpec((1, tq * dil, HEAD_DIM), lambda b, i: (b, i, 0)),
        ],
        out_shape=[
            jax.ShapeDtypeStruct((bsz, HEADS_PER_GROUP, seq, HEAD_DIM), F32),
            jax.ShapeDtypeStruct((bsz, seq, HEAD_DIM), F32),
        ],
        scratch_shapes=[pltpu.VMEM((HEADS_PER_GROUP, 2 * BAND, 4 * BAND), F32)],
        compiler_params=_cparams(("arbitrary", "arbitrary")),
        name=f"band_attn_g{gi}",
    )(idx, tbl, view, view, view)
    return o, lse.reshape(bsz * seq, HEAD_DIM)


def _short_conv_kernel(c_ref, p_ref, n_ref, w_ref, b_ref, o_ref, *, tl):
    i = pl.program_id(1)
    last = pl.num_programs(1) - 1
    cur = c_ref[0].astype(F32)
    prev_row = jnp.where(i == 0, 0.0, p_ref[0, HALO - 1:HALO, :].astype(F32))
    next_row = jnp.where(i == last, 0.0, n_ref[0, 0:1, :].astype(F32))
    row = lax.broadcasted_iota(jnp.int32, cur.shape, 0)
    up = jnp.where(row == 0, prev_row, pltpu.roll(cur, 1, 0))
    dn = jnp.where(row == tl - 1, next_row, pltpu.roll(cur, tl - 1, 0))
    w = w_ref[...]
    o_ref[0] = up * w[0:1, :] + cur * w[1:2, :] + dn * w[2:3, :] + b_ref[...]


def _short_conv(hyg3, w_short, b_short, tl=512, ct=512):
    bsz, seq, _ = hyg3.shape
    width = w_short.shape[1]
    nsub = tl // HALO
    return pl.pallas_call(
        functools.partial(_short_conv_kernel, tl=tl),
        grid=(bsz, seq // tl, width // ct),
        in_specs=[
            pl.BlockSpec((1, tl, ct), lambda b, i, c: (b, i, c)),
            pl.BlockSpec((1, HALO, ct), lambda b, i, c: (b, jnp.maximum(i * nsub - 1, 0), c)),
            pl.BlockSpec((1, HALO, ct), lambda b, i, c: (b, jnp.minimum((i + 1) * nsub, seq // HALO - 1), c)),
            pl.BlockSpec((3, ct), lambda b, i, c: (0, c)),
            pl.BlockSpec((1, ct), lambda b, i, c: (0, c)),
        ],
        out_specs=pl.BlockSpec((1, tl, ct), lambda b, i, c: (b, i, c)),
        out_shape=jax.ShapeDtypeStruct((bsz, seq, width), F32),
        compiler_params=_cparams(("parallel", "parallel", "parallel")),
        name="short_conv",
    )(hyg3, hyg3, hyg3, w_short, b_short)


def _filter_kernel(w1_ref, b1_ref, w2_ref, b2_ref, w3_ref, b3_ref, fr_ref, w4_ref,
                   ad_ref, o_ref, *, seq, tr):
    hp = lax.Precision.HIGHEST
    n = pl.program_id(0) * tr + lax.broadcasted_iota(jnp.int32, (tr, 128), 0)
    lag = jnp.where(n < seq, n, 2 * seq - n).astype(F32)
    t = lag / (seq - 1)
    ang = (2.0 * math.pi) * lag / seq
    f = lax.broadcasted_iota(jnp.int32, (tr, 128), 1)
    band_idx = jnp.where(f <= FILTER_BANDS, f - 1, f - 1 - FILTER_BANDS).astype(F32)
    bands = 1e-4 + band_idx * ((FILTER_BANDS - 1 - 1e-4) / (FILTER_BANDS - 1))
    arg = bands * ang
    z = jnp.where(f == 0, t,
                  jnp.where(f <= FILTER_BANDS, jnp.cos(arg),
                            jnp.where(f <= 2 * FILTER_BANDS, -jnp.sin(arg), 0.0)))
    fr = fr_ref[...]
    h = jnp.sin(fr * (jnp.dot(z, w1_ref[...], precision=hp, preferred_element_type=F32) + b1_ref[...]))
    h = jnp.sin(fr * (jnp.dot(h, w2_ref[...], precision=hp, preferred_element_type=F32) + b2_ref[...]))
    h = jnp.sin(fr * (jnp.dot(h, w3_ref[...], precision=hp, preferred_element_type=F32) + b3_ref[...]))
    h_hi = h.astype(BF16)
    h_lo = (h - h_hi.astype(F32)).astype(BF16)
    w_hi, w_lo = w4_ref[0], w4_ref[1]
    taps = (jnp.dot(h_hi, w_hi, preferred_element_type=F32) + jnp.dot(h_hi, w_lo, preferred_element_type=F32)
            + jnp.dot(h_lo, w_hi, preferred_element_type=F32))
    decay = jnp.exp(-t[:, 0:1] * ad_ref[...])
    o_ref[...] = jnp.where(n[:, 0:1] == seq, 0.0, taps * decay)


def _filter_taps(seq, w1, b1, w2, b2, w3, b3, w4, freq, tr=256):
    c = HYENA_WIDTH
    hid = FILTER_HIDDEN
    w1p = jnp.zeros((128, hid), F32).at[:w1.shape[0]].set(w1)
    w4r = w4.reshape(hid, HYENA_ORDER, 2, c)
    w4d = jnp.transpose(w4r, (2, 0, 1, 3)).reshape(2, hid, HYENA_ORDER * c)
    w4_hi = w4d.astype(BF16)
    w4_lo = (w4d - w4_hi.astype(F32)).astype(BF16)
    w4d = jnp.stack([w4_hi, w4_lo], axis=1)
    nblk = 2 * seq // tr
    min_decay = math.log(DECAY_TARGET) / SLOW_DECAY_PCT
    max_decay = math.log(DECAY_TARGET) / FAST_DECAY_PCT
    ad = jnp.abs(jnp.linspace(min_decay, max_decay, c, dtype=F32))
    ad = jnp.tile(ad, HYENA_ORDER)[None, :]
    row = lambda v: v.reshape(1, -1)
    full = lambda shape: pl.BlockSpec(shape, lambda i: (0,) * len(shape))
    return pl.pallas_call(
        functools.partial(_filter_kernel, seq=seq, tr=tr),
        grid=(nblk,),
        in_specs=[full((128, hid)), full((1, hid)), full((hid, hid)), full((1, hid)),
                  full((hid, hid)), full((1, hid)), full((1, hid)),
                  pl.BlockSpec((None, 2, hid, HYENA_ORDER * c), lambda i: (i // (nblk // 2), 0, 0, 0)),
                  full((1, HYENA_ORDER * c))],
        out_specs=pl.BlockSpec((tr, HYENA_ORDER * c), lambda i: (i, 0)),
        out_shape=jax.ShapeDtypeStruct((2 * seq, HYENA_ORDER * c), F32),
        compiler_params=_cparams(("parallel",)),
        name="hyena_filter_taps",
    )(w1p, row(b1), w2, row(b2), w3, row(b3), row(freq), w4d, ad)


@functools.lru_cache(maxsize=None)
def _dft_matrices(n1_in, n1_out, n_alpha):
    big = RADIX8 * FFT_N
    a = np.arange(n_alpha).reshape(-1, 1, 1, 1, 1)
    j = np.arange(N_CHUNK).reshape(1, -1, 1, 1, 1)
    k1 = np.arange(FFT_N1).reshape(1, 1, -1, 1, 1)
    t = np.arange(SUB).reshape(1, 1, 1, 1, -1)
    bf16 = jnp.dtype(BF16)

    def stage_a(n1_count, inverse):
        n1 = np.arange(n1_count).reshape(1, 1, 1, -1, 1)
        n2 = SUB * j + t
        e = (RADIX8 * FFT_N2 * n1 * k1 + RADIX8 * n2 * k1 + a * (FFT_N2 * n1 + n2)) % big
        ang = e * (2.0 * np.pi / big)
        cr = np.cos(ang)
        ci = np.sin(ang) if inverse else -np.sin(ang)
        if inverse:
            cr = cr / FFT_N
            ci = ci / FFT_N
        eye = np.eye(SUB)
        if not inverse:
            blk = lambda m: np.einsum('ajknt,ts->ajksnt', m, eye)
            top = np.concatenate([blk(cr), blk(-ci)], axis=4)
            bot = np.concatenate([blk(ci), blk(cr)], axis=4)
            m = np.stack([top, bot], axis=2)
            return m.reshape(n_alpha, N_CHUNK, 2 * FFT_N1 * SUB, 2 * n1_count * SUB).astype(bf16)
        blk = lambda m: np.einsum('ajknt,ts->ajnskt', m, eye)
        top = np.concatenate([blk(cr), blk(-ci)], axis=4)
        bot = np.concatenate([blk(ci), blk(cr)], axis=4)
        m = np.stack([top, bot], axis=2)
        return m.reshape(n_alpha, N_CHUNK, 2 * n1_count * SUB, 2 * FFT_N1 * SUB).astype(bf16)

    kf = stage_a(n1_in, False)
    ki = stage_a(n1_out, True)
    n2 = np.arange(FFT_N2)
    ang = ((n2[:, None] * n2[None, :]) % FFT_N2) * (2.0 * np.pi / FFT_N2)
    gr, gi = np.cos(ang), -np.sin(ang)
    g = np.block([[gr, -gi], [gi, gr]]).astype(bf16)
    ginv = np.block([[gr, gi], [-gi, gr]]).astype(bf16)
    return kf, g, ginv, ki


def _fft_stage_a(z_ref, kf, s1_ref, n1_in):
    def body(j, carry):
        off = pl.multiple_of(j * SUB, SUB)
        chunks = [z_ref[p, pl.ds(FFT_N2 * n1 + off, SUB), :].astype(F32)
                  for p in range(2) for n1 in range(n1_in)]
        slab = jnp.concatenate(chunks, axis=0).astype(BF16)
        a = jnp.dot(kf[j], slab, preferred_element_type=F32)
        for q in range(2):
            for k1 in range(FFT_N1):
                r = (q * FFT_N1 + k1) * SUB
                s1_ref[pl.ds(k1 * 2 * FFT_N2 + q * FFT_N2 + off, SUB), :] = a[r:r + SUB, :]
        return carry
    lax.fori_loop(0, N_CHUNK, body, 0, unroll=2)


def _fft_core_kernel(z_ref, kf_ref, g_ref, gi_ref, ki_ref, h_ref, o_ref, s1_ref, s2_ref, *, n1_in, n1_out):
    _fft_stage_a(z_ref, kf_ref, s1_ref, n1_in)

    def stage_b(k1, carry):
        r0 = pl.multiple_of(k1 * 2 * FFT_N2, 2 * FFT_N2)
        slab = s1_ref[pl.ds(r0, 2 * FFT_N2), :].astype(BF16)
        x = jnp.dot(g_ref[...], slab, preferred_element_type=F32)
        xr, xi = x[:FFT_N2], x[FFT_N2:]
        hr = h_ref[pl.ds(r0, FFT_N2), :]
        hi = h_ref[pl.ds(r0 + FFT_N2, FFT_N2), :]
        y = jnp.concatenate([xr * hr - xi * hi, xr * hi + xi * hr], axis=0).astype(BF16)
        s2_ref[pl.ds(r0, 2 * FFT_N2), :] = jnp.dot(gi_ref[...], y, preferred_element_type=F32)
        return carry
    lax.fori_loop(0, FFT_N1, stage_b, 0, unroll=4)

    def stage_c(j, carry):
        off = pl.multiple_of(j * SUB, SUB)
        chunks = [s2_ref[pl.ds(k1 * 2 * FFT_N2 + q * FFT_N2 + off, SUB), :]
                  for q in range(2) for k1 in range(FFT_N1)]
        slab = jnp.concatenate(chunks, axis=0).astype(BF16)
        y = jnp.dot(ki_ref[j], slab, preferred_element_type=F32)
        for p in range(2):
            for n1 in range(n1_out):
                r = (p * n1_out + n1) * SUB
                o_ref[p, pl.ds(FFT_N2 * n1 + off, SUB), :] = y[r:r + SUB, :]
        return carry
    lax.fori_loop(0, N_CHUNK, stage_c, 0, unroll=2)


def _fft_forward_kernel(z_ref, kf_ref, g_ref, o_ref, s1_ref, *, n1_in):
    _fft_stage_a(z_ref, kf_ref, s1_ref, n1_in)

    def stage_b(k1, carry):
        r0 = pl.multiple_of(k1 * 2 * FFT_N2, 2 * FFT_N2)
        slab = s1_ref[pl.ds(r0, 2 * FFT_N2), :].astype(BF16)
        o_ref[pl.ds(r0, 2 * FFT_N2), :] = jnp.dot(g_ref[...], slab, preferred_element_type=F32)
        return carry
    lax.fori_loop(0, FFT_N1, stage_b, 0, unroll=4)


def _const_spec(shape, index_map):
    return pl.BlockSpec(shape, index_map, pipeline_mode=pl.Buffered(1))


def _fft_conv_pairs(z3, col0, spec, mats, ct):
    kf, g, ginv, ki = mats
    bsz, seq, _ = z3.shape
    c = spec.shape[1]
    n1 = seq // FFT_N2
    cb0 = col0 // ct
    return pl.pallas_call(
        functools.partial(_fft_core_kernel, n1_in=n1, n1_out=n1),
        grid=(c // ct, bsz // 2),
        in_specs=[
            pl.BlockSpec((2, seq, ct), lambda cc, p: (p, 0, cb0 + cc)),
            _const_spec((None,) + kf.shape[1:], lambda cc, p: (0, 0, 0, 0)),
            _const_spec(g.shape, lambda cc, p: (0, 0)),
            _const_spec(ginv.shape, lambda cc, p: (0, 0)),
            _const_spec((None,) + ki.shape[1:], lambda cc, p: (0, 0, 0, 0)),
            _const_spec((SPEC_ROWS, ct), lambda cc, p: (0, cc)),
        ],
        out_specs=pl.BlockSpec((2, seq, ct), lambda cc, p: (p, 0, cc)),
        out_shape=jax.ShapeDtypeStruct((bsz, seq, c), F32),
        scratch_shapes=[pltpu.VMEM((SPEC_ROWS, ct), F32), pltpu.VMEM((SPEC_ROWS, ct), F32)],
        compiler_params=_cparams(("arbitrary", "arbitrary")),
        name="fft_conv_pairs",
    )(z3, kf, g, ginv, ki, spec)


def _fft_conv_residues(u, spec, mats, ct):
    kf, g, ginv, ki = mats
    c = u.shape[-1]
    return pl.pallas_call(
        functools.partial(_fft_core_kernel, n1_in=FFT_N1, n1_out=FFT_N1),
        grid=(N_ALPHA, c // ct),
        in_specs=[
            pl.BlockSpec((None, 2, FFT_N, ct), lambda a, cc: (a, 0, 0, cc)),
            _const_spec((None,) + kf.shape[1:], lambda a, cc: (a, 0, 0, 0)),
            _const_spec(g.shape, lambda a, cc: (0, 0)),
            _const_spec(ginv.shape, lambda a, cc: (0, 0)),
            _const_spec((None,) + ki.shape[1:], lambda a, cc: (a, 0, 0, 0)),
            pl.BlockSpec((None, SPEC_ROWS, ct), lambda a, cc: (a, 0, cc)),
        ],
        out_specs=pl.BlockSpec((None, 2, FFT_N, ct), lambda a, cc: (a, 0, 0, cc)),
        out_shape=jax.ShapeDtypeStruct(u.shape, F32),
        scratch_shapes=[pltpu.VMEM((SPEC_ROWS, ct), F32), pltpu.VMEM((SPEC_ROWS, ct), F32)],
        compiler_params=_cparams(("arbitrary", "arbitrary")),
        name="fft_conv_residues",
    )(u, kf, g, ginv, ki, spec)


def _fft_forward(u, kf, g, ct):
    n_a, _, _, c = u.shape
    return pl.pallas_call(
        functools.partial(_fft_forward_kernel, n1_in=FFT_N1),
        grid=(n_a, c // ct),
        in_specs=[
            pl.BlockSpec((None, 2, FFT_N, ct), lambda a, cc: (a, 0, 0, cc)),
            _const_spec((None,) + kf.shape[1:], lambda a, cc: (a, 0, 0, 0)),
            _const_spec(g.shape, lambda a, cc: (0, 0)),
        ],
        out_specs=pl.BlockSpec((None, SPEC_ROWS, ct), lambda a, cc: (a, 0, cc)),
        out_shape=jax.ShapeDtypeStruct((n_a, SPEC_ROWS, c), F32),
        scratch_shapes=[pltpu.VMEM((SPEC_ROWS, ct), F32)],
        compiler_params=_cparams(("arbitrary", "arbitrary")),
        name="fft_forward",
    )(u, kf, g)


def _radix8_coefs(n_blocks):
    a = np.arange(n_blocks)[None, :]
    al = np.arange(N_ALPHA)[:, None]
    ang = 2.0 * np.pi * ((a * al) % RADIX8) / RADIX8
    return np.round(np.cos(ang), 12), np.round(-np.sin(ang), 12)


def _radix8_split_kernel(x_ref, o_ref, *, n_blocks):
    cr, ci = _radix8_coefs(n_blocks)
    xs = [x_ref[a] for a in range(n_blocks)]
    for al in range(N_ALPHA):
        for part, coef in ((0, cr), (1, ci)):
            acc = None
            for a in range(n_blocks):
                w = float(coef[al, a])
                if w == 0.0:
                    continue
                term = xs[a] if w == 1.0 else (-xs[a] if w == -1.0 else xs[a] * w)
                acc = term if acc is None else acc + term
            o_ref[al, part] = jnp.zeros_like(xs[0]) if acc is None else acc


def _radix8_split(x, col0, width, n_blocks, tr=256, ct=512):
    xv = x.reshape(n_blocks, FFT_N, x.shape[1])
    cb0 = col0 // ct
    return pl.pallas_call(
        functools.partial(_radix8_split_kernel, n_blocks=n_blocks),
        grid=(FFT_N // tr, width // ct),
        in_specs=[pl.BlockSpec((n_blocks, tr, ct), lambda i, c: (0, i, cb0 + c))],
        out_specs=pl.BlockSpec((N_ALPHA, 2, tr, ct), lambda i, c: (0, 0, i, c)),
        out_shape=jax.ShapeDtypeStruct((N_ALPHA, 2, FFT_N, width), F32),
        compiler_params=_cparams(("parallel", "parallel")),
        name="radix8_split",
    )(xv)


def _radix8_merge_gate_kernel(v_ref, z_ref, xg_ref, sk_ref, o_ref, *, n_blocks):
    sk = sk_ref[...]
    for a in range(n_blocks):
        acc = None
        for al in range(N_ALPHA):
            cw = (1.0 if al in (0, RADIX8 // 2) else 2.0) / RADIX8
            ang = 2.0 * np.pi * ((a * al) % RADIX8) / RADIX8
            wr = float(np.round(np.cos(ang), 12)) * cw
            wi = float(np.round(np.sin(ang), 12)) * cw
            for w, part in ((wr, 0), (-wi, 1)):
                if w == 0.0:
                    continue
                term = v_ref[al, part] * w
                acc = term if acc is None else acc + term
        o_ref[a] = xg_ref[a] * (acc + sk * z_ref[a])


def _radix8_merge_gate(v, z, zcol0, xg, gcol0, skip_row, n_blocks, tr=256, ct=512):
    c = v.shape[-1]
    zv = z.reshape(n_blocks, FFT_N, z.shape[1])
    gv = xg.reshape(n_blocks, FFT_N, xg.shape[1])
    zb, gb = zcol0 // ct, gcol0 // ct
    out = pl.pallas_call(
        functools.partial(_radix8_merge_gate_kernel, n_blocks=n_blocks),
        grid=(FFT_N // tr, c // ct),
        in_specs=[
            pl.BlockSpec((N_ALPHA, 2, tr, ct), lambda i, cc: (0, 0, i, cc)),
            pl.BlockSpec((n_blocks, tr, ct), lambda i, cc: (0, i, zb + cc)),
            pl.BlockSpec((n_blocks, tr, ct), lambda i, cc: (0, i, gb + cc)),
            pl.BlockSpec((1, ct), lambda i, cc: (0, cc)),
        ],
        out_specs=pl.BlockSpec((n_blocks, tr, ct), lambda i, cc: (0, i, cc)),
        out_shape=jax.ShapeDtypeStruct((n_blocks, FFT_N, c), F32),
        compiler_params=_cparams(("parallel", "parallel")),
        name="radix8_merge_gate",
    )(v, zv, gv, skip_row)
    return out.reshape(n_blocks * FFT_N, c)


def _gate_kernel(y_ref, z_ref, xg_ref, sk_ref, o_ref):
    o_ref[...] = xg_ref[...] * (y_ref[...] + sk_ref[...] * z_ref[...])


def _gate(y, z, zcol0, xg, gcol0, skip_row, tr=512, ct=512):
    t, c = y.shape
    zb, gb = zcol0 // ct, gcol0 // ct
    return pl.pallas_call(
        _gate_kernel,
        grid=(t // tr, c // ct),
        in_specs=[
            pl.BlockSpec((tr, ct), lambda i, cc: (i, cc)),
            pl.BlockSpec((tr, ct), lambda i, cc: (i, zb + cc)),
            pl.BlockSpec((tr, ct), lambda i, cc: (i, gb + cc)),
            pl.BlockSpec((1, ct), lambda i, cc: (0, cc)),
        ],
        out_specs=pl.BlockSpec((tr, ct), lambda i, cc: (i, cc)),
        out_shape=jax.ShapeDtypeStruct((t, c), F32),
        compiler_params=_cparams(("parallel", "parallel")),
        name="hyena_gate",
    )(y, z, xg, skip_row)


def _hyena(hyg3, w_short, b_short, filt, filt_skip):
    bsz, seq, _ = hyg3.shape
    c = HYENA_WIDTH
    hc = _short_conv(hyg3, w_short, b_short)
    taps = _filter_taps(seq, *filt)
    hc2 = hc.reshape(bsz * seq, 3 * c)
    if 2 * seq == FFT_N:
        mats = _dft_matrices(seq // FFT_N2, seq // FFT_N2, 1)
        mats_full = _dft_matrices(FFT_N1, FFT_N1, 1)
        u = jnp.stack([taps, jnp.zeros_like(taps)], axis=0)[None]
        spec = _fft_forward(u, mats_full[0], mats_full[1], 256)[0]
        z = hc
        zcol = 0
        for n in range(HYENA_ORDER):
            y = _fft_conv_pairs(z, zcol, spec[:, n * c:(n + 1) * c], mats, 256)
            z2 = _gate(y.reshape(bsz * seq, c), z.reshape(bsz * seq, -1), zcol, hc2, (n + 1) * c,
                       filt_skip[n][None, :])
            z = z2.reshape(bsz, seq, c)
            zcol = 0
        return z.reshape(bsz * seq, c)
    assert bsz == 1 and 2 * seq == RADIX8 * FFT_N
    n_blocks = seq // FFT_N
    mats = _dft_matrices(FFT_N1, FFT_N1, N_ALPHA)
    tap_res = _radix8_split(taps, 0, HYENA_ORDER * c, RADIX8)
    spec = _fft_forward(tap_res, mats[0], mats[1], 256)
    z = hc2
    zcol = 0
    for n in range(HYENA_ORDER):
        u = _radix8_split(z, zcol, c, n_blocks)
        v = _fft_conv_residues(u, spec[:, :, n * c:(n + 1) * c], mats, 128)
        z = _radix8_merge_gate(v, z, zcol, hc2, (n + 1) * c, filt_skip[n][None, :], n_blocks)
        zcol = 0
    return z


def _merge_kernel(o0_ref, o1_ref, o2_ref, l0_ref, l1_ref, l2_ref, zh_ref, ga_ref, gh_ref, x_ref,
                  wa_ref, wh_ref, wo_ref, out_ref):
    l0, l1, l2 = l0_ref[...], l1_ref[...], l2_ref[...]
    m = jnp.maximum(jnp.maximum(l0, l1), l2)
    e0, e1, e2 = jnp.exp(l0 - m), jnp.exp(l1 - m), jnp.exp(l2 - m)
    tot = e0 + e1 + e2
    w0, w1, w2 = e0 / tot, e1 / tot, e2 / tot
    heads = []
    for h in range(HEADS_PER_GROUP):
        heads.append(o0_ref[h] * w0[:, h:h + 1] + o1_ref[h] * w1[:, h:h + 1] + o2_ref[h] * w2[:, h:h + 1])
    attn = jnp.concatenate(heads, axis=1).astype(BF16)
    ab = jnp.dot(attn, wa_ref[...], preferred_element_type=F32)
    hb = jnp.dot(zh_ref[...].astype(BF16), wh_ref[...], preferred_element_type=F32)
    merged = (jax.nn.sigmoid(ga_ref[...].astype(F32)) * ab
              + jax.nn.sigmoid(gh_ref[...].astype(F32)) * hb)
    out_ref[...] = x_ref[...] + jnp.dot(merged.astype(BF16), wo_ref[...], preferred_element_type=F32)


def _merge(outs, lses, zh, hyg, x2d, wa, wh, wo, tm=512):
    t, d = x2d.shape
    seq = outs[0].shape[2]
    per_seq = seq // tm
    tok = lambda w: pl.BlockSpec((tm, w), lambda i: (i, 0))
    full = lambda a: pl.BlockSpec(a.shape, lambda i: (0, 0))
    heads = pl.BlockSpec((None, HEADS_PER_GROUP, tm, HEAD_DIM),
                         lambda i: (i // per_seq, 0, i % per_seq, 0))
    gcol = 3 * HYENA_WIDTH // d
    return pl.pallas_call(
        _merge_kernel,
        grid=(t // tm,),
        in_specs=[heads] * 3 + [tok(HEAD_DIM)] * 3 + [
            tok(HYENA_WIDTH),
            pl.BlockSpec((tm, d), lambda i: (i, gcol)),
            pl.BlockSpec((tm, d), lambda i: (i, gcol + 1)),
            tok(d), full(wa), full(wh), full(wo)],
        out_specs=tok(d),
        out_shape=jax.ShapeDtypeStruct((t, d), F32),
        compiler_params=_cparams(("parallel",)),
        name="merge_out_proj",
    )(*outs, *lses, zh, hyg, hyg, x2d, wa, wh, wo)


def _mlp_kernel(x_ref, g_ref, w1_ref, w2_ref, o_ref, xn_ref, acc_ref):
    j = pl.program_id(1)

    @pl.when(j == 0)
    def _():
        x = x_ref[...]
        ms = jnp.mean(x * x, axis=-1, keepdims=True)
        xn_ref[...] = (x * lax.rsqrt(ms + NORM_EPS) * g_ref[...]).astype(BF16)
        acc_ref[...] = jnp.zeros_like(acc_ref)

    h = jnp.dot(xn_ref[...], w1_ref[...], preferred_element_type=F32)
    a = jnp.square(jnp.maximum(h, 0.0)).astype(BF16)
    acc_ref[...] += jnp.dot(a, w2_ref[...], preferred_element_type=F32)

    @pl.when(j == pl.num_programs(1) - 1)
    def _():
        o_ref[...] = x_ref[...] + acc_ref[...]


def _mlp(x2d, g, w1, w2, tm=1024, tf=512):
    t, d = x2d.shape
    f = w1.shape[1]
    return pl.pallas_call(
        _mlp_kernel,
        grid=(t // tm, f // tf),
        in_specs=[
            pl.BlockSpec((tm, d), lambda i, j: (i, 0)),
            pl.BlockSpec((1, d), lambda i, j: (0, 0)),
            pl.BlockSpec((d, tf), lambda i, j: (0, j)),
            pl.BlockSpec((tf, d), lambda i, j: (j, 0)),
        ],
        out_specs=pl.BlockSpec((tm, d), lambda i, j: (i, 0)),
        out_shape=jax.ShapeDtypeStruct((t, d), F32),
        scratch_shapes=[pltpu.VMEM((tm, d), BF16), pltpu.VMEM((tm, d), F32)],
        compiler_params=_cparams(("parallel", "arbitrary")),
        name="mlp",
    )(x2d, g, w1, w2)


def _encoder_layer(x, p):
    bsz, seq, d = x.shape
    x2d = x.reshape(bsz * seq, d)
    hyg = _norm_matmul(x2d, p['g_mix'], p['w_hyg'], BF16)
    outs, lses = [], []
    for gi, (_, dil) in enumerate(DILATION_GROUPS):
        qkv_g = _qkv_group(x2d, p['g_mix'], p['w_qkv'][gi], p['head_gain'], dil)
        o, s = _attn_group(qkv_g, p['rel_bias'], gi, bsz, seq)
        outs.append(o)
        lses.append(s)
    zh = _hyena(hyg.reshape(bsz, seq, HYG_WIDTH), p['w_short'], p['b_short'], p['filt'], p['filt_skip'])
    x1 = _merge(outs, lses, zh, hyg, x2d, p['wa'], p['wh'], p['wo'])
    y = _mlp(x1, p['g_mlp'], p['w1'], p['w2'])
    return y.reshape(bsz, seq, d)


def kernel(x_prompt, x_sample, rel_bias, g_mix, w_in, g_q, g_k, w_attn_branch, w_short, b_short,
           filt_w1, filt_b1, filt_w2, filt_b2, filt_w3, filt_b3, filt_w4, filt_freq, filt_skip,
           w_hyena_branch, w_out, g_mlp, w_ff1, w_ff2):
    y_prompt, y_sample = x_prompt, x_sample
    for l in range(g_mix.shape[0]):
        w_in_b = w_in[l].astype(BF16)
        w_qkv = [jnp.concatenate([w_in_b[:, s * ATTN_WIDTH + gi * GROUP_WIDTH:s * ATTN_WIDTH + (gi + 1) * GROUP_WIDTH]
                                  for s in range(3)], axis=1) for gi in range(N_GROUPS)]
        head_gain = jnp.concatenate([
            jnp.tile(g_q[l].astype(F32) * (HEAD_DIM ** -0.5), HEADS_PER_GROUP),
            jnp.tile(g_k[l].astype(F32), HEADS_PER_GROUP),
            jnp.ones((GROUP_WIDTH,), F32)])[None, :]
        p = dict(
            g_mix=g_mix[l][None, :].astype(F32),
            w_qkv=w_qkv, w_hyg=w_in_b[:, QKV_WIDTH:],
            head_gain=head_gain,
            rel_bias=rel_bias,
            w_short=w_short[l], b_short=b_short[l][None, :],
            filt=(filt_w1[l], filt_b1[l], filt_w2[l], filt_b2[l], filt_w3[l], filt_b3[l], filt_w4[l],
                  filt_freq[l]),
            filt_skip=filt_skip[l],
            wa=w_attn_branch[l].astype(BF16), wh=w_hyena_branch[l].astype(BF16),
            wo=w_out[l].astype(BF16),
            g_mlp=g_mlp[l][None, :].astype(F32), w1=w_ff1[l].astype(BF16), w2=w_ff2[l].astype(BF16))
        y_prompt = _encoder_layer(y_prompt, p)
        y_sample = _encoder_layer(y_sample, p)
    return (y_prompt, y_sample)
```

```python
import functools
import math

import numpy as np
import jax
import jax.numpy as jnp
from jax import lax
from jax.experimental import pallas as pl
from jax.experimental.pallas import tpu as pltpu

F32 = jnp.float32
BF16 = jnp.bfloat16

D_MODEL = 1024
HEAD_DIM = 128
HEADS_PER_GROUP = 4
DILATION_GROUPS = ((128, 1), (512, 4), (2048, 16))
N_GROUPS = len(DILATION_GROUPS)
ATTN_WIDTH = N_GROUPS * HEADS_PER_GROUP * HEAD_DIM
GROUP_WIDTH = HEADS_PER_GROUP * HEAD_DIM
GROUP_QKV = 3 * GROUP_WIDTH
BAND = 64
N_BUCKETS = 32
MAX_DISTANCE = 1024
HYENA_WIDTH = D_MODEL
HYENA_ORDER = 2
FILTER_BANDS = 16
FILTER_HIDDEN = 64
FILTER_OUT_SCALE = 0.02
DECAY_TARGET = 1e-2
FAST_DECAY_PCT = 0.3
SLOW_DECAY_PCT = 1.5
D_FF = 4 * D_MODEL
QKV_WIDTH = 3 * ATTN_WIDTH
HYG_WIDTH = 3 * HYENA_WIDTH + 2 * D_MODEL
NORM_EPS = 1e-6
MASK_VALUE = -1e30

FFT_N = 4096
FFT_N1 = 32
FFT_N2 = 128
SUB = 8
LANES = 128
HALO = 16
N_CHUNK = FFT_N2 // SUB
SPEC_ROWS = 2 * FFT_N
RADIX8 = 8
N_ALPHA = 5

VMEM_LIMIT = 56 * 1024 * 1024
FUSED_VMEM_LIMIT = 60 * 1024 * 1024


def _cparams(sem):
    return pltpu.CompilerParams(dimension_semantics=sem, vmem_limit_bytes=VMEM_LIMIT)


def _norm_matmul_kernel(x_ref, g_ref, w_ref, o_ref, xn_ref):
    @pl.when(pl.program_id(1) == 0)
    def _():
        x = x_ref[...]
        ms = jnp.mean(x * x, axis=-1, keepdims=True)
        xn_ref[...] = (x * lax.rsqrt(ms + NORM_EPS) * g_ref[...]).astype(BF16)

    o_ref[...] = jnp.dot(xn_ref[...], w_ref[...], preferred_element_type=F32).astype(o_ref.dtype)


def _norm_matmul(x2d, g, w_bf16, out_dtype, tm=1024, tn=512):
    t, d = x2d.shape
    n = w_bf16.shape[1]
    return pl.pallas_call(
        _norm_matmul_kernel,
        grid=(t // tm, n // tn),
        in_specs=[
            pl.BlockSpec((tm, d), lambda i, j: (i, 0)),
            pl.BlockSpec((1, d), lambda i, j: (0, 0)),
            pl.BlockSpec((d, tn), lambda i, j: (0, j)),
        ],
        out_specs=pl.BlockSpec((tm, tn), lambda i, j: (i, j)),
        out_shape=jax.ShapeDtypeStruct((t, n), out_dtype),
        scratch_shapes=[pltpu.VMEM((tm, d), BF16)],
        compiler_params=_cparams(("parallel", "arbitrary")),
        name="norm_matmul",
    )(x2d, g, w_bf16)


def _qkv_group_kernel(x_ref, g_ref, w_ref, hg_ref, o_ref, *scratch, tm, dil):
    rows = tm // dil
    x = x_ref[...]
    ms = jnp.mean(x * x, axis=-1, keepdims=True)
    xn = x * lax.rsqrt(ms + NORM_EPS) * g_ref[...]
    if dil == 1:
        xn = xn.astype(BF16)
    else:
        s_ref, = scratch
        n_slab = xn.shape[1] // LANES
        for c in range(n_slab):
            s_ref[c * tm:(c + 1) * tm, :] = xn[:, c * LANES:(c + 1) * LANES]
        parts = []
        for r in range(dil):
            slabs = [s_ref[pl.ds(c * tm + r, rows, stride=dil), :] for c in range(n_slab)]
            parts.append(jnp.concatenate(slabs, axis=1).astype(BF16))
        xn = jnp.concatenate(parts, axis=0)
    hg = hg_ref[...]
    for jt in range(3):
        acc = jnp.dot(xn, w_ref[:, jt * GROUP_WIDTH:(jt + 1) * GROUP_WIDTH], preferred_element_type=F32)
        if jt < 2:
            heads = []
            for h in range(HEADS_PER_GROUP):
                a = acc[:, h * HEAD_DIM:(h + 1) * HEAD_DIM]
                ms = jnp.mean(a * a, axis=-1, keepdims=True)
                c0 = jt * GROUP_WIDTH + h * HEAD_DIM
                heads.append(a * lax.rsqrt(ms + NORM_EPS) * hg[:, c0:c0 + HEAD_DIM])
            acc = jnp.concatenate(heads, axis=1)
        acc = acc.astype(BF16)
        for r in range(dil):
            c0 = r * GROUP_QKV + jt * GROUP_WIDTH
            o_ref[:, c0:c0 + GROUP_WIDTH] = acc[r * rows:(r + 1) * rows, :]


def _qkv_group(x2d, g, w_g, head_gain_g, dil, tm=1024):
    t, d = x2d.shape
    return pl.pallas_call(
        functools.partial(_qkv_group_kernel, tm=tm, dil=dil),
        grid=(t // tm,),
        in_specs=[
            pl.BlockSpec((tm, d), lambda i: (i, 0)),
            pl.BlockSpec((1, d), lambda i: (0, 0)),
            pl.BlockSpec((d, GROUP_QKV), lambda i: (0, 0)),
            pl.BlockSpec((1, GROUP_QKV), lambda i: (0, 0)),
        ],
        out_specs=pl.BlockSpec((tm // dil, dil * GROUP_QKV), lambda i: (i, 0)),
        out_shape=jax.ShapeDtypeStruct((t // dil, dil * GROUP_QKV), BF16),
        scratch_shapes=[] if dil == 1 else [pltpu.VMEM((tm * d // LANES, LANES), F32)],
        compiler_params=_cparams(("parallel",)),
        name=f"qkv_group_d{dil}",
    )(x2d, g, w_g, head_gain_g)


def _t5_bucket_np(rel):
    nb = N_BUCKETS // 2
    max_exact = nb // 2
    side = np.where(rel > 0, nb, 0)
    n = np.abs(rel)
    nf = np.maximum(n, 1).astype(np.float32)
    large = max_exact + (np.log(nf / np.float32(max_exact)) / np.float32(math.log(MAX_DISTANCE / max_exact))
                         * np.float32(nb - max_exact)).astype(np.int32)
    large = np.minimum(large, nb - 1)
    return side + np.where(n < max_exact, n, large)


def _band_bucket_index(dil):
    qi = np.arange(2 * BAND)[:, None]
    kj = np.arange(4 * BAND)[None, :]
    delta = kj - BAND - qi
    idx = _t5_bucket_np(delta * dil).astype(np.int32)
    return np.where(np.abs(delta) <= BAND, idx, -1).astype(np.int32)


def _attn_kernel(idx_ref, tbl_ref, c_ref, p_ref, n_ref, o_ref, lse_ref, bias_ref, *, tq, m_len, dil):
    qb = 2 * BAND
    wb = 4 * BAND
    nsub = tq // qb
    first = (pl.program_id(0) == 0) & (pl.program_id(1) == 0)

    @pl.when(first)
    def _():
        idx = idx_ref[...]
        for h in range(HEADS_PER_GROUP):
            acc = jnp.full((qb, wb), MASK_VALUE, F32)
            for b in range(N_BUCKETS):
                acc = jnp.where(idx == b, tbl_ref[h, b], acc)
            bias_ref[h] = acc

    i = pl.program_id(1)
    col = lax.broadcasted_iota(jnp.int32, (qb, wb), 1)
    lane = lax.broadcasted_iota(jnp.int32, (qb, HEAD_DIM), 1)

    def window(s, cols):
        lo, hi = s * qb - BAND, s * qb + qb + BAND
        parts = []
        if lo < 0:
            parts.append(p_ref[0, :, cols])
            lo = 0
        parts.append(c_ref[0, lo:min(hi, tq), cols])
        if hi > tq:
            parts.append(n_ref[0, :, cols])
        return parts[0] if len(parts) == 1 else jnp.concatenate(parts, axis=0)

    for s in range(nsub):
        lo = BAND - i * tq - qb * s
        valid = (col >= lo) & (col < m_len + lo)
        for r in range(dil):
            base = r * GROUP_QKV
            lse_tile = jnp.zeros((qb, HEAD_DIM), F32)
            rows = pl.ds(s * qb * dil + r, qb, stride=dil) if dil > 1 else pl.ds(s * qb, qb)
            for h in range(HEADS_PER_GROUP):
                qs = c_ref[0, s * qb:(s + 1) * qb, base + h * HEAD_DIM:base + (h + 1) * HEAD_DIM]
                kk = window(s, slice(base + GROUP_WIDTH + h * HEAD_DIM, base + GROUP_WIDTH + (h + 1) * HEAD_DIM))
                sc = lax.dot_general(qs, kk, (((1,), (1,)), ((), ())), preferred_element_type=F32)
                sc = jnp.where(valid, sc + bias_ref[h], MASK_VALUE)
                mx = jnp.max(sc, axis=-1, keepdims=True)
                p = jnp.exp(sc - mx)
                den = jnp.sum(p, axis=-1, keepdims=True)
                vv = window(s, slice(base + 2 * GROUP_WIDTH + h * HEAD_DIM,
                                     base + 2 * GROUP_WIDTH + (h + 1) * HEAD_DIM))
                o = jnp.dot(p.astype(BF16), vv, preferred_element_type=F32) / den
                o_ref[0, h, rows, :] = o
                lse_tile = jnp.where(lane == h, mx + jnp.log(den), lse_tile)
            lse_ref[0, rows, :] = lse_tile


def _attn_group(qkv_g, rel_bias, gi, bsz, seq):
    _, dil = DILATION_GROUPS[gi]
    m_len = seq // dil
    tq = min(m_len, 512 if dil == 1 else 128)
    nblk = tq // BAND
    n_halo = m_len // BAND
    width = dil * GROUP_QKV
    view = qkv_g.reshape(bsz, m_len, width)
    idx = jnp.asarray(_band_bucket_index(dil))
    tbl = rel_bias[:, gi * HEADS_PER_GROUP:(gi + 1) * HEADS_PER_GROUP].T.astype(F32)
    o, lse = pl.pallas_call(
        functools.partial(_attn_kernel, tq=tq, m_len=m_len, dil=dil),
        grid=(bsz, m_len // tq),
        in_specs=[
            pl.BlockSpec((2 * BAND, 4 * BAND), lambda b, i: (0, 0)),
            pl.BlockSpec(memory_space=pltpu.SMEM),
            pl.BlockSpec((1, tq, width), lambda b, i: (b, i, 0)),
            pl.BlockSpec((1, BAND, width), lambda b, i: (b, jnp.maximum(i * nblk - 1, 0), 0)),
            pl.BlockSpec((1, BAND, width), lambda b, i: (b, jnp.minimum((i + 1) * nblk, n_halo - 1), 0)),
        ],
        out_specs=[
            pl.BlockSpec((1, HEADS_PER_GROUP, tq * dil, HEAD_DIM), lambda b, i: (b, 0, i, 0)),
            pl.BlockSpec((1, tq * dil, HEAD_DIM), lambda b, i: (b, i, 0)),
        ],
        out_shape=[
            jax.ShapeDtypeStruct((bsz, HEADS_PER_GROUP, seq, HEAD_DIM), F32),
            jax.ShapeDtypeStruct((bsz, seq, HEAD_DIM), F32),
        ],
        scratch_shapes=[pltpu.VMEM((HEADS_PER_GROUP, 2 * BAND, 4 * BAND), F32)],
        compiler_params=_cparams(("arbitrary", "arbitrary")),
        name=f"band_attn_g{gi}",
    )(idx, tbl, view, view, view)
    return o, lse.reshape(bsz * seq, HEAD_DIM)


def _short_conv_kernel(c_ref, p_ref, n_ref, w_ref, b_ref, o_ref, *, tl):
    i = pl.program_id(1)
    last = pl.num_programs(1) - 1
    cur = c_ref[0].astype(F32)
    prev_row = jnp.where(i == 0, 0.0, p_ref[0, HALO - 1:HALO, :].astype(F32))
    next_row = jnp.where(i == last, 0.0, n_ref[0, 0:1, :].astype(F32))
    row = lax.broadcasted_iota(jnp.int32, cur.shape, 0)
    up = jnp.where(row == 0, prev_row, pltpu.roll(cur, 1, 0))
    dn = jnp.where(row == tl - 1, next_row, pltpu.roll(cur, tl - 1, 0))
    w = w_ref[...]
    o_ref[0] = up * w[0:1, :] + cur * w[1:2, :] + dn * w[2:3, :] + b_ref[...]


def _short_conv(hyg3, w_short, b_short, tl=512, ct=512):
    bsz, seq, _ = hyg3.shape
    width = w_short.shape[1]
    nsub = tl // HALO
    return pl.pallas_call(
        functools.partial(_short_conv_kernel, tl=tl),
        grid=(bsz, seq // tl, width // ct),
        in_specs=[
            pl.BlockSpec((1, tl, ct), lambda b, i, c: (b, i, c)),
            pl.BlockSpec((1, HALO, ct), lambda b, i, c: (b, jnp.maximum(i * nsub - 1, 0), c)),
            pl.BlockSpec((1, HALO, ct), lambda b, i, c: (b, jnp.minimum((i + 1) * nsub, seq // HALO - 1), c)),
            pl.BlockSpec((3, ct), lambda b, i, c: (0, c)),
            pl.BlockSpec((1, ct), lambda b, i, c: (0, c)),
        ],
        out_specs=pl.BlockSpec((1, tl, ct), lambda b, i, c: (b, i, c)),
        out_shape=jax.ShapeDtypeStruct((bsz, seq, width), F32),
        compiler_params=_cparams(("parallel", "parallel", "parallel")),
        name="short_conv",
    )(hyg3, hyg3, hyg3, w_short, b_short)


def _filter_kernel(w1_ref, b1_ref, w2_ref, b2_ref, w3_ref, b3_ref, fr_ref, w4_ref,
                   ad_ref, o_ref, *, seq, tr):
    hp = lax.Precision.HIGHEST
    n = pl.program_id(0) * tr + lax.broadcasted_iota(jnp.int32, (tr, 128), 0)
    lag = jnp.where(n < seq, n, 2 * seq - n).astype(F32)
    t = lag / (seq - 1)
    ang = (2.0 * math.pi) * lag / seq
    f = lax.broadcasted_iota(jnp.int32, (tr, 128), 1)
    band_idx = jnp.where(f <= FILTER_BANDS, f - 1, f - 1 - FILTER_BANDS).astype(F32)
    bands = 1e-4 + band_idx * ((FILTER_BANDS - 1 - 1e-4) / (FILTER_BANDS - 1))
    arg = bands * ang
    z = jnp.where(f == 0, t,
                  jnp.where(f <= FILTER_BANDS, jnp.cos(arg),
                            jnp.where(f <= 2 * FILTER_BANDS, -jnp.sin(arg), 0.0)))
    fr = fr_ref[...]
    h = jnp.sin(fr * (jnp.dot(z, w1_ref[...], precision=hp, preferred_element_type=F32) + b1_ref[...]))
    h = jnp.sin(fr * (jnp.dot(h, w2_ref[...], precision=hp, preferred_element_type=F32) + b2_ref[...]))
    h = jnp.sin(fr * (jnp.dot(h, w3_ref[...], precision=hp, preferred_element_type=F32) + b3_ref[...]))
    h_hi = h.astype(BF16)
    h_lo = (h - h_hi.astype(F32)).astype(BF16)
    w_hi, w_lo = w4_ref[0], w4_ref[1]
    taps = (jnp.dot(h_hi, w_hi, preferred_element_type=F32) + jnp.dot(h_hi, w_lo, preferred_element_type=F32)
            + jnp.dot(h_lo, w_hi, preferred_element_type=F32))
    decay = jnp.exp(-t[:, 0:1] * ad_ref[...])
    o_ref[...] = jnp.where(n[:, 0:1] == seq, 0.0, taps * decay)


def _filter_taps(seq, w1, b1, w2, b2, w3, b3, w4, freq, tr=256):
    c = HYENA_WIDTH
    hid = FILTER_HIDDEN
    w1p = jnp.zeros((128, hid), F32).at[:w1.shape[0]].set(w1)
    w4r = w4.reshape(hid, HYENA_ORDER, 2, c)
    w4d = jnp.transpose(w4r, (2, 0, 1, 3)).reshape(2, hid, HYENA_ORDER * c)
    w4_hi = w4d.astype(BF16)
    w4_lo = (w4d - w4_hi.astype(F32)).astype(BF16)
    w4d = jnp.stack([w4_hi, w4_lo], axis=1)
    nblk = 2 * seq // tr
    min_decay = math.log(DECAY_TARGET) / SLOW_DECAY_PCT
    max_decay = math.log(DECAY_TARGET) / FAST_DECAY_PCT
    ad = jnp.abs(jnp.linspace(min_decay, max_decay, c, dtype=F32))
    ad = jnp.tile(ad, HYENA_ORDER)[None, :]
    row = lambda v: v.reshape(1, -1)
    full = lambda shape: pl.BlockSpec(shape, lambda i: (0,) * len(shape))
    return pl.pallas_call(
        functools.partial(_filter_kernel, seq=seq, tr=tr),
        grid=(nblk,),
        in_specs=[full((128, hid)), full((1, hid)), full((hid, hid)), full((1, hid)),
                  full((hid, hid)), full((1, hid)), full((1, hid)),
                  pl.BlockSpec((None, 2, hid, HYENA_ORDER * c), lambda i: (i // (nblk // 2), 0, 0, 0)),
                  full((1, HYENA_ORDER * c))],
        out_specs=pl.BlockSpec((tr, HYENA_ORDER * c), lambda i: (i, 0)),
        out_shape=jax.ShapeDtypeStruct((2 * seq, HYENA_ORDER * c), F32),
        compiler_params=_cparams(("parallel",)),
        name="hyena_filter_taps",
    )(w1p, row(b1), w2, row(b2), w3, row(b3), row(freq), w4d, ad)


@functools.lru_cache(maxsize=None)
def _dft_matrices(n1_in, n1_out, n_alpha):
    big = RADIX8 * FFT_N
    a = np.arange(n_alpha).reshape(-1, 1, 1, 1, 1)
    j = np.arange(N_CHUNK).reshape(1, -1, 1, 1, 1)
    k1 = np.arange(FFT_N1).reshape(1, 1, -1, 1, 1)
    t = np.arange(SUB).reshape(1, 1, 1, 1, -1)
    bf16 = jnp.dtype(BF16)

    def stage_a(n1_count, inverse):
        n1 = np.arange(n1_count).reshape(1, 1, 1, -1, 1)
        n2 = SUB * j + t
        e = (RADIX8 * FFT_N2 * n1 * k1 + RADIX8 * n2 * k1 + a * (FFT_N2 * n1 + n2)) % big
        ang = e * (2.0 * np.pi / big)
        cr = np.cos(ang)
        ci = np.sin(ang) if inverse else -np.sin(ang)
        if inverse:
            cr = cr / FFT_N
            ci = ci / FFT_N
        eye = np.eye(SUB)
        if not inverse:
            blk = lambda m: np.einsum('ajknt,ts->ajksnt', m, eye)
            top = np.concatenate([blk(cr), blk(-ci)], axis=4)
            bot = np.concatenate([blk(ci), blk(cr)], axis=4)
            m = np.stack([top, bot], axis=2)
            return m.reshape(n_alpha, N_CHUNK, 2 * FFT_N1 * SUB, 2 * n1_count * SUB).astype(bf16)
        blk = lambda m: np.einsum('ajknt,ts->ajnskt', m, eye)
        top = np.concatenate([blk(cr), blk(-ci)], axis=4)
        bot = np.concatenate([blk(ci), blk(cr)], axis=4)
        m = np.stack([top, bot], axis=2)
        return m.reshape(n_alpha, N_CHUNK, 2 * n1_count * SUB, 2 * FFT_N1 * SUB).astype(bf16)

    kf = stage_a(n1_in, False)
    ki = stage_a(n1_out, True)
    n2 = np.arange(FFT_N2)
    ang = ((n2[:, None] * n2[None, :]) % FFT_N2) * (2.0 * np.pi / FFT_N2)
    gr, gi = np.cos(ang), -np.sin(ang)
    g = np.block([[gr, -gi], [gi, gr]]).astype(bf16)
    ginv = np.block([[gr, gi], [-gi, gr]]).astype(bf16)
    return kf, g, ginv, ki


def _fft_stage_a(z_ref, kf, s1_ref, n1_in):
    def body(j, carry):
        off = pl.multiple_of(j * SUB, SUB)
        chunks = [z_ref[p, pl.ds(FFT_N2 * n1 + off, SUB), :].astype(F32)
                  for p in range(2) for n1 in range(n1_in)]
        slab = jnp.concatenate(chunks, axis=0).astype(BF16)
        a = jnp.dot(kf[j], slab, preferred_element_type=F32)
        for q in range(2):
            for k1 in range(FFT_N1):
                r = (q * FFT_N1 + k1) * SUB
                s1_ref[pl.ds(k1 * 2 * FFT_N2 + q * FFT_N2 + off, SUB), :] = a[r:r + SUB, :]
        return carry
    lax.fori_loop(0, N_CHUNK, body, 0, unroll=2)


def _fft_core_kernel(z_ref, kf_ref, g_ref, gi_ref, ki_ref, h_ref, o_ref, s1_ref, s2_ref, *, n1_in, n1_out):
    _fft_stage_a(z_ref, kf_ref, s1_ref, n1_in)

    def stage_b(k1, carry):
        r0 = pl.multiple_of(k1 * 2 * FFT_N2, 2 * FFT_N2)
        slab = s1_ref[pl.ds(r0, 2 * FFT_N2), :].astype(BF16)
        x = jnp.dot(g_ref[...], slab, preferred_element_type=F32)
        xr, xi = x[:FFT_N2], x[FFT_N2:]
        hr = h_ref[pl.ds(r0, FFT_N2), :]
        hi = h_ref[pl.ds(r0 + FFT_N2, FFT_N2), :]
        y = jnp.concatenate([xr * hr - xi * hi, xr * hi + xi * hr], axis=0).astype(BF16)
        s2_ref[pl.ds(r0, 2 * FFT_N2), :] = jnp.dot(gi_ref[...], y, preferred_element_type=F32)
        return carry
    lax.fori_loop(0, FFT_N1, stage_b, 0, unroll=4)

    def stage_c(j, carry):
        off = pl.multiple_of(j * SUB, SUB)
        chunks = [s2_ref[pl.ds(k1 * 2 * FFT_N2 + q * FFT_N2 + off, SUB), :]
                  for q in range(2) for k1 in range(FFT_N1)]
        slab = jnp.concatenate(chunks, axis=0).astype(BF16)
        y = jnp.dot(ki_ref[j], slab, preferred_element_type=F32)
        for p in range(2):
            for n1 in range(n1_out):
                r = (p * n1_out + n1) * SUB
                o_ref[p, pl.ds(FFT_N2 * n1 + off, SUB), :] = y[r:r + SUB, :]
        return carry
    lax.fori_loop(0, N_CHUNK, stage_c, 0, unroll=2)


def _fft_forward_kernel(z_ref, kf_ref, g_ref, o_ref, s1_ref, *, n1_in):
    _fft_stage_a(z_ref, kf_ref, s1_ref, n1_in)

    def stage_b(k1, carry):
        r0 = pl.multiple_of(k1 * 2 * FFT_N2, 2 * FFT_N2)
        slab = s1_ref[pl.ds(r0, 2 * FFT_N2), :].astype(BF16)
        o_ref[pl.ds(r0, 2 * FFT_N2), :] = jnp.dot(g_ref[...], slab, preferred_element_type=F32)
        return carry
    lax.fori_loop(0, FFT_N1, stage_b, 0, unroll=4)


def _const_spec(shape, index_map):
    return pl.BlockSpec(shape, index_map, pipeline_mode=pl.Buffered(1))


def _fft_conv_pairs(z3, col0, spec, mats, ct):
    kf, g, ginv, ki = mats
    bsz, seq, _ = z3.shape
    c = spec.shape[1]
    n1 = seq // FFT_N2
    cb0 = col0 // ct
    return pl.pallas_call(
        functools.partial(_fft_core_kernel, n1_in=n1, n1_out=n1),
        grid=(c // ct, bsz // 2),
        in_specs=[
            pl.BlockSpec((2, seq, ct), lambda cc, p: (p, 0, cb0 + cc)),
            _const_spec((None,) + kf.shape[1:], lambda cc, p: (0, 0, 0, 0)),
            _const_spec(g.shape, lambda cc, p: (0, 0)),
            _const_spec(ginv.shape, lambda cc, p: (0, 0)),
            _const_spec((None,) + ki.shape[1:], lambda cc, p: (0, 0, 0, 0)),
            _const_spec((SPEC_ROWS, ct), lambda cc, p: (0, cc)),
        ],
        out_specs=pl.BlockSpec((2, seq, ct), lambda cc, p: (p, 0, cc)),
        out_shape=jax.ShapeDtypeStruct((bsz, seq, c), F32),
        scratch_shapes=[pltpu.VMEM((SPEC_ROWS, ct), F32), pltpu.VMEM((SPEC_ROWS, ct), F32)],
        compiler_params=_cparams(("arbitrary", "arbitrary")),
        name="fft_conv_pairs",
    )(z3, kf, g, ginv, ki, spec)


def _fft_conv_residues(u, spec, mats, ct):
    kf, g, ginv, ki = mats
    c = u.shape[-1]
    return pl.pallas_call(
        functools.partial(_fft_core_kernel, n1_in=FFT_N1, n1_out=FFT_N1),
        grid=(N_ALPHA, c // ct),
        in_specs=[
            pl.BlockSpec((None, 2, FFT_N, ct), lambda a, cc: (a, 0, 0, cc)),
            _const_spec((None,) + kf.shape[1:], lambda a, cc: (a, 0, 0, 0)),
            _const_spec(g.shape, lambda a, cc: (0, 0)),
            _const_spec(ginv.shape, lambda a, cc: (0, 0)),
            _const_spec((None,) + ki.shape[1:], lambda a, cc: (a, 0, 0, 0)),
            pl.BlockSpec((None, SPEC_ROWS, ct), lambda a, cc: (a, 0, cc)),
        ],
        out_specs=pl.BlockSpec((None, 2, FFT_N, ct), lambda a, cc: (a, 0, 0, cc)),
        out_shape=jax.ShapeDtypeStruct(u.shape, F32),
        scratch_shapes=[pltpu.VMEM((SPEC_ROWS, ct), F32), pltpu.VMEM((SPEC_ROWS, ct), F32)],
        compiler_params=_cparams(("arbitrary", "arbitrary")),
        name="fft_conv_residues",
    )(u, kf, g, ginv, ki, spec)


def _fft_forward(u, kf, g, ct):
    n_a, _, _, c = u.shape
    return pl.pallas_call(
        functools.partial(_fft_forward_kernel, n1_in=FFT_N1),
        grid=(n_a, c // ct),
        in_specs=[
            pl.BlockSpec((None, 2, FFT_N, ct), lambda a, cc: (a, 0, 0, cc)),
            _const_spec((None,) + kf.shape[1:], lambda a, cc: (a, 0, 0, 0)),
            _const_spec(g.shape, lambda a, cc: (0, 0)),
        ],
        out_specs=pl.BlockSpec((None, SPEC_ROWS, ct), lambda a, cc: (a, 0, cc)),
        out_shape=jax.ShapeDtypeStruct((n_a, SPEC_ROWS, c), F32),
        scratch_shapes=[pltpu.VMEM((SPEC_ROWS, ct), F32)],
        compiler_params=_cparams(("arbitrary", "arbitrary")),
        name="fft_forward",
    )(u, kf, g)


def _radix8_coefs(n_blocks):
    a = np.arange(n_blocks)[None, :]
    al = np.arange(N_ALPHA)[:, None]
    ang = 2.0 * np.pi * ((a * al) % RADIX8) / RADIX8
    return np.round(np.cos(ang), 12), np.round(-np.sin(ang), 12)


def _radix8_split_kernel(x_ref, o_ref, *, n_blocks):
    cr, ci = _radix8_coefs(n_blocks)
    xs = [x_ref[a] for a in range(n_blocks)]
    for al in range(N_ALPHA):
        for part, coef in ((0, cr), (1, ci)):
            acc = None
            for a in range(n_blocks):
                w = float(coef[al, a])
                if w == 0.0:
                    continue
                term = xs[a] if w == 1.0 else (-xs[a] if w == -1.0 else xs[a] * w)
                acc = term if acc is None else acc + term
            o_ref[al, part] = jnp.zeros_like(xs[0]) if acc is None else acc


def _radix8_split(x, col0, width, n_blocks, tr=256, ct=512):
    xv = x.reshape(n_blocks, FFT_N, x.shape[1])
    cb0 = col0 // ct
    return pl.pallas_call(
        functools.partial(_radix8_split_kernel, n_blocks=n_blocks),
        grid=(FFT_N // tr, width // ct),
        in_specs=[pl.BlockSpec((n_blocks, tr, ct), lambda i, c: (0, i, cb0 + c))],
        out_specs=pl.BlockSpec((N_ALPHA, 2, tr, ct), lambda i, c: (0, 0, i, c)),
        out_shape=jax.ShapeDtypeStruct((N_ALPHA, 2, FFT_N, width), F32),
        compiler_params=_cparams(("parallel", "parallel")),
        name="radix8_split",
    )(xv)


def _radix8_merge_gate_kernel(v_ref, z_ref, xg_ref, sk_ref, o_ref, *, n_blocks):
    sk = sk_ref[...]
    for a in range(n_blocks):
        acc = None
        for al in range(N_ALPHA):
            cw = (1.0 if al in (0, RADIX8 // 2) else 2.0) / RADIX8
            ang = 2.0 * np.pi * ((a * al) % RADIX8) / RADIX8
            wr = float(np.round(np.cos(ang), 12)) * cw
            wi = float(np.round(np.sin(ang), 12)) * cw
            for w, part in ((wr, 0), (-wi, 1)):
                if w == 0.0:
                    continue
                term = v_ref[al, part] * w
                acc = term if acc is None else acc + term
        o_ref[a] = xg_ref[a] * (acc + sk * z_ref[a])


def _radix8_merge_gate(v, z, zcol0, xg, gcol0, skip_row, n_blocks, tr=256, ct=512):
    c = v.shape[-1]
    zv = z.reshape(n_blocks, FFT_N, z.shape[1])
    gv = xg.reshape(n_blocks, FFT_N, xg.shape[1])
    zb, gb = zcol0 // ct, gcol0 // ct
    out = pl.pallas_call(
        functools.partial(_radix8_merge_gate_kernel, n_blocks=n_blocks),
        grid=(FFT_N // tr, c // ct),
        in_specs=[
            pl.BlockSpec((N_ALPHA, 2, tr, ct), lambda i, cc: (0, 0, i, cc)),
            pl.BlockSpec((n_blocks, tr, ct), lambda i, cc: (0, i, zb + cc)),
            pl.BlockSpec((n_blocks, tr, ct), lambda i, cc: (0, i, gb + cc)),
            pl.BlockSpec((1, ct), lambda i, cc: (0, cc)),
        ],
        out_specs=pl.BlockSpec((n_blocks, tr, ct), lambda i, cc: (0, i, cc)),
        out_shape=jax.ShapeDtypeStruct((n_blocks, FFT_N, c), F32),
        compiler_params=_cparams(("parallel", "parallel")),
        name="radix8_merge_gate",
    )(v, zv, gv, skip_row)
    return out.reshape(n_blocks * FFT_N, c)


CONV_ROWS = 256


def _conv3_chunk(u_ref, p, i, w, b, seq):
    n_chunks = seq // CONV_ROWS
    r0 = pl.multiple_of(i * CONV_ROWS, CONV_ROWS)
    cur = u_ref[p, pl.ds(r0, CONV_ROWS), :].astype(F32)
    pr = pl.multiple_of(jnp.maximum(r0 - HALO, 0), HALO)
    nr = pl.multiple_of(jnp.minimum(r0 + CONV_ROWS, seq - HALO), HALO)
    prev_row = jnp.where(i == 0, 0.0, u_ref[p, pl.ds(pr, HALO), :][HALO - 1:HALO, :].astype(F32))
    next_row = jnp.where(i == n_chunks - 1, 0.0, u_ref[p, pl.ds(nr, HALO), :][0:1, :].astype(F32))
    row = lax.broadcasted_iota(jnp.int32, cur.shape, 0)
    up = jnp.where(row == 0, prev_row, pltpu.roll(cur, 1, 0))
    dn = jnp.where(row == CONV_ROWS - 1, next_row, pltpu.roll(cur, CONV_ROWS - 1, 0))
    return up * w[0:1, :] + cur * w[1:2, :] + dn * w[2:3, :] + b


def _hyena_pairs_kernel(z_ref, x1_ref, x2_ref, wz_ref, bz_ref, w1_ref, b1_ref, w2_ref, b2_ref, sk_ref,
                        kf_ref, g_ref, gi_ref, ki_ref, h0_ref, h1_ref, o_ref, zin_ref, s1_ref, s2_ref,
                        *, seq):
    n1 = seq // FFT_N2
    n_chunks = seq // CONV_ROWS

    def each_chunk(fn):
        for p in range(2):
            def body(i, carry, p=p):
                fn(p, i, pl.ds(pl.multiple_of(i * CONV_ROWS, CONV_ROWS), CONV_ROWS))
                return carry
            lax.fori_loop(0, n_chunks, body, 0)

    def load_z(p, i, rows):
        zin_ref[p, rows, :] = _conv3_chunk(z_ref, p, i, wz_ref[...], bz_ref[...], seq)
    each_chunk(load_z)

    _fft_core_kernel(zin_ref, kf_ref, g_ref, gi_ref, ki_ref, h0_ref, o_ref, s1_ref, s2_ref, n1_in=n1, n1_out=n1)

    def gate1(p, i, rows):
        xg = _conv3_chunk(x1_ref, p, i, w1_ref[...], b1_ref[...], seq)
        zin_ref[p, rows, :] = xg * (o_ref[p, rows, :] + sk_ref[0:1, :] * zin_ref[p, rows, :])
    each_chunk(gate1)

    _fft_core_kernel(zin_ref, kf_ref, g_ref, gi_ref, ki_ref, h1_ref, o_ref, s1_ref, s2_ref, n1_in=n1, n1_out=n1)

    def gate2(p, i, rows):
        xg = _conv3_chunk(x2_ref, p, i, w2_ref[...], b2_ref[...], seq)
        o_ref[p, rows, :] = xg * (o_ref[p, rows, :] + sk_ref[1:2, :] * zin_ref[p, rows, :])
    each_chunk(gate2)


def _hyena_pairs(hyg3, w_short, b_short, spec, filt_skip, mats, ct=256):
    kf, g, ginv, ki = mats
    bsz, seq, _ = hyg3.shape
    c = HYENA_WIDTH
    nc = c // ct
    hy = lambda k: pl.BlockSpec((2, seq, ct), lambda cc, p: (p, 0, k * nc + cc), pipeline_mode=pl.Buffered(1))
    wsp = lambda k: pl.BlockSpec((3, ct), lambda cc, p: (0, k * nc + cc))
    bsp = lambda k: pl.BlockSpec((1, ct), lambda cc, p: (0, k * nc + cc))
    return pl.pallas_call(
        functools.partial(_hyena_pairs_kernel, seq=seq),
        grid=(nc, bsz // 2),
        in_specs=[
            hy(0), hy(1), hy(2), wsp(0), bsp(0), wsp(1), bsp(1), wsp(2), bsp(2),
            pl.BlockSpec((HYENA_ORDER, ct), lambda cc, p: (0, cc)),
            _const_spec((None,) + kf.shape[1:], lambda cc, p: (0, 0, 0, 0)),
            _const_spec(g.shape, lambda cc, p: (0, 0)),
            _const_spec(ginv.shape, lambda cc, p: (0, 0)),
            _const_spec((None,) + ki.shape[1:], lambda cc, p: (0, 0, 0, 0)),
            _const_spec((SPEC_ROWS, ct), lambda cc, p: (0, cc)),
            _const_spec((SPEC_ROWS, ct), lambda cc, p: (0, nc + cc)),
        ],
        out_specs=pl.BlockSpec((2, seq, ct), lambda cc, p: (p, 0, cc)),
        out_shape=jax.ShapeDtypeStruct((bsz, seq, c), F32),
        scratch_shapes=[pltpu.VMEM((2, seq, ct), F32), pltpu.VMEM((SPEC_ROWS, ct), F32),
                        pltpu.VMEM((SPEC_ROWS, ct), F32)],
        compiler_params=pltpu.CompilerParams(dimension_semantics=("arbitrary", "arbitrary"),
                                             vmem_limit_bytes=FUSED_VMEM_LIMIT),
        name="hyena_pairs",
    )(hyg3, hyg3, hyg3, w_short, b_short, w_short, b_short, w_short, b_short, filt_skip,
      kf, g, ginv, ki, spec, spec)


def _hyena(hyg3, w_short, b_short, filt, filt_skip):
    bsz, seq, _ = hyg3.shape
    c = HYENA_WIDTH
    taps = _filter_taps(seq, *filt)
    if 2 * seq == FFT_N:
        mats = _dft_matrices(seq // FFT_N2, seq // FFT_N2, 1)
        mats_full = _dft_matrices(FFT_N1, FFT_N1, 1)
        u = jnp.stack([taps, jnp.zeros_like(taps)], axis=0)[None]
        spec = _fft_forward(u, mats_full[0], mats_full[1], 256)[0]
        zh = _hyena_pairs(hyg3, w_short, b_short, spec.astype(BF16), filt_skip, mats)
        return zh.reshape(bsz * seq, c)
    hc = _short_conv(hyg3, w_short, b_short)
    hc2 = hc.reshape(bsz * seq, 3 * c)
    assert bsz == 1 and 2 * seq == RADIX8 * FFT_N
    n_blocks = seq // FFT_N
    mats = _dft_matrices(FFT_N1, FFT_N1, N_ALPHA)
    tap_res = _radix8_split(taps, 0, HYENA_ORDER * c, RADIX8)
    spec = _fft_forward(tap_res, mats[0], mats[1], 256)
    z = hc2
    zcol = 0
    for n in range(HYENA_ORDER):
        u = _radix8_split(z, zcol, c, n_blocks)
        v = _fft_conv_residues(u, spec[:, :, n * c:(n + 1) * c], mats, 128)
        z = _radix8_merge_gate(v, z, zcol, hc2, (n + 1) * c, filt_skip[n][None, :], n_blocks)
        zcol = 0
    return z


def _merge_kernel(o0_ref, o1_ref, o2_ref, l0_ref, l1_ref, l2_ref, zh_ref, ga_ref, gh_ref, x_ref,
                  wa_ref, wh_ref, wo_ref, out_ref):
    l0, l1, l2 = l0_ref[...], l1_ref[...], l2_ref[...]
    m = jnp.maximum(jnp.maximum(l0, l1), l2)
    e0, e1, e2 = jnp.exp(l0 - m), jnp.exp(l1 - m), jnp.exp(l2 - m)
    tot = e0 + e1 + e2
    w0, w1, w2 = e0 / tot, e1 / tot, e2 / tot
    heads = []
    for h in range(HEADS_PER_GROUP):
        heads.append(o0_ref[h] * w0[:, h:h + 1] + o1_ref[h] * w1[:, h:h + 1] + o2_ref[h] * w2[:, h:h + 1])
    attn = jnp.concatenate(heads, axis=1).astype(BF16)
    ab = jnp.dot(attn, wa_ref[...], preferred_element_type=F32)
    hb = jnp.dot(zh_ref[...].astype(BF16), wh_ref[...], preferred_element_type=F32)
    merged = (jax.nn.sigmoid(ga_ref[...].astype(F32)) * ab
              + jax.nn.sigmoid(gh_ref[...].astype(F32)) * hb)
    out_ref[...] = x_ref[...] + jnp.dot(merged.astype(BF16), wo_ref[...], preferred_element_type=F32)


def _merge(outs, lses, zh, hyg, x2d, wa, wh, wo, tm=512):
    t, d = x2d.shape
    seq = outs[0].shape[2]
    per_seq = seq // tm
    tok = lambda w: pl.BlockSpec((tm, w), lambda i: (i, 0))
    full = lambda a: pl.BlockSpec(a.shape, lambda i: (0, 0))
    heads = pl.BlockSpec((None, HEADS_PER_GROUP, tm, HEAD_DIM),
                         lambda i: (i // per_seq, 0, i % per_seq, 0))
    gcol = 3 * HYENA_WIDTH // d
    return pl.pallas_call(
        _merge_kernel,
        grid=(t // tm,),
        in_specs=[heads] * 3 + [tok(HEAD_DIM)] * 3 + [
            tok(HYENA_WIDTH),
            pl.BlockSpec((tm, d), lambda i: (i, gcol)),
            pl.BlockSpec((tm, d), lambda i: (i, gcol + 1)),
            tok(d), full(wa), full(wh), full(wo)],
        out_specs=tok(d),
        out_shape=jax.ShapeDtypeStruct((t, d), F32),
        compiler_params=_cparams(("parallel",)),
        name="merge_out_proj",
    )(*outs, *lses, zh, hyg, hyg, x2d, wa, wh, wo)


def _mlp_kernel(x_ref, g_ref, w1_ref, w2_ref, o_ref, xn_ref, acc_ref):
    j = pl.program_id(1)

    @pl.when(j == 0)
    def _():
        x = x_ref[...]
        ms = jnp.mean(x * x, axis=-1, keepdims=True)
        xn_ref[...] = (x * lax.rsqrt(ms + NORM_EPS) * g_ref[...]).astype(BF16)
        acc_ref[...] = jnp.zeros_like(acc_ref)

    h = jnp.dot(xn_ref[...], w1_ref[...], preferred_element_type=F32)
    a = jnp.square(jnp.maximum(h, 0.0)).astype(BF16)
    acc_ref[...] += jnp.dot(a, w2_ref[...], preferred_element_type=F32)

    @pl.when(j == pl.num_programs(1) - 1)
    def _():
        o_ref[...] = x_ref[...] + acc_ref[...]


def _mlp(x2d, g, w1, w2, tm=1024, tf=512):
    t, d = x2d.shape
    f = w1.shape[1]
    return pl.pallas_call(
        _mlp_kernel,
        grid=(t // tm, f // tf),
        in_specs=[
            pl.BlockSpec((tm, d), lambda i, j: (i, 0)),
            pl.BlockSpec((1, d), lambda i, j: (0, 0)),
            pl.BlockSpec((d, tf), lambda i, j: (0, j)),
            pl.BlockSpec((tf, d), lambda i, j: (j, 0)),
        ],
        out_specs=pl.BlockSpec((tm, d), lambda i, j: (i, 0)),
        out_shape=jax.ShapeDtypeStruct((t, d), F32),
        scratch_shapes=[pltpu.VMEM((tm, d), BF16), pltpu.VMEM((tm, d), F32)],
        compiler_params=_cparams(("parallel", "arbitrary")),
        name="mlp",
    )(x2d, g, w1, w2)


def _encoder_layer(x, p):
    bsz, seq, d = x.shape
    x2d = x.reshape(bsz * seq, d)
    hyg = _norm_matmul(x2d, p['g_mix'], p['w_hyg'], BF16)
    outs, lses = [], []
    for gi, (_, dil) in enumerate(DILATION_GROUPS):
        qkv_g = _qkv_group(x2d, p['g_mix'], p['w_qkv'][gi], p['head_gain'], dil)
        o, s = _attn_group(qkv_g, p['rel_bias'], gi, bsz, seq)
        outs.append(o)
        lses.append(s)
    zh = _hyena(hyg.reshape(bsz, seq, HYG_WIDTH), p['w_short'], p['b_short'], p['filt'], p['filt_skip'])
    x1 = _merge(outs, lses, zh, hyg, x2d, p['wa'], p['wh'], p['wo'])
    y = _mlp(x1, p['g_mlp'], p['w1'], p['w2'])
    return y.reshape(bsz, seq, d)


def kernel(x_prompt, x_sample, rel_bias, g_mix, w_in, g_q, g_k, w_attn_branch, w_short, b_short,
           filt_w1, filt_b1, filt_w2, filt_b2, filt_w3, filt_b3, filt_w4, filt_freq, filt_skip,
           w_hyena_branch, w_out, g_mlp, w_ff1, w_ff2):
    y_prompt, y_sample = x_prompt, x_sample
    for l in range(g_mix.shape[0]):
        w_in_b = w_in[l].astype(BF16)
        w_qkv = [jnp.concatenate([w_in_b[:, s * ATTN_WIDTH + gi * GROUP_WIDTH:s * ATTN_WIDTH + (gi + 1) * GROUP_WIDTH]
                                  for s in range(3)], axis=1) for gi in range(N_GROUPS)]
        head_gain = jnp.concatenate([
            jnp.tile(g_q[l].astype(F32) * (HEAD_DIM ** -0.5), HEADS_PER_GROUP),
            jnp.tile(g_k[l].astype(F32), HEADS_PER_GROUP),
            jnp.ones((GROUP_WIDTH,), F32)])[None, :]
        p = dict(
            g_mix=g_mix[l][None, :].astype(F32),
            w_qkv=w_qkv, w_hyg=w_in_b[:, QKV_WIDTH:],
            head_gain=head_gain,
            rel_bias=rel_bias,
            w_short=w_short[l], b_short=b_short[l][None, :],
            filt=(filt_w1[l], filt_b1[l], filt_w2[l], filt_b2[l], filt_w3[l], filt_b3[l], filt_w4[l],
                  filt_freq[l]),
            filt_skip=filt_skip[l],
            wa=w_attn_branch[l].astype(BF16), wh=w_hyena_branch[l].astype(BF16),
            wo=w_out[l].astype(BF16),
            g_mlp=g_mlp[l][None, :].astype(F32), w1=w_ff1[l].astype(BF16), w2=w_ff2[l].astype(BF16))
        y_prompt = _encoder_layer(y_prompt, p)
        y_sample = _encoder_layer(y_sample, p)
    return (y_prompt, y_sample)
```

```python
import functools
import math

import numpy as np
import jax
import jax.numpy as jnp
from jax import lax
from jax.experimental import pallas as pl
from jax.experimental.pallas import tpu as pltpu

F32 = jnp.float32
BF16 = jnp.bfloat16

D_MODEL = 1024
HEAD_DIM = 128
HEADS_PER_GROUP = 4
DILATION_GROUPS = ((128, 1), (512, 4), (2048, 16))
N_GROUPS = len(DILATION_GROUPS)
ATTN_WIDTH = N_GROUPS * HEADS_PER_GROUP * HEAD_DIM
GROUP_WIDTH = HEADS_PER_GROUP * HEAD_DIM
GROUP_QKV = 3 * GROUP_WIDTH
BAND = 64
N_BUCKETS = 32
MAX_DISTANCE = 1024
HYENA_WIDTH = D_MODEL
HYENA_ORDER = 2
FILTER_BANDS = 16
FILTER_HIDDEN = 64
FILTER_OUT_SCALE = 0.02
DECAY_TARGET = 1e-2
FAST_DECAY_PCT = 0.3
SLOW_DECAY_PCT = 1.5
D_FF = 4 * D_MODEL
QKV_WIDTH = 3 * ATTN_WIDTH
HYG_WIDTH = 3 * HYENA_WIDTH + 2 * D_MODEL
NORM_EPS = 1e-6
MASK_VALUE = -1e30

FFT_N = 4096
FFT_N1 = 32
FFT_N2 = 128
SUB = 8
LANES = 128
HALO = 16
N_CHUNK = FFT_N2 // SUB
SPEC_ROWS = 2 * FFT_N
RADIX8 = 8
N_ALPHA = 5

VMEM_LIMIT = 56 * 1024 * 1024
FUSED_VMEM_LIMIT = 60 * 1024 * 1024


def _cparams(sem):
    return pltpu.CompilerParams(dimension_semantics=sem, vmem_limit_bytes=VMEM_LIMIT)


def _norm_matmul_kernel(x_ref, g_ref, w_ref, o_ref, xn_ref):
    @pl.when(pl.program_id(1) == 0)
    def _():
        x = x_ref[...]
        ms = jnp.mean(x * x, axis=-1, keepdims=True)
        xn_ref[...] = (x * lax.rsqrt(ms + NORM_EPS) * g_ref[...]).astype(BF16)

    o_ref[...] = jnp.dot(xn_ref[...], w_ref[...], preferred_element_type=F32).astype(o_ref.dtype)


def _norm_matmul(x2d, g, w_bf16, out_dtype, tm=2048, tn=1280):
    t, d = x2d.shape
    n = w_bf16.shape[1]
    return pl.pallas_call(
        _norm_matmul_kernel,
        grid=(t // tm, n // tn),
        in_specs=[
            pl.BlockSpec((tm, d), lambda i, j: (i, 0)),
            pl.BlockSpec((1, d), lambda i, j: (0, 0)),
            pl.BlockSpec((d, tn), lambda i, j: (0, j)),
        ],
        out_specs=pl.BlockSpec((tm, tn), lambda i, j: (i, j)),
        out_shape=jax.ShapeDtypeStruct((t, n), out_dtype),
        scratch_shapes=[pltpu.VMEM((tm, d), BF16)],
        compiler_params=_cparams(("parallel", "arbitrary")),
        name="norm_matmul",
    )(x2d, g, w_bf16)


def _qkv_group_kernel(x_ref, g_ref, w_ref, hg_ref, o_ref, *scratch, tm, dil):
    rows = tm // dil
    x = x_ref[...]
    ms = jnp.mean(x * x, axis=-1, keepdims=True)
    xn = x * lax.rsqrt(ms + NORM_EPS) * g_ref[...]
    if dil == 1:
        xn = xn.astype(BF16)
    else:
        s_ref, = scratch
        n_slab = xn.shape[1] // LANES
        for c in range(n_slab):
            s_ref[c * tm:(c + 1) * tm, :] = xn[:, c * LANES:(c + 1) * LANES]
        parts = []
        for r in range(dil):
            slabs = [s_ref[pl.ds(c * tm + r, rows, stride=dil), :] for c in range(n_slab)]
            parts.append(jnp.concatenate(slabs, axis=1).astype(BF16))
        xn = jnp.concatenate(parts, axis=0)
    hg = hg_ref[...]
    for jt in range(3):
        acc = jnp.dot(xn, w_ref[:, jt * GROUP_WIDTH:(jt + 1) * GROUP_WIDTH], preferred_element_type=F32)
        if jt < 2:
            heads = []
            for h in range(HEADS_PER_GROUP):
                a = acc[:, h * HEAD_DIM:(h + 1) * HEAD_DIM]
                ms = jnp.mean(a * a, axis=-1, keepdims=True)
                c0 = jt * GROUP_WIDTH + h * HEAD_DIM
                heads.append(a * lax.rsqrt(ms + NORM_EPS) * hg[:, c0:c0 + HEAD_DIM])
            acc = jnp.concatenate(heads, axis=1)
        acc = acc.astype(BF16)
        for r in range(dil):
            c0 = r * GROUP_QKV + jt * GROUP_WIDTH
            o_ref[:, c0:c0 + GROUP_WIDTH] = acc[r * rows:(r + 1) * rows, :]


def _qkv_group(x2d, g, w_g, head_gain_g, dil, tm=1024):
    t, d = x2d.shape
    return pl.pallas_call(
        functools.partial(_qkv_group_kernel, tm=tm, dil=dil),
        grid=(t // tm,),
        in_specs=[
            pl.BlockSpec((tm, d), lambda i: (i, 0)),
            pl.BlockSpec((1, d), lambda i: (0, 0)),
            pl.BlockSpec((d, GROUP_QKV), lambda i: (0, 0)),
            pl.BlockSpec((1, GROUP_QKV), lambda i: (0, 0)),
        ],
        out_specs=pl.BlockSpec((tm // dil, dil * GROUP_QKV), lambda i: (i, 0)),
        out_shape=jax.ShapeDtypeStruct((t // dil, dil * GROUP_QKV), BF16),
        scratch_shapes=[] if dil == 1 else [pltpu.VMEM((tm * d // LANES, LANES), F32)],
        compiler_params=_cparams(("parallel",)),
        name=f"qkv_group_d{dil}",
    )(x2d, g, w_g, head_gain_g)


def _t5_bucket_np(rel):
    nb = N_BUCKETS // 2
    max_exact = nb // 2
    side = np.where(rel > 0, nb, 0)
    n = np.abs(rel)
    nf = np.maximum(n, 1).astype(np.float32)
    large = max_exact + (np.log(nf / np.float32(max_exact)) / np.float32(math.log(MAX_DISTANCE / max_exact))
                         * np.float32(nb - max_exact)).astype(np.int32)
    large = np.minimum(large, nb - 1)
    return side + np.where(n < max_exact, n, large)


def _band_bucket_index(dil):
    qi = np.arange(2 * BAND)[:, None]
    kj = np.arange(4 * BAND)[None, :]
    delta = kj - BAND - qi
    idx = _t5_bucket_np(delta * dil).astype(np.int32)
    return np.where(np.abs(delta) <= BAND, idx, -1).astype(np.int32)


def _attn_kernel(idx_ref, tbl_ref, c_ref, p_ref, n_ref, o_ref, lse_ref, bias_ref, *, tq, m_len, dil):
    qb = 2 * BAND
    wb = 4 * BAND
    nsub = tq // qb
    first = (pl.program_id(0) == 0) & (pl.program_id(1) == 0)

    @pl.when(first)
    def _():
        idx = idx_ref[...]
        for h in range(HEADS_PER_GROUP):
            acc = jnp.full((qb, wb), MASK_VALUE, F32)
            for b in range(N_BUCKETS):
                acc = jnp.where(idx == b, tbl_ref[h, b], acc)
            bias_ref[h] = acc

    i = pl.program_id(1)
    col = lax.broadcasted_iota(jnp.int32, (qb, wb), 1)
    lane = lax.broadcasted_iota(jnp.int32, (qb, HEAD_DIM), 1)

    def window(s, cols):
        lo, hi = s * qb - BAND, s * qb + qb + BAND
        parts = []
        if lo < 0:
            parts.append(p_ref[0, :, cols])
            lo = 0
        parts.append(c_ref[0, lo:min(hi, tq), cols])
        if hi > tq:
            parts.append(n_ref[0, :, cols])
        return parts[0] if len(parts) == 1 else jnp.concatenate(parts, axis=0)

    for s in range(nsub):
        lo = BAND - i * tq - qb * s
        valid = (col >= lo) & (col < m_len + lo)
        for r in range(dil):
            base = r * GROUP_QKV
            lse_tile = jnp.zeros((qb, HEAD_DIM), F32)
            rows = pl.ds(s * qb * dil + r, qb, stride=dil) if dil > 1 else pl.ds(s * qb, qb)
            for h in range(HEADS_PER_GROUP):
                qs = c_ref[0, s * qb:(s + 1) * qb, base + h * HEAD_DIM:base + (h + 1) * HEAD_DIM]
                kk = window(s, slice(base + GROUP_WIDTH + h * HEAD_DIM, base + GROUP_WIDTH + (h + 1) * HEAD_DIM))
                sc = lax.dot_general(qs, kk, (((1,), (1,)), ((), ())), preferred_element_type=F32)
                sc = jnp.where(valid, sc + bias_ref[h], MASK_VALUE)
                mx = jnp.max(sc, axis=-1, keepdims=True)
                p = jnp.exp(sc - mx)
                den = jnp.sum(p, axis=-1, keepdims=True)
                vv = window(s, slice(base + 2 * GROUP_WIDTH + h * HEAD_DIM,
                                     base + 2 * GROUP_WIDTH + (h + 1) * HEAD_DIM))
                o = jnp.dot(p.astype(BF16), vv, preferred_element_type=F32) / den
                o_ref[0, h, rows, :] = o
                lse_tile = jnp.where(lane == h, mx + jnp.log(den), lse_tile)
            lse_ref[0, rows, :] = lse_tile


def _attn_group(qkv_g, rel_bias, gi, bsz, seq):
    _, dil = DILATION_GROUPS[gi]
    m_len = seq // dil
    tq = min(m_len, 512 if dil == 1 else 128)
    nblk = tq // BAND
    n_halo = m_len // BAND
    width = dil * GROUP_QKV
    view = qkv_g.reshape(bsz, m_len, width)
    idx = jnp.asarray(_band_bucket_index(dil))
    tbl = rel_bias[:, gi * HEADS_PER_GROUP:(gi + 1) * HEADS_PER_GROUP].T.astype(F32)
    o, lse = pl.pallas_call(
        functools.partial(_attn_kernel, tq=tq, m_len=m_len, dil=dil),
        grid=(bsz, m_len // tq),
        in_specs=[
            pl.BlockSpec((2 * BAND, 4 * BAND), lambda b, i: (0, 0)),
            pl.BlockSpec(memory_space=pltpu.SMEM),
            pl.BlockSpec((1, tq, width), lambda b, i: (b, i, 0)),
            pl.BlockSpec((1, BAND, width), lambda b, i: (b, jnp.maximum(i * nblk - 1, 0), 0)),
            pl.BlockSpec((1, BAND, width), lambda b, i: (b, jnp.minimum((i + 1) * nblk, n_halo - 1), 0)),
        ],
        out_specs=[
            pl.BlockSpec((1, HEADS_PER_GROUP, tq * dil, HEAD_DIM), lambda b, i: (b, 0, i, 0)),
            pl.BlockSpec((1, tq * dil, HEAD_DIM), lambda b, i: (b, i, 0)),
        ],
        out_shape=[
            jax.ShapeDtypeStruct((bsz, HEADS_PER_GROUP, seq, HEAD_DIM), F32),
            jax.ShapeDtypeStruct((bsz, seq, HEAD_DIM), F32),
        ],
        scratch_shapes=[pltpu.VMEM((HEADS_PER_GROUP, 2 * BAND, 4 * BAND), F32)],
        compiler_params=_cparams(("arbitrary", "arbitrary")),
        name=f"band_attn_g{gi}",
    )(idx, tbl, view, view, view)
    return o, lse.reshape(bsz * seq, HEAD_DIM)


def _short_conv_kernel(c_ref, p_ref, n_ref, w_ref, b_ref, o_ref, *, tl):
    i = pl.program_id(1)
    last = pl.num_programs(1) - 1
    cur = c_ref[0].astype(F32)
    prev_row = jnp.where(i == 0, 0.0, p_ref[0, HALO - 1:HALO, :].astype(F32))
    next_row = jnp.where(i == last, 0.0, n_ref[0, 0:1, :].astype(F32))
    row = lax.broadcasted_iota(jnp.int32, cur.shape, 0)
    up = jnp.where(row == 0, prev_row, pltpu.roll(cur, 1, 0))
    dn = jnp.where(row == tl - 1, next_row, pltpu.roll(cur, tl - 1, 0))
    w = w_ref[...]
    o_ref[0] = up * w[0:1, :] + cur * w[1:2, :] + dn * w[2:3, :] + b_ref[...]


def _short_conv(hyg3, w_short, b_short, tl=512, ct=512):
    bsz, seq, _ = hyg3.shape
    width = w_short.shape[1]
    nsub = tl // HALO
    return pl.pallas_call(
        functools.partial(_short_conv_kernel, tl=tl),
        grid=(bsz, seq // tl, width // ct),
        in_specs=[
            pl.BlockSpec((1, tl, ct), lambda b, i, c: (b, i, c)),
            pl.BlockSpec((1, HALO, ct), lambda b, i, c: (b, jnp.maximum(i * nsub - 1, 0), c)),
            pl.BlockSpec((1, HALO, ct), lambda b, i, c: (b, jnp.minimum((i + 1) * nsub, seq // HALO - 1), c)),
            pl.BlockSpec((3, ct), lambda b, i, c: (0, c)),
            pl.BlockSpec((1, ct), lambda b, i, c: (0, c)),
        ],
        out_specs=pl.BlockSpec((1, tl, ct), lambda b, i, c: (b, i, c)),
        out_shape=jax.ShapeDtypeStruct((bsz, seq, width), F32),
        compiler_params=_cparams(("parallel", "parallel", "parallel")),
        name="short_conv",
    )(hyg3, hyg3, hyg3, w_short, b_short)


def _filter_kernel(w1_ref, b1_ref, w2_ref, b2_ref, w3_ref, b3_ref, fr_ref, w4_ref,
                   ad_ref, o_ref, *, seq, tr):
    hp = lax.Precision.HIGHEST
    n = pl.program_id(0) * tr + lax.broadcasted_iota(jnp.int32, (tr, 128), 0)
    lag = jnp.where(n < seq, n, 2 * seq - n).astype(F32)
    t = lag / (seq - 1)
    ang = (2.0 * math.pi) * lag / seq
    f = lax.broadcasted_iota(jnp.int32, (tr, 128), 1)
    band_idx = jnp.where(f <= FILTER_BANDS, f - 1, f - 1 - FILTER_BANDS).astype(F32)
    bands = 1e-4 + band_idx * ((FILTER_BANDS - 1 - 1e-4) / (FILTER_BANDS - 1))
    arg = bands * ang
    z = jnp.where(f == 0, t,
                  jnp.where(f <= FILTER_BANDS, jnp.cos(arg),
                            jnp.where(f <= 2 * FILTER_BANDS, -jnp.sin(arg), 0.0)))
    fr = fr_ref[...]
    h = jnp.sin(fr * (jnp.dot(z, w1_ref[...], precision=hp, preferred_element_type=F32) + b1_ref[...]))
    h = jnp.sin(fr * (jnp.dot(h, w2_ref[...], precision=hp, preferred_element_type=F32) + b2_ref[...]))
    h = jnp.sin(fr * (jnp.dot(h, w3_ref[...], precision=hp, preferred_element_type=F32) + b3_ref[...]))
    h_hi = h.astype(BF16)
    h_lo = (h - h_hi.astype(F32)).astype(BF16)
    w_hi, w_lo = w4_ref[0], w4_ref[1]
    taps = (jnp.dot(h_hi, w_hi, preferred_element_type=F32) + jnp.dot(h_hi, w_lo, preferred_element_type=F32)
            + jnp.dot(h_lo, w_hi, preferred_element_type=F32))
    decay = jnp.exp(-t[:, 0:1] * ad_ref[...])
    o_ref[...] = jnp.where(n[:, 0:1] == seq, 0.0, taps * decay)


def _filter_taps(seq, w1, b1, w2, b2, w3, b3, w4, freq, tr=256):
    c = HYENA_WIDTH
    hid = FILTER_HIDDEN
    w1p = jnp.zeros((128, hid), F32).at[:w1.shape[0]].set(w1)
    w4r = w4.reshape(hid, HYENA_ORDER, 2, c)
    w4d = jnp.transpose(w4r, (2, 0, 1, 3)).reshape(2, hid, HYENA_ORDER * c)
    w4_hi = w4d.astype(BF16)
    w4_lo = (w4d - w4_hi.astype(F32)).astype(BF16)
    w4d = jnp.stack([w4_hi, w4_lo], axis=1)
    nblk = 2 * seq // tr
    min_decay = math.log(DECAY_TARGET) / SLOW_DECAY_PCT
    max_decay = math.log(DECAY_TARGET) / FAST_DECAY_PCT
    ad = jnp.abs(jnp.linspace(min_decay, max_decay, c, dtype=F32))
    ad = jnp.tile(ad, HYENA_ORDER)[None, :]
    row = lambda v: v.reshape(1, -1)
    full = lambda shape: pl.BlockSpec(shape, lambda i: (0,) * len(shape))
    return pl.pallas_call(
        functools.partial(_filter_kernel, seq=seq, tr=tr),
        grid=(nblk,),
        in_specs=[full((128, hid)), full((1, hid)), full((hid, hid)), full((1, hid)),
                  full((hid, hid)), full((1, hid)), full((1, hid)),
                  pl.BlockSpec((None, 2, hid, HYENA_ORDER * c), lambda i: (i // (nblk // 2), 0, 0, 0)),
                  full((1, HYENA_ORDER * c))],
        out_specs=pl.BlockSpec((tr, HYENA_ORDER * c), lambda i: (i, 0)),
        out_shape=jax.ShapeDtypeStruct((2 * seq, HYENA_ORDER * c), F32),
        compiler_params=_cparams(("parallel",)),
        name="hyena_filter_taps",
    )(w1p, row(b1), w2, row(b2), w3, row(b3), row(freq), w4d, ad)


@functools.lru_cache(maxsize=None)
def _dft_matrices(n1_in, n1_out, n_alpha):
    big = RADIX8 * FFT_N
    a = np.arange(n_alpha).reshape(-1, 1, 1, 1, 1)
    j = np.arange(N_CHUNK).reshape(1, -1, 1, 1, 1)
    k1 = np.arange(FFT_N1).reshape(1, 1, -1, 1, 1)
    t = np.arange(SUB).reshape(1, 1, 1, 1, -1)
    bf16 = jnp.dtype(BF16)

    def stage_a(n1_count, inverse):
        n1 = np.arange(n1_count).reshape(1, 1, 1, -1, 1)
        n2 = SUB * j + t
        e = (RADIX8 * FFT_N2 * n1 * k1 + RADIX8 * n2 * k1 + a * (FFT_N2 * n1 + n2)) % big
        ang = e * (2.0 * np.pi / big)
        cr = np.cos(ang)
        ci = np.sin(ang) if inverse else -np.sin(ang)
        if inverse:
            cr = cr / FFT_N
            ci = ci / FFT_N
        eye = np.eye(SUB)
        if not inverse:
            blk = lambda m: np.einsum('ajknt,ts->ajksnt', m, eye)
            top = np.concatenate([blk(cr), blk(-ci)], axis=4)
            bot = np.concatenate([blk(ci), blk(cr)], axis=4)
            m = np.stack([top, bot], axis=2)
            return m.reshape(n_alpha, N_CHUNK, 2 * FFT_N1 * SUB, 2 * n1_count * SUB).astype(bf16)
        blk = lambda m: np.einsum('ajknt,ts->ajnskt', m, eye)
        top = np.concatenate([blk(cr), blk(-ci)], axis=4)
        bot = np.concatenate([blk(ci), blk(cr)], axis=4)
        m = np.stack([top, bot], axis=2)
        return m.reshape(n_alpha, N_CHUNK, 2 * n1_count * SUB, 2 * FFT_N1 * SUB).astype(bf16)

    kf = stage_a(n1_in, False)
    ki = stage_a(n1_out, True)
    n2 = np.arange(FFT_N2)
    ang = ((n2[:, None] * n2[None, :]) % FFT_N2) * (2.0 * np.pi / FFT_N2)
    gr, gi = np.cos(ang), -np.sin(ang)
    g = np.block([[gr, -gi], [gi, gr]]).astype(bf16)
    ginv = np.block([[gr, gi], [-gi, gr]]).astype(bf16)
    return kf, g, ginv, ki


def _fft_stage_a(z_ref, kf, s1_ref, n1_in):
    def body(j, carry):
        off = pl.multiple_of(j * SUB, SUB)
        chunks = [z_ref[p, pl.ds(FFT_N2 * n1 + off, SUB), :].astype(F32)
                  for p in range(2) for n1 in range(n1_in)]
        slab = jnp.concatenate(chunks, axis=0).astype(BF16)
        a = jnp.dot(kf[j], slab, preferred_element_type=F32)
        for q in range(2):
            for k1 in range(FFT_N1):
                r = (q * FFT_N1 + k1) * SUB
                s1_ref[pl.ds(k1 * 2 * FFT_N2 + q * FFT_N2 + off, SUB), :] = a[r:r + SUB, :]
        return carry
    lax.fori_loop(0, N_CHUNK, body, 0, unroll=2)


def _fft_core_kernel(z_ref, kf_ref, g_ref, gi_ref, ki_ref, h_ref, o_ref, s1_ref, s2_ref, *, n1_in, n1_out):
    _fft_stage_a(z_ref, kf_ref, s1_ref, n1_in)

    def stage_b(k1, carry):
        r0 = pl.multiple_of(k1 * 2 * FFT_N2, 2 * FFT_N2)
        slab = s1_ref[pl.ds(r0, 2 * FFT_N2), :].astype(BF16)
        x = jnp.dot(g_ref[...], slab, preferred_element_type=F32)
        xr, xi = x[:FFT_N2], x[FFT_N2:]
        hr = h_ref[pl.ds(r0, FFT_N2), :]
        hi = h_ref[pl.ds(r0 + FFT_N2, FFT_N2), :]
        y = jnp.concatenate([xr * hr - xi * hi, xr * hi + xi * hr], axis=0).astype(BF16)
        s2_ref[pl.ds(r0, 2 * FFT_N2), :] = jnp.dot(gi_ref[...], y, preferred_element_type=F32)
        return carry
    lax.fori_loop(0, FFT_N1, stage_b, 0, unroll=4)

    def stage_c(j, carry):
        off = pl.multiple_of(j * SUB, SUB)
        chunks = [s2_ref[pl.ds(k1 * 2 * FFT_N2 + q * FFT_N2 + off, SUB), :]
                  for q in range(2) for k1 in range(FFT_N1)]
        slab = jnp.concatenate(chunks, axis=0).astype(BF16)
        y = jnp.dot(ki_ref[j], slab, preferred_element_type=F32)
        for p in range(2):
            for n1 in range(n1_out):
                r = (p * n1_out + n1) * SUB
                o_ref[p, pl.ds(FFT_N2 * n1 + off, SUB), :] = y[r:r + SUB, :]
        return carry
    lax.fori_loop(0, N_CHUNK, stage_c, 0, unroll=2)


def _fft_core_full_kernel(z_ref, kf_ref, g_ref, gi_ref, ki_ref, h_ref, o_ref, s1_ref):
    half = FFT_N1 // 2
    _fft_stage_a(z_ref, kf_ref, s1_ref, FFT_N1)

    def stage_b(k1, carry):
        r0 = pl.multiple_of(k1 * 2 * FFT_N2, 2 * FFT_N2)
        slab = s1_ref[pl.ds(r0, 2 * FFT_N2), :].astype(BF16)
        x = jnp.dot(g_ref[...], slab, preferred_element_type=F32)
        xr, xi = x[:FFT_N2], x[FFT_N2:]
        hr = h_ref[pl.ds(r0, FFT_N2), :]
        hi = h_ref[pl.ds(r0 + FFT_N2, FFT_N2), :]
        y = jnp.concatenate([xr * hr - xi * hi, xr * hi + xi * hr], axis=0).astype(BF16)
        ro = pl.multiple_of((k1 % half) * 2 * FFT_N2, 2 * FFT_N2)
        o_ref[k1 // half, pl.ds(ro, 2 * FFT_N2), :] = jnp.dot(gi_ref[...], y, preferred_element_type=F32)
        return carry
    lax.fori_loop(0, FFT_N1, stage_b, 0, unroll=4)

    def stage_c(j, carry):
        off = pl.multiple_of(j * SUB, SUB)
        chunks = [o_ref[k1 // half, pl.ds((k1 % half) * 2 * FFT_N2 + q * FFT_N2 + off, SUB), :]
                  for q in range(2) for k1 in range(FFT_N1)]
        slab = jnp.concatenate(chunks, axis=0).astype(BF16)
        y = jnp.dot(ki_ref[j], slab, preferred_element_type=F32)
        for p in range(2):
            for n1 in range(FFT_N1):
                r = (p * FFT_N1 + n1) * SUB
                s1_ref[pl.ds(p * FFT_N + FFT_N2 * n1 + off, SUB), :] = y[r:r + SUB, :]
        return carry
    lax.fori_loop(0, N_CHUNK, stage_c, 0, unroll=2)

    def copy_out(i, carry):
        r0 = pl.multiple_of(i * CONV_ROWS, CONV_ROWS)
        for p in range(2):
            o_ref[p, pl.ds(r0, CONV_ROWS), :] = s1_ref[pl.ds(p * FFT_N + r0, CONV_ROWS), :]
        return carry
    lax.fori_loop(0, FFT_N // CONV_ROWS, copy_out, 0)


def _fft_forward_kernel(z_ref, kf_ref, g_ref, o_ref, s1_ref, *, n1_in):
    _fft_stage_a(z_ref, kf_ref, s1_ref, n1_in)

    def stage_b(k1, carry):
        r0 = pl.multiple_of(k1 * 2 * FFT_N2, 2 * FFT_N2)
        slab = s1_ref[pl.ds(r0, 2 * FFT_N2), :].astype(BF16)
        o_ref[pl.ds(r0, 2 * FFT_N2), :] = jnp.dot(g_ref[...], slab, preferred_element_type=F32).astype(o_ref.dtype)
        return carry
    lax.fori_loop(0, FFT_N1, stage_b, 0, unroll=4)


def _const_spec(shape, index_map):
    return pl.BlockSpec(shape, index_map, pipeline_mode=pl.Buffered(1))


def _fft_conv_residues(u, spec, col0, mats, ct=256):
    kf, g, ginv, ki = mats
    c = u.shape[-1]
    cb0 = col0 // ct
    return pl.pallas_call(
        _fft_core_full_kernel,
        grid=(N_ALPHA, c // ct),
        in_specs=[
            _const_spec((None, 2, FFT_N, ct), lambda a, cc: (a, 0, 0, cc)),
            _const_spec((None,) + kf.shape[1:], lambda a, cc: (a, 0, 0, 0)),
            _const_spec(g.shape, lambda a, cc: (0, 0)),
            _const_spec(ginv.shape, lambda a, cc: (0, 0)),
            _const_spec((None,) + ki.shape[1:], lambda a, cc: (a, 0, 0, 0)),
            pl.BlockSpec((None, SPEC_ROWS, ct), lambda a, cc: (a, 0, cb0 + cc)),
        ],
        out_specs=pl.BlockSpec((None, 2, FFT_N, ct), lambda a, cc: (a, 0, 0, cc)),
        out_shape=jax.ShapeDtypeStruct(u.shape, F32),
        scratch_shapes=[pltpu.VMEM((SPEC_ROWS, ct), F32)],
        compiler_params=pltpu.CompilerParams(dimension_semantics=("arbitrary", "arbitrary"),
                                             vmem_limit_bytes=FUSED_VMEM_LIMIT),
        name="fft_conv_residues",
    )(u, kf, g, ginv, ki, spec)


def _fft_forward(u, kf, g, ct):
    n_a, _, _, c = u.shape
    return pl.pallas_call(
        functools.partial(_fft_forward_kernel, n1_in=FFT_N1),
        grid=(n_a, c // ct),
        in_specs=[
            pl.BlockSpec((None, 2, FFT_N, ct), lambda a, cc: (a, 0, 0, cc)),
            _const_spec((None,) + kf.shape[1:], lambda a, cc: (a, 0, 0, 0)),
            _const_spec(g.shape, lambda a, cc: (0, 0)),
        ],
        out_specs=pl.BlockSpec((None, SPEC_ROWS, ct), lambda a, cc: (a, 0, cc)),
        out_shape=jax.ShapeDtypeStruct((n_a, SPEC_ROWS, c), BF16),
        scratch_shapes=[pltpu.VMEM((SPEC_ROWS, ct), F32)],
        compiler_params=_cparams(("arbitrary", "arbitrary")),
        name="fft_forward",
    )(u, kf, g)


def _radix8_coefs(n_blocks):
    a = np.arange(n_blocks)[None, :]
    al = np.arange(N_ALPHA)[:, None]
    ang = 2.0 * np.pi * ((a * al) % RADIX8) / RADIX8
    return np.round(np.cos(ang), 12), np.round(-np.sin(ang), 12)


def _radix8_split_kernel(x_ref, o_ref, *, n_blocks):
    cr, ci = _radix8_coefs(n_blocks)
    xs = [x_ref[a] for a in range(n_blocks)]
    for al in range(N_ALPHA):
        for part, coef in ((0, cr), (1, ci)):
            acc = None
            for a in range(n_blocks):
                w = float(coef[al, a])
                if w == 0.0:
                    continue
                term = xs[a] if w == 1.0 else (-xs[a] if w == -1.0 else xs[a] * w)
                acc = term if acc is None else acc + term
            o_ref[al, part] = jnp.zeros_like(xs[0]) if acc is None else acc


def _radix8_split(x, col0, width, n_blocks, tr=256, ct=512):
    xv = x.reshape(n_blocks, FFT_N, x.shape[1])
    cb0 = col0 // ct
    return pl.pallas_call(
        functools.partial(_radix8_split_kernel, n_blocks=n_blocks),
        grid=(FFT_N // tr, width // ct),
        in_specs=[pl.BlockSpec((n_blocks, tr, ct), lambda i, c: (0, i, cb0 + c))],
        out_specs=pl.BlockSpec((N_ALPHA, 2, tr, ct), lambda i, c: (0, 0, i, c)),
        out_shape=jax.ShapeDtypeStruct((N_ALPHA, 2, FFT_N, width), F32),
        compiler_params=_cparams(("parallel", "parallel")),
        name="radix8_split",
    )(xv)


def _radix8_merge_gate_kernel(v_ref, z_ref, xg_ref, sk_ref, o_ref, *, n_blocks):
    sk = sk_ref[...]
    for a in range(n_blocks):
        acc = None
        for al in range(N_ALPHA):
            cw = (1.0 if al in (0, RADIX8 // 2) else 2.0) / RADIX8
            ang = 2.0 * np.pi * ((a * al) % RADIX8) / RADIX8
            wr = float(np.round(np.cos(ang), 12)) * cw
            wi = float(np.round(np.sin(ang), 12)) * cw
            for w, part in ((wr, 0), (-wi, 1)):
                if w == 0.0:
                    continue
                term = v_ref[al, part] * w
                acc = term if acc is None else acc + term
        o_ref[a] = xg_ref[a] * (acc + sk * z_ref[a])


def _radix8_merge_gate(v, z, zcol0, xg, gcol0, skip_row, n_blocks, tr=256, ct=512):
    c = v.shape[-1]
    zv = z.reshape(n_blocks, FFT_N, z.shape[1])
    gv = xg.reshape(n_blocks, FFT_N, xg.shape[1])
    zb, gb = zcol0 // ct, gcol0 // ct
    out = pl.pallas_call(
        functools.partial(_radix8_merge_gate_kernel, n_blocks=n_blocks),
        grid=(FFT_N // tr, c // ct),
        in_specs=[
            pl.BlockSpec((N_ALPHA, 2, tr, ct), lambda i, cc: (0, 0, i, cc)),
            pl.BlockSpec((n_blocks, tr, ct), lambda i, cc: (0, i, zb + cc)),
            pl.BlockSpec((n_blocks, tr, ct), lambda i, cc: (0, i, gb + cc)),
            pl.BlockSpec((1, ct), lambda i, cc: (0, cc)),
        ],
        out_specs=pl.BlockSpec((n_blocks, tr, ct), lambda i, cc: (0, i, cc)),
        out_shape=jax.ShapeDtypeStruct((n_blocks, FFT_N, c), F32),
        compiler_params=_cparams(("parallel", "parallel")),
        name="radix8_merge_gate",
    )(v, zv, gv, skip_row)
    return out.reshape(n_blocks * FFT_N, c)


CONV_ROWS = 256


def _conv3_chunk(u_ref, p, i, w, b, seq):
    n_chunks = seq // CONV_ROWS
    r0 = pl.multiple_of(i * CONV_ROWS, CONV_ROWS)
    cur = u_ref[p, pl.ds(r0, CONV_ROWS), :].astype(F32)
    pr = pl.multiple_of(jnp.maximum(r0 - HALO, 0), HALO)
    nr = pl.multiple_of(jnp.minimum(r0 + CONV_ROWS, seq - HALO), HALO)
    prev_row = jnp.where(i == 0, 0.0, u_ref[p, pl.ds(pr, HALO), :][HALO - 1:HALO, :].astype(F32))
    next_row = jnp.where(i == n_chunks - 1, 0.0, u_ref[p, pl.ds(nr, HALO), :][0:1, :].astype(F32))
    row = lax.broadcasted_iota(jnp.int32, cur.shape, 0)
    up = jnp.where(row == 0, prev_row, pltpu.roll(cur, 1, 0))
    dn = jnp.where(row == CONV_ROWS - 1, next_row, pltpu.roll(cur, CONV_ROWS - 1, 0))
    return up * w[0:1, :] + cur * w[1:2, :] + dn * w[2:3, :] + b


def _hyena_pairs_kernel(z_ref, x1_ref, x2_ref, wz_ref, bz_ref, w1_ref, b1_ref, w2_ref, b2_ref, sk_ref,
                        kf_ref, g_ref, gi_ref, ki_ref, h0_ref, h1_ref, o_ref, zin_ref, s1_ref, s2_ref,
                        *, seq):
    n1 = seq // FFT_N2
    n_chunks = seq // CONV_ROWS

    def each_chunk(fn):
        for p in range(2):
            def body(i, carry, p=p):
                fn(p, i, pl.ds(pl.multiple_of(i * CONV_ROWS, CONV_ROWS), CONV_ROWS))
                return carry
            lax.fori_loop(0, n_chunks, body, 0)

    def load_z(p, i, rows):
        zin_ref[p, rows, :] = _conv3_chunk(z_ref, p, i, wz_ref[...], bz_ref[...], seq)
    each_chunk(load_z)

    _fft_core_kernel(zin_ref, kf_ref, g_ref, gi_ref, ki_ref, h0_ref, o_ref, s1_ref, s2_ref, n1_in=n1, n1_out=n1)

    def gate1(p, i, rows):
        xg = _conv3_chunk(x1_ref, p, i, w1_ref[...], b1_ref[...], seq)
        zin_ref[p, rows, :] = xg * (o_ref[p, rows, :] + sk_ref[0:1, :] * zin_ref[p, rows, :])
    each_chunk(gate1)

    _fft_core_kernel(zin_ref, kf_ref, g_ref, gi_ref, ki_ref, h1_ref, o_ref, s1_ref, s2_ref, n1_in=n1, n1_out=n1)

    def gate2(p, i, rows):
        xg = _conv3_chunk(x2_ref, p, i, w2_ref[...], b2_ref[...], seq)
        o_ref[p, rows, :] = xg * (o_ref[p, rows, :] + sk_ref[1:2, :] * zin_ref[p, rows, :])
    each_chunk(gate2)


def _hyena_pairs(hyg3, w_short, b_short, spec, filt_skip, mats, ct=256):
    kf, g, ginv, ki = mats
    bsz, seq, _ = hyg3.shape
    c = HYENA_WIDTH
    nc = c // ct
    hy = lambda k: pl.BlockSpec((2, seq, ct), lambda cc, p: (p, 0, k * nc + cc))
    wsp = lambda k: pl.BlockSpec((3, ct), lambda cc, p: (0, k * nc + cc))
    bsp = lambda k: pl.BlockSpec((1, ct), lambda cc, p: (0, k * nc + cc))
    return pl.pallas_call(
        functools.partial(_hyena_pairs_kernel, seq=seq),
        grid=(nc, bsz // 2),
        in_specs=[
            hy(0), hy(1), hy(2), wsp(0), bsp(0), wsp(1), bsp(1), wsp(2), bsp(2),
            pl.BlockSpec((HYENA_ORDER, ct), lambda cc, p: (0, cc)),
            _const_spec((None,) + kf.shape[1:], lambda cc, p: (0, 0, 0, 0)),
            _const_spec(g.shape, lambda cc, p: (0, 0)),
            _const_spec(ginv.shape, lambda cc, p: (0, 0)),
            _const_spec((None,) + ki.shape[1:], lambda cc, p: (0, 0, 0, 0)),
            _const_spec((SPEC_ROWS, ct), lambda cc, p: (0, cc)),
            _const_spec((SPEC_ROWS, ct), lambda cc, p: (0, nc + cc)),
        ],
        out_specs=pl.BlockSpec((2, seq, ct), lambda cc, p: (p, 0, cc)),
        out_shape=jax.ShapeDtypeStruct((bsz, seq, c), F32),
        scratch_shapes=[pltpu.VMEM((2, seq, ct), F32), pltpu.VMEM((SPEC_ROWS, ct), F32),
                        pltpu.VMEM((SPEC_ROWS, ct), F32)],
        compiler_params=pltpu.CompilerParams(dimension_semantics=("arbitrary", "arbitrary"),
                                             vmem_limit_bytes=FUSED_VMEM_LIMIT),
        name="hyena_pairs",
    )(hyg3, hyg3, hyg3, w_short, b_short, w_short, b_short, w_short, b_short, filt_skip,
      kf, g, ginv, ki, spec, spec)


def _hyena(hyg3, w_short, b_short, filt, filt_skip):
    bsz, seq, _ = hyg3.shape
    c = HYENA_WIDTH
    taps = _filter_taps(seq, *filt)
    if 2 * seq == FFT_N:
        mats = _dft_matrices(seq // FFT_N2, seq // FFT_N2, 1)
        mats_full = _dft_matrices(FFT_N1, FFT_N1, 1)
        u = jnp.stack([taps, jnp.zeros_like(taps)], axis=0)[None]
        spec = _fft_forward(u, mats_full[0], mats_full[1], 256)[0]
        zh = _hyena_pairs(hyg3, w_short, b_short, spec, filt_skip, mats)
        return zh.reshape(bsz * seq, c)
    hc = _short_conv(hyg3, w_short, b_short)
    hc2 = hc.reshape(bsz * seq, 3 * c)
    assert bsz == 1 and 2 * seq == RADIX8 * FFT_N
    n_blocks = seq // FFT_N
    mats = _dft_matrices(FFT_N1, FFT_N1, N_ALPHA)
    tap_res = _radix8_split(taps, 0, HYENA_ORDER * c, RADIX8)
    spec = _fft_forward(tap_res, mats[0], mats[1], 256)
    z = hc2
    zcol = 0
    for n in range(HYENA_ORDER):
        u = _radix8_split(z, zcol, c, n_blocks)
        v = _fft_conv_residues(u, spec, n * c, mats)
        z = _radix8_merge_gate(v, z, zcol, hc2, (n + 1) * c, filt_skip[n][None, :], n_blocks)
        zcol = 0
    return z


def _merge_kernel(o0_ref, o1_ref, o2_ref, l0_ref, l1_ref, l2_ref, zh_ref, ga_ref, gh_ref, x_ref,
                  wa_ref, wh_ref, wo_ref, out_ref):
    l0, l1, l2 = l0_ref[...], l1_ref[...], l2_ref[...]
    m = jnp.maximum(jnp.maximum(l0, l1), l2)
    e0, e1, e2 = jnp.exp(l0 - m), jnp.exp(l1 - m), jnp.exp(l2 - m)
    tot = e0 + e1 + e2
    w0, w1, w2 = e0 / tot, e1 / tot, e2 / tot
    heads = []
    for h in range(HEADS_PER_GROUP):
        heads.append(o0_ref[h] * w0[:, h:h + 1] + o1_ref[h] * w1[:, h:h + 1] + o2_ref[h] * w2[:, h:h + 1])
    attn = jnp.concatenate(heads, axis=1).astype(BF16)
    ab = jnp.dot(attn, wa_ref[...], preferred_element_type=F32)
    hb = jnp.dot(zh_ref[...].astype(BF16), wh_ref[...], preferred_element_type=F32)
    merged = (jax.nn.sigmoid(ga_ref[...].astype(F32)) * ab
              + jax.nn.sigmoid(gh_ref[...].astype(F32)) * hb)
    out_ref[...] = x_ref[...] + jnp.dot(merged.astype(BF16), wo_ref[...], preferred_element_type=F32)


def _merge(outs, lses, zh, hyg, x2d, wa, wh, wo, tm=512):
    t, d = x2d.shape
    seq = outs[0].shape[2]
    per_seq = seq // tm
    tok = lambda w: pl.BlockSpec((tm, w), lambda i: (i, 0))
    full = lambda a: pl.BlockSpec(a.shape, lambda i: (0, 0))
    heads = pl.BlockSpec((None, HEADS_PER_GROUP, tm, HEAD_DIM),
                         lambda i: (i // per_seq, 0, i % per_seq, 0))
    gcol = 3 * HYENA_WIDTH // d
    return pl.pallas_call(
        _merge_kernel,
        grid=(t // tm,),
        in_specs=[heads] * 3 + [tok(HEAD_DIM)] * 3 + [
            tok(HYENA_WIDTH),
            pl.BlockSpec((tm, d), lambda i: (i, gcol)),
            pl.BlockSpec((tm, d), lambda i: (i, gcol + 1)),
            tok(d), full(wa), full(wh), full(wo)],
        out_specs=tok(d),
        out_shape=jax.ShapeDtypeStruct((t, d), F32),
        compiler_params=_cparams(("parallel",)),
        name="merge_out_proj",
    )(*outs, *lses, zh, hyg, hyg, x2d, wa, wh, wo)


def _mlp_kernel(x_ref, g_ref, w1_ref, w2_ref, o_ref, xn_ref, acc_ref):
    j = pl.program_id(1)

    @pl.when(j == 0)
    def _():
        x = x_ref[...]
        ms = jnp.mean(x * x, axis=-1, keepdims=True)
        xn_ref[...] = (x * lax.rsqrt(ms + NORM_EPS) * g_ref[...]).astype(BF16)
        acc_ref[...] = jnp.zeros_like(acc_ref)

    h = jnp.dot(xn_ref[...], w1_ref[...], preferred_element_type=F32)
    a = jnp.square(jnp.maximum(h, 0.0)).astype(BF16)
    acc_ref[...] += jnp.dot(a, w2_ref[...], preferred_element_type=F32)

    @pl.when(j == pl.num_programs(1) - 1)
    def _():
        o_ref[...] = x_ref[...] + acc_ref[...]


def _mlp(x2d, g, w1, w2, tm=2048, tf=512):
    t, d = x2d.shape
    f = w1.shape[1]
    return pl.pallas_call(
        _mlp_kernel,
        grid=(t // tm, f // tf),
        in_specs=[
            pl.BlockSpec((tm, d), lambda i, j: (i, 0)),
            pl.BlockSpec((1, d), lambda i, j: (0, 0)),
            pl.BlockSpec((d, tf), lambda i, j: (0, j)),
            pl.BlockSpec((tf, d), lambda i, j: (j, 0)),
        ],
        out_specs=pl.BlockSpec((tm, d), lambda i, j: (i, 0)),
        out_shape=jax.ShapeDtypeStruct((t, d), F32),
        scratch_shapes=[pltpu.VMEM((tm, d), BF16), pltpu.VMEM((tm, d), F32)],
        compiler_params=_cparams(("parallel", "arbitrary")),
        name="mlp",
    )(x2d, g, w1, w2)


def _encoder_layer(x, p):
    bsz, seq, d = x.shape
    x2d = x.reshape(bsz * seq, d)
    hyg = _norm_matmul(x2d, p['g_mix'], p['w_hyg'], BF16)
    outs, lses = [], []
    for gi, (_, dil) in enumerate(DILATION_GROUPS):
        qkv_g = _qkv_group(x2d, p['g_mix'], p['w_qkv'][gi], p['head_gain'], dil)
        o, s = _attn_group(qkv_g, p['rel_bias'], gi, bsz, seq)
        outs.append(o)
        lses.append(s)
    zh = _hyena(hyg.reshape(bsz, seq, HYG_WIDTH), p['w_short'], p['b_short'], p['filt'], p['filt_skip'])
    x1 = _merge(outs, lses, zh, hyg, x2d, p['wa'], p['wh'], p['wo'])
    y = _mlp(x1, p['g_mlp'], p['w1'], p['w2'])
    return y.reshape(bsz, seq, d)


def kernel(x_prompt, x_sample, rel_bias, g_mix, w_in, g_q, g_k, w_attn_branch, w_short, b_short,
           filt_w1, filt_b1, filt_w2, filt_b2, filt_w3, filt_b3, filt_w4, filt_freq, filt_skip,
           w_hyena_branch, w_out, g_mlp, w_ff1, w_ff2):
    y_prompt, y_sample = x_prompt, x_sample
    for l in range(g_mix.shape[0]):
        w_in_b = w_in[l].astype(BF16)
        w_qkv = [jnp.concatenate([w_in_b[:, s * ATTN_WIDTH + gi * GROUP_WIDTH:s * ATTN_WIDTH + (gi + 1) * GROUP_WIDTH]
                                  for s in range(3)], axis=1) for gi in range(N_GROUPS)]
        head_gain = jnp.concatenate([
            jnp.tile(g_q[l].astype(F32) * (HEAD_DIM ** -0.5), HEADS_PER_GROUP),
            jnp.tile(g_k[l].astype(F32), HEADS_PER_GROUP),
            jnp.ones((GROUP_WIDTH,), F32)])[None, :]
        p = dict(
            g_mix=g_mix[l][None, :].astype(F32),
            w_qkv=w_qkv, w_hyg=w_in_b[:, QKV_WIDTH:],
            head_gain=head_gain,
            rel_bias=rel_bias,
            w_short=w_short[l], b_short=b_short[l][None, :],
            filt=(filt_w1[l], filt_b1[l], filt_w2[l], filt_b2[l], filt_w3[l], filt_b3[l], filt_w4[l],
                  filt_freq[l]),
            filt_skip=filt_skip[l],
            wa=w_attn_branch[l].astype(BF16), wh=w_hyena_branch[l].astype(BF16),
            wo=w_out[l].astype(BF16),
            g_mlp=g_mlp[l][None, :].astype(F32), w1=w_ff1[l].astype(BF16), w2=w_ff2[l].astype(BF16))
        y_prompt = _encoder_layer(y_prompt, p)
        y_sample = _encoder_layer(y_sample, p)
    return (y_prompt, y_sample)
```

```python
import functools
import math

import numpy as np
import jax
import jax.numpy as jnp
from jax import lax
from jax.experimental import pallas as pl
from jax.experimental.pallas import tpu as pltpu

F32 = jnp.float32
BF16 = jnp.bfloat16

D_MODEL = 1024
HEAD_DIM = 128
HEADS_PER_GROUP = 4
DILATION_GROUPS = ((128, 1), (512, 4), (2048, 16))
N_GROUPS = len(DILATION_GROUPS)
ATTN_WIDTH = N_GROUPS * HEADS_PER_GROUP * HEAD_DIM
GROUP_WIDTH = HEADS_PER_GROUP * HEAD_DIM
GROUP_QKV = 3 * GROUP_WIDTH
BAND = 64
N_BUCKETS = 32
MAX_DISTANCE = 1024
HYENA_WIDTH = D_MODEL
HYENA_ORDER = 2
FILTER_BANDS = 16
FILTER_HIDDEN = 64
FILTER_OUT_SCALE = 0.02
DECAY_TARGET = 1e-2
FAST_DECAY_PCT = 0.3
SLOW_DECAY_PCT = 1.5
D_FF = 4 * D_MODEL
QKV_WIDTH = 3 * ATTN_WIDTH
HYG_WIDTH = 3 * HYENA_WIDTH + 2 * D_MODEL
NORM_EPS = 1e-6
MASK_VALUE = -1e30

FFT_N = 4096
FFT_N1 = 32
FFT_N2 = 128
SUB = 8
LANES = 128
HALO = 16
N_CHUNK = FFT_N2 // SUB
SPEC_ROWS = 2 * FFT_N
RADIX8 = 8
N_ALPHA = 5

VMEM_LIMIT = 56 * 1024 * 1024
FUSED_VMEM_LIMIT = 60 * 1024 * 1024


def _cparams(sem):
    return pltpu.CompilerParams(dimension_semantics=sem, vmem_limit_bytes=VMEM_LIMIT)


def _norm_matmul_kernel(x_ref, g_ref, w_ref, o_ref, xn_ref):
    @pl.when(pl.program_id(1) == 0)
    def _():
        x = x_ref[...]
        ms = jnp.mean(x * x, axis=-1, keepdims=True)
        xn_ref[...] = (x * lax.rsqrt(ms + NORM_EPS) * g_ref[...]).astype(BF16)

    o_ref[...] = jnp.dot(xn_ref[...], w_ref[...], preferred_element_type=F32).astype(o_ref.dtype)


def _norm_matmul(x2d, g, w_bf16, out_dtype, tm=2048, tn=1280):
    t, d = x2d.shape
    n = w_bf16.shape[1]
    return pl.pallas_call(
        _norm_matmul_kernel,
        grid=(t // tm, n // tn),
        in_specs=[
            pl.BlockSpec((tm, d), lambda i, j: (i, 0)),
            pl.BlockSpec((1, d), lambda i, j: (0, 0)),
            pl.BlockSpec((d, tn), lambda i, j: (0, j)),
        ],
        out_specs=pl.BlockSpec((tm, tn), lambda i, j: (i, j)),
        out_shape=jax.ShapeDtypeStruct((t, n), out_dtype),
        scratch_shapes=[pltpu.VMEM((tm, d), BF16)],
        compiler_params=_cparams(("parallel", "arbitrary")),
        name="norm_matmul",
    )(x2d, g, w_bf16)


def _qkv_group_kernel(x_ref, g_ref, w_ref, hg_ref, o_ref, *scratch, tm, dil):
    rows = tm // dil
    x = x_ref[...]
    ms = jnp.mean(x * x, axis=-1, keepdims=True)
    xn = x * lax.rsqrt(ms + NORM_EPS) * g_ref[...]
    if dil == 1:
        xn = xn.astype(BF16)
    else:
        s_ref, = scratch
        n_slab = xn.shape[1] // LANES
        for c in range(n_slab):
            s_ref[c * tm:(c + 1) * tm, :] = xn[:, c * LANES:(c + 1) * LANES]
        parts = []
        for r in range(dil):
            slabs = [s_ref[pl.ds(c * tm + r, rows, stride=dil), :] for c in range(n_slab)]
            parts.append(jnp.concatenate(slabs, axis=1).astype(BF16))
        xn = jnp.concatenate(parts, axis=0)
    hg = hg_ref[...]
    for jt in range(3):
        acc = jnp.dot(xn, w_ref[:, jt * GROUP_WIDTH:(jt + 1) * GROUP_WIDTH], preferred_element_type=F32)
        if jt < 2:
            heads = []
            for h in range(HEADS_PER_GROUP):
                a = acc[:, h * HEAD_DIM:(h + 1) * HEAD_DIM]
                ms = jnp.mean(a * a, axis=-1, keepdims=True)
                c0 = jt * GROUP_WIDTH + h * HEAD_DIM
                heads.append(a * lax.rsqrt(ms + NORM_EPS) * hg[:, c0:c0 + HEAD_DIM])
            acc = jnp.concatenate(heads, axis=1)
        acc = acc.astype(BF16)
        for r in range(dil):
            c0 = r * GROUP_QKV + jt * GROUP_WIDTH
            o_ref[:, c0:c0 + GROUP_WIDTH] = acc[r * rows:(r + 1) * rows, :]


def _qkv_group(x2d, g, w_g, head_gain_g, dil, tm=1024):
    t, d = x2d.shape
    return pl.pallas_call(
        functools.partial(_qkv_group_kernel, tm=tm, dil=dil),
        grid=(t // tm,),
        in_specs=[
            pl.BlockSpec((tm, d), lambda i: (i, 0)),
            pl.BlockSpec((1, d), lambda i: (0, 0)),
            pl.BlockSpec((d, GROUP_QKV), lambda i: (0, 0)),
            pl.BlockSpec((1, GROUP_QKV), lambda i: (0, 0)),
        ],
        out_specs=pl.BlockSpec((tm // dil, dil * GROUP_QKV), lambda i: (i, 0)),
        out_shape=jax.ShapeDtypeStruct((t // dil, dil * GROUP_QKV), BF16),
        scratch_shapes=[] if dil == 1 else [pltpu.VMEM((tm * d // LANES, LANES), F32)],
        compiler_params=_cparams(("parallel",)),
        name=f"qkv_group_d{dil}",
    )(x2d, g, w_g, head_gain_g)


def _t5_bucket_np(rel):
    nb = N_BUCKETS // 2
    max_exact = nb // 2
    side = np.where(rel > 0, nb, 0)
    n = np.abs(rel)
    nf = np.maximum(n, 1).astype(np.float32)
    large = max_exact + (np.log(nf / np.float32(max_exact)) / np.float32(math.log(MAX_DISTANCE / max_exact))
                         * np.float32(nb - max_exact)).astype(np.int32)
    large = np.minimum(large, nb - 1)
    return side + np.where(n < max_exact, n, large)


def _band_bucket_index(dil):
    qi = np.arange(2 * BAND)[:, None]
    kj = np.arange(4 * BAND)[None, :]
    delta = kj - BAND - qi
    idx = _t5_bucket_np(delta * dil).astype(np.int32)
    return np.where(np.abs(delta) <= BAND, idx, -1).astype(np.int32)


def _attn_kernel(idx_ref, tbl_ref, c_ref, p_ref, n_ref, o_ref, lse_ref, bias_ref, *, tq, m_len, dil):
    qb = 2 * BAND
    wb = 4 * BAND
    nsub = tq // qb
    first = (pl.program_id(0) == 0) & (pl.program_id(1) == 0)

    @pl.when(first)
    def _():
        idx = idx_ref[...]
        for h in range(HEADS_PER_GROUP):
            acc = jnp.full((qb, wb), MASK_VALUE, F32)
            for b in range(N_BUCKETS):
                acc = jnp.where(idx == b, tbl_ref[h, b], acc)
            bias_ref[h] = acc

    i = pl.program_id(1)
    col = lax.broadcasted_iota(jnp.int32, (qb, wb), 1)
    lane = lax.broadcasted_iota(jnp.int32, (qb, HEAD_DIM), 1)

    def window(s, cols):
        lo, hi = s * qb - BAND, s * qb + qb + BAND
        parts = []
        if lo < 0:
            parts.append(p_ref[0, :, cols])
            lo = 0
        parts.append(c_ref[0, lo:min(hi, tq), cols])
        if hi > tq:
            parts.append(n_ref[0, :, cols])
        return parts[0] if len(parts) == 1 else jnp.concatenate(parts, axis=0)

    for s in range(nsub):
        lo = BAND - i * tq - qb * s
        valid = (col >= lo) & (col < m_len + lo)
        for r in range(dil):
            base = r * GROUP_QKV
            lse_tile = jnp.zeros((qb, HEAD_DIM), F32)
            rows = pl.ds(s * qb * dil + r, qb, stride=dil) if dil > 1 else pl.ds(s * qb, qb)
            for h in range(HEADS_PER_GROUP):
                qs = c_ref[0, s * qb:(s + 1) * qb, base + h * HEAD_DIM:base + (h + 1) * HEAD_DIM]
                kk = window(s, slice(base + GROUP_WIDTH + h * HEAD_DIM, base + GROUP_WIDTH + (h + 1) * HEAD_DIM))
                sc = lax.dot_general(qs, kk, (((1,), (1,)), ((), ())), preferred_element_type=F32)
                sc = jnp.where(valid, sc + bias_ref[h], MASK_VALUE)
                mx = jnp.max(sc, axis=-1, keepdims=True)
                p = jnp.exp(sc - mx)
                den = jnp.sum(p, axis=-1, keepdims=True)
                vv = window(s, slice(base + 2 * GROUP_WIDTH + h * HEAD_DIM,
                                     base + 2 * GROUP_WIDTH + (h + 1) * HEAD_DIM))
                o = jnp.dot(p.astype(BF16), vv, preferred_element_type=F32) / den
                o_ref[0, h, rows, :] = o
                lse_tile = jnp.where(lane == h, mx + jnp.log(den), lse_tile)
            lse_ref[0, rows, :] = lse_tile


def _attn_group(qkv_g, rel_bias, gi, bsz, seq):
    _, dil = DILATION_GROUPS[gi]
    m_len = seq // dil
    tq = min(m_len, 512 if dil == 1 else 128)
    nblk = tq // BAND
    n_halo = m_len // BAND
    width = dil * GROUP_QKV
    view = qkv_g.reshape(bsz, m_len, width)
    idx = jnp.asarray(_band_bucket_index(dil))
    tbl = rel_bias[:, gi * HEADS_PER_GROUP:(gi + 1) * HEADS_PER_GROUP].T.astype(F32)
    o, lse = pl.pallas_call(
        functools.partial(_attn_kernel, tq=tq, m_len=m_len, dil=dil),
        grid=(bsz, m_len // tq),
        in_specs=[
            pl.BlockSpec((2 * BAND, 4 * BAND), lambda b, i: (0, 0)),
            pl.BlockSpec(memory_space=pltpu.SMEM),
            pl.BlockSpec((1, tq, width), lambda b, i: (b, i, 0)),
            pl.BlockSpec((1, BAND, width), lambda b, i: (b, jnp.maximum(i * nblk - 1, 0), 0)),
            pl.BlockSpec((1, BAND, width), lambda b, i: (b, jnp.minimum((i + 1) * nblk, n_halo - 1), 0)),
        ],
        out_specs=[
            pl.BlockSpec((1, HEADS_PER_GROUP, tq * dil, HEAD_DIM), lambda b, i: (b, 0, i, 0)),
            pl.BlockSpec((1, tq * dil, HEAD_DIM), lambda b, i: (b, i, 0)),
        ],
        out_shape=[
            jax.ShapeDtypeStruct((bsz, HEADS_PER_GROUP, seq, HEAD_DIM), F32),
            jax.ShapeDtypeStruct((bsz, seq, HEAD_DIM), F32),
        ],
        scratch_shapes=[pltpu.VMEM((HEADS_PER_GROUP, 2 * BAND, 4 * BAND), F32)],
        compiler_params=_cparams(("arbitrary", "arbitrary")),
        name=f"band_attn_g{gi}",
    )(idx, tbl, view, view, view)
    return o, lse.reshape(bsz * seq, HEAD_DIM)


def _short_conv_kernel(c_ref, p_ref, n_ref, w_ref, b_ref, o_ref, *, tl):
    i = pl.program_id(1)
    last = pl.num_programs(1) - 1
    cur = c_ref[0].astype(F32)
    prev_row = jnp.where(i == 0, 0.0, p_ref[0, HALO - 1:HALO, :].astype(F32))
    next_row = jnp.where(i == last, 0.0, n_ref[0, 0:1, :].astype(F32))
    row = lax.broadcasted_iota(jnp.int32, cur.shape, 0)
    up = jnp.where(row == 0, prev_row, pltpu.roll(cur, 1, 0))
    dn = jnp.where(row == tl - 1, next_row, pltpu.roll(cur, tl - 1, 0))
    w = w_ref[...]
    o_ref[0] = up * w[0:1, :] + cur * w[1:2, :] + dn * w[2:3, :] + b_ref[...]


def _short_conv(hyg3, w_short, b_short, tl=512, ct=512):
    bsz, seq, _ = hyg3.shape
    width = w_short.shape[1]
    nsub = tl // HALO
    return pl.pallas_call(
        functools.partial(_short_conv_kernel, tl=tl),
        grid=(bsz, seq // tl, width // ct),
        in_specs=[
            pl.BlockSpec((1, tl, ct), lambda b, i, c: (b, i, c)),
            pl.BlockSpec((1, HALO, ct), lambda b, i, c: (b, jnp.maximum(i * nsub - 1, 0), c)),
            pl.BlockSpec((1, HALO, ct), lambda b, i, c: (b, jnp.minimum((i + 1) * nsub, seq // HALO - 1), c)),
            pl.BlockSpec((3, ct), lambda b, i, c: (0, c)),
            pl.BlockSpec((1, ct), lambda b, i, c: (0, c)),
        ],
        out_specs=pl.BlockSpec((1, tl, ct), lambda b, i, c: (b, i, c)),
        out_shape=jax.ShapeDtypeStruct((bsz, seq, width), F32),
        compiler_params=_cparams(("parallel", "parallel", "parallel")),
        name="short_conv",
    )(hyg3, hyg3, hyg3, w_short, b_short)


def _filter_kernel(w1_ref, b1_ref, w2_ref, b2_ref, w3_ref, b3_ref, fr_ref, w4_ref,
                   ad_ref, o_ref, *, seq, tr):
    hp = lax.Precision.HIGHEST
    n = pl.program_id(0) * tr + lax.broadcasted_iota(jnp.int32, (tr, 128), 0)
    lag = jnp.where(n < seq, n, 2 * seq - n).astype(F32)
    t = lag / (seq - 1)
    ang = (2.0 * math.pi) * lag / seq
    f = lax.broadcasted_iota(jnp.int32, (tr, 128), 1)
    band_idx = jnp.where(f <= FILTER_BANDS, f - 1, f - 1 - FILTER_BANDS).astype(F32)
    bands = 1e-4 + band_idx * ((FILTER_BANDS - 1 - 1e-4) / (FILTER_BANDS - 1))
    arg = bands * ang
    z = jnp.where(f == 0, t,
                  jnp.where(f <= FILTER_BANDS, jnp.cos(arg),
                            jnp.where(f <= 2 * FILTER_BANDS, -jnp.sin(arg), 0.0)))
    fr = fr_ref[...]
    h = jnp.sin(fr * (jnp.dot(z, w1_ref[...], precision=hp, preferred_element_type=F32) + b1_ref[...]))
    h = jnp.sin(fr * (jnp.dot(h, w2_ref[...], precision=hp, preferred_element_type=F32) + b2_ref[...]))
    h = jnp.sin(fr * (jnp.dot(h, w3_ref[...], precision=hp, preferred_element_type=F32) + b3_ref[...]))
    h_hi = h.astype(BF16)
    h_lo = (h - h_hi.astype(F32)).astype(BF16)
    w_hi, w_lo = w4_ref[0], w4_ref[1]
    taps = (jnp.dot(h_hi, w_hi, preferred_element_type=F32) + jnp.dot(h_hi, w_lo, preferred_element_type=F32)
            + jnp.dot(h_lo, w_hi, preferred_element_type=F32))
    decay = jnp.exp(-t[:, 0:1] * ad_ref[...])
    o_ref[...] = jnp.where(n[:, 0:1] == seq, 0.0, taps * decay)


def _filter_taps(seq, w1, b1, w2, b2, w3, b3, w4, freq, tr=256):
    c = HYENA_WIDTH
    hid = FILTER_HIDDEN
    w1p = jnp.zeros((128, hid), F32).at[:w1.shape[0]].set(w1)
    w4r = w4.reshape(hid, HYENA_ORDER, 2, c)
    w4d = jnp.transpose(w4r, (2, 0, 1, 3)).reshape(2, hid, HYENA_ORDER * c)
    w4_hi = w4d.astype(BF16)
    w4_lo = (w4d - w4_hi.astype(F32)).astype(BF16)
    w4d = jnp.stack([w4_hi, w4_lo], axis=1)
    nblk = 2 * seq // tr
    min_decay = math.log(DECAY_TARGET) / SLOW_DECAY_PCT
    max_decay = math.log(DECAY_TARGET) / FAST_DECAY_PCT
    ad = jnp.abs(jnp.linspace(min_decay, max_decay, c, dtype=F32))
    ad = jnp.tile(ad, HYENA_ORDER)[None, :]
    row = lambda v: v.reshape(1, -1)
    full = lambda shape: pl.BlockSpec(shape, lambda i: (0,) * len(shape))
    return pl.pallas_call(
        functools.partial(_filter_kernel, seq=seq, tr=tr),
        grid=(nblk,),
        in_specs=[full((128, hid)), full((1, hid)), full((hid, hid)), full((1, hid)),
                  full((hid, hid)), full((1, hid)), full((1, hid)),
                  pl.BlockSpec((None, 2, hid, HYENA_ORDER * c), lambda i: (i // (nblk // 2), 0, 0, 0)),
                  full((1, HYENA_ORDER * c))],
        out_specs=pl.BlockSpec((tr, HYENA_ORDER * c), lambda i: (i, 0)),
        out_shape=jax.ShapeDtypeStruct((2 * seq, HYENA_ORDER * c), F32),
        compiler_params=_cparams(("parallel",)),
        name="hyena_filter_taps",
    )(w1p, row(b1), w2, row(b2), w3, row(b3), row(freq), w4d, ad)


@functools.lru_cache(maxsize=None)
def _dft_matrices(n1_in, n1_out, n_alpha):
    big = RADIX8 * FFT_N
    a = np.arange(n_alpha).reshape(-1, 1, 1, 1, 1)
    j = np.arange(N_CHUNK).reshape(1, -1, 1, 1, 1)
    k1 = np.arange(FFT_N1).reshape(1, 1, -1, 1, 1)
    t = np.arange(SUB).reshape(1, 1, 1, 1, -1)
    bf16 = jnp.dtype(BF16)

    def stage_a(n1_count, inverse):
        n1 = np.arange(n1_count).reshape(1, 1, 1, -1, 1)
        n2 = SUB * j + t
        e = (RADIX8 * FFT_N2 * n1 * k1 + RADIX8 * n2 * k1 + a * (FFT_N2 * n1 + n2)) % big
        ang = e * (2.0 * np.pi / big)
        cr = np.cos(ang)
        ci = np.sin(ang) if inverse else -np.sin(ang)
        if inverse:
            cr = cr / FFT_N
            ci = ci / FFT_N
        eye = np.eye(SUB)
        if not inverse:
            blk = lambda m: np.einsum('ajknt,ts->ajksnt', m, eye)
            top = np.concatenate([blk(cr), blk(-ci)], axis=4)
            bot = np.concatenate([blk(ci), blk(cr)], axis=4)
            m = np.stack([top, bot], axis=2)
            return m.reshape(n_alpha, N_CHUNK, 2 * FFT_N1 * SUB, 2 * n1_count * SUB).astype(bf16)
        blk = lambda m: np.einsum('ajknt,ts->ajnskt', m, eye)
        top = np.concatenate([blk(cr), blk(-ci)], axis=4)
        bot = np.concatenate([blk(ci), blk(cr)], axis=4)
        m = np.stack([top, bot], axis=2)
        return m.reshape(n_alpha, N_CHUNK, 2 * n1_count * SUB, 2 * FFT_N1 * SUB).astype(bf16)

    kf = stage_a(n1_in, False)
    ki = stage_a(n1_out, True)
    n2 = np.arange(FFT_N2)
    ang = ((n2[:, None] * n2[None, :]) % FFT_N2) * (2.0 * np.pi / FFT_N2)
    gr, gi = np.cos(ang), -np.sin(ang)
    g = np.block([[gr, -gi], [gi, gr]]).astype(bf16)
    ginv = np.block([[gr, gi], [-gi, gr]]).astype(bf16)
    return kf, g, ginv, ki


def _chunk_loop(body):
    for j in range(N_CHUNK):
        body(j, j * SUB)


def _fft_stage_a(z_ref, kf, s1_ref, n1_in):
    def body(j, off):
        chunks = [z_ref[p, pl.ds(FFT_N2 * n1 + off, SUB), :].astype(F32)
                  for p in range(2) for n1 in range(n1_in)]
        slab = jnp.concatenate(chunks, axis=0).astype(BF16)
        a = jnp.dot(kf[j], slab, preferred_element_type=F32)
        for q in range(2):
            for k1 in range(FFT_N1):
                r = (q * FFT_N1 + k1) * SUB
                s1_ref[pl.ds(k1 * 2 * FFT_N2 + q * FFT_N2 + off, SUB), :] = a[r:r + SUB, :]
    _chunk_loop(body)


def _fft_core_kernel(z_ref, kf_ref, g_ref, gi_ref, ki_ref, h_ref, o_ref, s1_ref, s2_ref, *, n1_in, n1_out):
    _fft_stage_a(z_ref, kf_ref, s1_ref, n1_in)

    def stage_b(k1, carry):
        r0 = pl.multiple_of(k1 * 2 * FFT_N2, 2 * FFT_N2)
        slab = s1_ref[pl.ds(r0, 2 * FFT_N2), :].astype(BF16)
        x = jnp.dot(g_ref[...], slab, preferred_element_type=F32)
        xr, xi = x[:FFT_N2], x[FFT_N2:]
        hr = h_ref[pl.ds(r0, FFT_N2), :]
        hi = h_ref[pl.ds(r0 + FFT_N2, FFT_N2), :]
        y = jnp.concatenate([xr * hr - xi * hi, xr * hi + xi * hr], axis=0).astype(BF16)
        s2_ref[pl.ds(r0, 2 * FFT_N2), :] = jnp.dot(gi_ref[...], y, preferred_element_type=F32)
        return carry
    lax.fori_loop(0, FFT_N1, stage_b, 0, unroll=8)

    def stage_c(j, off):
        chunks = [s2_ref[pl.ds(k1 * 2 * FFT_N2 + q * FFT_N2 + off, SUB), :]
                  for q in range(2) for k1 in range(FFT_N1)]
        slab = jnp.concatenate(chunks, axis=0).astype(BF16)
        y = jnp.dot(ki_ref[j], slab, preferred_element_type=F32)
        for p in range(2):
            for n1 in range(n1_out):
                r = (p * n1_out + n1) * SUB
                o_ref[p, pl.ds(FFT_N2 * n1 + off, SUB), :] = y[r:r + SUB, :]
    _chunk_loop(stage_c)


def _fft_core_full_kernel(z_ref, kf_ref, g_ref, gi_ref, ki_ref, h_ref, o_ref, s1_ref):
    half = FFT_N1 // 2
    _fft_stage_a(z_ref, kf_ref, s1_ref, FFT_N1)

    def stage_b(k1, carry):
        r0 = pl.multiple_of(k1 * 2 * FFT_N2, 2 * FFT_N2)
        slab = s1_ref[pl.ds(r0, 2 * FFT_N2), :].astype(BF16)
        x = jnp.dot(g_ref[...], slab, preferred_element_type=F32)
        xr, xi = x[:FFT_N2], x[FFT_N2:]
        hr = h_ref[pl.ds(r0, FFT_N2), :]
        hi = h_ref[pl.ds(r0 + FFT_N2, FFT_N2), :]
        y = jnp.concatenate([xr * hr - xi * hi, xr * hi + xi * hr], axis=0).astype(BF16)
        ro = pl.multiple_of((k1 % half) * 2 * FFT_N2, 2 * FFT_N2)
        o_ref[k1 // half, pl.ds(ro, 2 * FFT_N2), :] = jnp.dot(gi_ref[...], y, preferred_element_type=F32)
        return carry
    lax.fori_loop(0, FFT_N1, stage_b, 0, unroll=8)

    def stage_c(j, off):
        chunks = [o_ref[k1 // half, pl.ds((k1 % half) * 2 * FFT_N2 + q * FFT_N2 + off, SUB), :]
                  for q in range(2) for k1 in range(FFT_N1)]
        slab = jnp.concatenate(chunks, axis=0).astype(BF16)
        y = jnp.dot(ki_ref[j], slab, preferred_element_type=F32)
        for p in range(2):
            for n1 in range(FFT_N1):
                r = (p * FFT_N1 + n1) * SUB
                s1_ref[pl.ds(p * FFT_N + FFT_N2 * n1 + off, SUB), :] = y[r:r + SUB, :]
    _chunk_loop(stage_c)

    def copy_out(i, carry):
        r0 = pl.multiple_of(i * CONV_ROWS, CONV_ROWS)
        for p in range(2):
            o_ref[p, pl.ds(r0, CONV_ROWS), :] = s1_ref[pl.ds(p * FFT_N + r0, CONV_ROWS), :]
        return carry
    lax.fori_loop(0, FFT_N // CONV_ROWS, copy_out, 0)


def _fft_forward_kernel(z_ref, kf_ref, g_ref, o_ref, s1_ref, *, n1_in):
    _fft_stage_a(z_ref, kf_ref, s1_ref, n1_in)

    def stage_b(k1, carry):
        r0 = pl.multiple_of(k1 * 2 * FFT_N2, 2 * FFT_N2)
        slab = s1_ref[pl.ds(r0, 2 * FFT_N2), :].astype(BF16)
        o_ref[pl.ds(r0, 2 * FFT_N2), :] = jnp.dot(g_ref[...], slab, preferred_element_type=F32).astype(o_ref.dtype)
        return carry
    lax.fori_loop(0, FFT_N1, stage_b, 0, unroll=4)


def _const_spec(shape, index_map):
    return pl.BlockSpec(shape, index_map, pipeline_mode=pl.Buffered(1))


def _fft_conv_residues(u, spec, col0, mats, ct=256):
    kf, g, ginv, ki = mats
    c = u.shape[-1]
    cb0 = col0 // ct
    return pl.pallas_call(
        _fft_core_full_kernel,
        grid=(N_ALPHA, c // ct),
        in_specs=[
            _const_spec((None, 2, FFT_N, ct), lambda a, cc: (a, 0, 0, cc)),
            _const_spec((None,) + kf.shape[1:], lambda a, cc: (a, 0, 0, 0)),
            _const_spec(g.shape, lambda a, cc: (0, 0)),
            _const_spec(ginv.shape, lambda a, cc: (0, 0)),
            _const_spec((None,) + ki.shape[1:], lambda a, cc: (a, 0, 0, 0)),
            pl.BlockSpec((None, SPEC_ROWS, ct), lambda a, cc: (a, 0, cb0 + cc)),
        ],
        out_specs=pl.BlockSpec((None, 2, FFT_N, ct), lambda a, cc: (a, 0, 0, cc)),
        out_shape=jax.ShapeDtypeStruct(u.shape, F32),
        scratch_shapes=[pltpu.VMEM((SPEC_ROWS, ct), F32)],
        compiler_params=pltpu.CompilerParams(dimension_semantics=("arbitrary", "arbitrary"),
                                             vmem_limit_bytes=FUSED_VMEM_LIMIT),
        name="fft_conv_residues",
    )(u, kf, g, ginv, ki, spec)


def _fft_forward(u, kf, g, ct):
    n_a, _, _, c = u.shape
    return pl.pallas_call(
        functools.partial(_fft_forward_kernel, n1_in=FFT_N1),
        grid=(n_a, c // ct),
        in_specs=[
            pl.BlockSpec((None, 2, FFT_N, ct), lambda a, cc: (a, 0, 0, cc)),
            _const_spec((None,) + kf.shape[1:], lambda a, cc: (a, 0, 0, 0)),
            _const_spec(g.shape, lambda a, cc: (0, 0)),
        ],
        out_specs=pl.BlockSpec((None, SPEC_ROWS, ct), lambda a, cc: (a, 0, cc)),
        out_shape=jax.ShapeDtypeStruct((n_a, SPEC_ROWS, c), BF16),
        scratch_shapes=[pltpu.VMEM((SPEC_ROWS, ct), F32)],
        compiler_params=_cparams(("arbitrary", "arbitrary")),
        name="fft_forward",
    )(u, kf, g)


def _radix8_coefs(n_blocks):
    a = np.arange(n_blocks)[None, :]
    al = np.arange(N_ALPHA)[:, None]
    ang = 2.0 * np.pi * ((a * al) % RADIX8) / RADIX8
    return np.round(np.cos(ang), 12), np.round(-np.sin(ang), 12)


def _radix8_split_kernel(x_ref, o_ref, *, n_blocks):
    cr, ci = _radix8_coefs(n_blocks)
    xs = [x_ref[a] for a in range(n_blocks)]
    for al in range(N_ALPHA):
        for part, coef in ((0, cr), (1, ci)):
            acc = None
            for a in range(n_blocks):
                w = float(coef[al, a])
                if w == 0.0:
                    continue
                term = xs[a] if w == 1.0 else (-xs[a] if w == -1.0 else xs[a] * w)
                acc = term if acc is None else acc + term
            o_ref[al, part] = jnp.zeros_like(xs[0]) if acc is None else acc


def _radix8_split(x, col0, width, n_blocks, tr=256, ct=512):
    xv = x.reshape(n_blocks, FFT_N, x.shape[1])
    cb0 = col0 // ct
    return pl.pallas_call(
        functools.partial(_radix8_split_kernel, n_blocks=n_blocks),
        grid=(FFT_N // tr, width // ct),
        in_specs=[pl.BlockSpec((n_blocks, tr, ct), lambda i, c: (0, i, cb0 + c))],
        out_specs=pl.BlockSpec((N_ALPHA, 2, tr, ct), lambda i, c: (0, 0, i, c)),
        out_shape=jax.ShapeDtypeStruct((N_ALPHA, 2, FFT_N, width), F32),
        compiler_params=_cparams(("parallel", "parallel")),
        name="radix8_split",
    )(xv)


def _radix8_merge_gate_kernel(v_ref, z_ref, xg_ref, sk_ref, o_ref, *, n_blocks):
    sk = sk_ref[...]
    for a in range(n_blocks):
        acc = None
        for al in range(N_ALPHA):
            cw = (1.0 if al in (0, RADIX8 // 2) else 2.0) / RADIX8
            ang = 2.0 * np.pi * ((a * al) % RADIX8) / RADIX8
            wr = float(np.round(np.cos(ang), 12)) * cw
            wi = float(np.round(np.sin(ang), 12)) * cw
            for w, part in ((wr, 0), (-wi, 1)):
                if w == 0.0:
                    continue
                term = v_ref[al, part] * w
                acc = term if acc is None else acc + term
        o_ref[a] = xg_ref[a] * (acc + sk * z_ref[a])


def _radix8_merge_gate(v, z, zcol0, xg, gcol0, skip_row, n_blocks, tr=256, ct=512):
    c = v.shape[-1]
    zv = z.reshape(n_blocks, FFT_N, z.shape[1])
    gv = xg.reshape(n_blocks, FFT_N, xg.shape[1])
    zb, gb = zcol0 // ct, gcol0 // ct
    out = pl.pallas_call(
        functools.partial(_radix8_merge_gate_kernel, n_blocks=n_blocks),
        grid=(FFT_N // tr, c // ct),
        in_specs=[
            pl.BlockSpec((N_ALPHA, 2, tr, ct), lambda i, cc: (0, 0, i, cc)),
            pl.BlockSpec((n_blocks, tr, ct), lambda i, cc: (0, i, zb + cc)),
            pl.BlockSpec((n_blocks, tr, ct), lambda i, cc: (0, i, gb + cc)),
            pl.BlockSpec((1, ct), lambda i, cc: (0, cc)),
        ],
        out_specs=pl.BlockSpec((n_blocks, tr, ct), lambda i, cc: (0, i, cc)),
        out_shape=jax.ShapeDtypeStruct((n_blocks, FFT_N, c), F32),
        compiler_params=_cparams(("parallel", "parallel")),
        name="radix8_merge_gate",
    )(v, zv, gv, skip_row)
    return out.reshape(n_blocks * FFT_N, c)


CONV_ROWS = 256


def _conv3_chunk(u_ref, p, i, w, b, seq):
    n_chunks = seq // CONV_ROWS
    r0 = pl.multiple_of(i * CONV_ROWS, CONV_ROWS)
    cur = u_ref[p, pl.ds(r0, CONV_ROWS), :].astype(F32)
    pr = pl.multiple_of(jnp.maximum(r0 - HALO, 0), HALO)
    nr = pl.multiple_of(jnp.minimum(r0 + CONV_ROWS, seq - HALO), HALO)
    prev_row = jnp.where(i == 0, 0.0, u_ref[p, pl.ds(pr, HALO), :][HALO - 1:HALO, :].astype(F32))
    next_row = jnp.where(i == n_chunks - 1, 0.0, u_ref[p, pl.ds(nr, HALO), :][0:1, :].astype(F32))
    row = lax.broadcasted_iota(jnp.int32, cur.shape, 0)
    up = jnp.where(row == 0, prev_row, pltpu.roll(cur, 1, 0))
    dn = jnp.where(row == CONV_ROWS - 1, next_row, pltpu.roll(cur, CONV_ROWS - 1, 0))
    return up * w[0:1, :] + cur * w[1:2, :] + dn * w[2:3, :] + b


def _hyena_pairs_kernel(z_ref, x1_ref, x2_ref, wz_ref, bz_ref, w1_ref, b1_ref, w2_ref, b2_ref, sk_ref,
                        kf_ref, g_ref, gi_ref, ki_ref, h0_ref, h1_ref, o_ref, zin_ref, s1_ref, s2_ref,
                        *, seq):
    n1 = seq // FFT_N2
    n_chunks = seq // CONV_ROWS

    def each_chunk(fn):
        for p in range(2):
            def body(i, carry, p=p):
                fn(p, i, pl.ds(pl.multiple_of(i * CONV_ROWS, CONV_ROWS), CONV_ROWS))
                return carry
            lax.fori_loop(0, n_chunks, body, 0)

    def load_z(p, i, rows):
        zin_ref[p, rows, :] = _conv3_chunk(z_ref, p, i, wz_ref[...], bz_ref[...], seq)
    each_chunk(load_z)

    _fft_core_kernel(zin_ref, kf_ref, g_ref, gi_ref, ki_ref, h0_ref, o_ref, s1_ref, s2_ref, n1_in=n1, n1_out=n1)

    def gate1(p, i, rows):
        xg = _conv3_chunk(x1_ref, p, i, w1_ref[...], b1_ref[...], seq)
        zin_ref[p, rows, :] = xg * (o_ref[p, rows, :] + sk_ref[0:1, :] * zin_ref[p, rows, :])
    each_chunk(gate1)

    _fft_core_kernel(zin_ref, kf_ref, g_ref, gi_ref, ki_ref, h1_ref, o_ref, s1_ref, s2_ref, n1_in=n1, n1_out=n1)

    def gate2(p, i, rows):
        xg = _conv3_chunk(x2_ref, p, i, w2_ref[...], b2_ref[...], seq)
        o_ref[p, rows, :] = xg * (o_ref[p, rows, :] + sk_ref[1:2, :] * zin_ref[p, rows, :])
    each_chunk(gate2)


def _hyena_pairs(hyg3, w_short, b_short, spec, filt_skip, mats, ct=256):
    kf, g, ginv, ki = mats
    bsz, seq, _ = hyg3.shape
    c = HYENA_WIDTH
    nc = c // ct
    hy = lambda k: pl.BlockSpec((2, seq, ct), lambda cc, p: (p, 0, k * nc + cc))
    wsp = lambda k: pl.BlockSpec((3, ct), lambda cc, p: (0, k * nc + cc))
    bsp = lambda k: pl.BlockSpec((1, ct), lambda cc, p: (0, k * nc + cc))
    return pl.pallas_call(
        functools.partial(_hyena_pairs_kernel, seq=seq),
        grid=(nc, bsz // 2),
        in_specs=[
            hy(0), hy(1), hy(2), wsp(0), bsp(0), wsp(1), bsp(1), wsp(2), bsp(2),
            pl.BlockSpec((HYENA_ORDER, ct), lambda cc, p: (0, cc)),
            _const_spec((None,) + kf.shape[1:], lambda cc, p: (0, 0, 0, 0)),
            _const_spec(g.shape, lambda cc, p: (0, 0)),
            _const_spec(ginv.shape, lambda cc, p: (0, 0)),
            _const_spec((None,) + ki.shape[1:], lambda cc, p: (0, 0, 0, 0)),
            _const_spec((SPEC_ROWS, ct), lambda cc, p: (0, cc)),
            _const_spec((SPEC_ROWS, ct), lambda cc, p: (0, nc + cc)),
        ],
        out_specs=pl.BlockSpec((2, seq, ct), lambda cc, p: (p, 0, cc)),
        out_shape=jax.ShapeDtypeStruct((bsz, seq, c), F32),
        scratch_shapes=[pltpu.VMEM((2, seq, ct), F32), pltpu.VMEM((SPEC_ROWS, ct), F32),
                        pltpu.VMEM((SPEC_ROWS, ct), F32)],
        compiler_params=pltpu.CompilerParams(dimension_semantics=("arbitrary", "arbitrary"),
                                             vmem_limit_bytes=FUSED_VMEM_LIMIT),
        name="hyena_pairs",
    )(hyg3, hyg3, hyg3, w_short, b_short, w_short, b_short, w_short, b_short, filt_skip,
      kf, g, ginv, ki, spec, spec)


def _hyena(hyg3, w_short, b_short, filt, filt_skip):
    bsz, seq, _ = hyg3.shape
    c = HYENA_WIDTH
    taps = _filter_taps(seq, *filt)
    if 2 * seq == FFT_N:
        mats = _dft_matrices(seq // FFT_N2, seq // FFT_N2, 1)
        mats_full = _dft_matrices(FFT_N1, FFT_N1, 1)
        u = jnp.stack([taps, jnp.zeros_like(taps)], axis=0)[None]
        spec = _fft_forward(u, mats_full[0], mats_full[1], 256)[0]
        zh = _hyena_pairs(hyg3, w_short, b_short, spec, filt_skip, mats)
        return zh.reshape(bsz * seq, c)
    hc = _short_conv(hyg3, w_short, b_short)
    hc2 = hc.reshape(bsz * seq, 3 * c)
    assert bsz == 1 and 2 * seq == RADIX8 * FFT_N
    n_blocks = seq // FFT_N
    mats = _dft_matrices(FFT_N1, FFT_N1, N_ALPHA)
    tap_res = _radix8_split(taps, 0, HYENA_ORDER * c, RADIX8)
    spec = _fft_forward(tap_res, mats[0], mats[1], 256)
    z = hc2
    zcol = 0
    for n in range(HYENA_ORDER):
        u = _radix8_split(z, zcol, c, n_blocks)
        v = _fft_conv_residues(u, spec, n * c, mats)
        z = _radix8_merge_gate(v, z, zcol, hc2, (n + 1) * c, filt_skip[n][None, :], n_blocks)
        zcol = 0
    return z


def _merge_kernel(o0_ref, o1_ref, o2_ref, l0_ref, l1_ref, l2_ref, zh_ref, ga_ref, gh_ref, x_ref,
                  wa_ref, wh_ref, wo_ref, out_ref):
    l0, l1, l2 = l0_ref[...], l1_ref[...], l2_ref[...]
    m = jnp.maximum(jnp.maximum(l0, l1), l2)
    e0, e1, e2 = jnp.exp(l0 - m), jnp.exp(l1 - m), jnp.exp(l2 - m)
    tot = e0 + e1 + e2
    w0, w1, w2 = e0 / tot, e1 / tot, e2 / tot
    heads = []
    for h in range(HEADS_PER_GROUP):
        heads.append(o0_ref[h] * w0[:, h:h + 1] + o1_ref[h] * w1[:, h:h + 1] + o2_ref[h] * w2[:, h:h + 1])
    attn = jnp.concatenate(heads, axis=1).astype(BF16)
    ab = jnp.dot(attn, wa_ref[...], preferred_element_type=F32)
    hb = jnp.dot(zh_ref[...].astype(BF16), wh_ref[...], preferred_element_type=F32)
    merged = (jax.nn.sigmoid(ga_ref[...].astype(F32)) * ab
              + jax.nn.sigmoid(gh_ref[...].astype(F32)) * hb)
    out_ref[...] = x_ref[...] + jnp.dot(merged.astype(BF16), wo_ref[...], preferred_element_type=F32)


def _merge(outs, lses, zh, hyg, x2d, wa, wh, wo, tm=512):
    t, d = x2d.shape
    seq = outs[0].shape[2]
    per_seq = seq // tm
    tok = lambda w: pl.BlockSpec((tm, w), lambda i: (i, 0))
    full = lambda a: _const_spec(a.shape, lambda i: (0, 0))
    heads = pl.BlockSpec((None, HEADS_PER_GROUP, tm, HEAD_DIM),
                         lambda i: (i // per_seq, 0, i % per_seq, 0))
    gcol = 3 * HYENA_WIDTH // d
    return pl.pallas_call(
        _merge_kernel,
        grid=(t // tm,),
        in_specs=[heads] * 3 + [tok(HEAD_DIM)] * 3 + [
            tok(HYENA_WIDTH),
            pl.BlockSpec((tm, d), lambda i: (i, gcol)),
            pl.BlockSpec((tm, d), lambda i: (i, gcol + 1)),
            tok(d), full(wa), full(wh), full(wo)],
        out_specs=tok(d),
        out_shape=jax.ShapeDtypeStruct((t, d), F32),
        compiler_params=_cparams(("parallel",)),
        name="merge_out_proj",
    )(*outs, *lses, zh, hyg, hyg, x2d, wa, wh, wo)


def _mlp_kernel(x_ref, g_ref, w1_ref, w2_ref, o_ref, xn_ref, acc_ref):
    j = pl.program_id(1)

    @pl.when(j == 0)
    def _():
        x = x_ref[...]
        ms = jnp.mean(x * x, axis=-1, keepdims=True)
        xn_ref[...] = (x * lax.rsqrt(ms + NORM_EPS) * g_ref[...]).astype(BF16)
        acc_ref[...] = jnp.zeros_like(acc_ref)

    h = jnp.dot(xn_ref[...], w1_ref[...], preferred_element_type=F32)
    a = jnp.square(jnp.maximum(h, 0.0)).astype(BF16)
    acc_ref[...] += jnp.dot(a, w2_ref[...], preferred_element_type=F32)

    @pl.when(j == pl.num_programs(1) - 1)
    def _():
        o_ref[...] = x_ref[...] + acc_ref[...]


def _mlp(x2d, g, w1, w2, tm=2048, tf=512):
    t, d = x2d.shape
    f = w1.shape[1]
    return pl.pallas_call(
        _mlp_kernel,
        grid=(t // tm, f // tf),
        in_specs=[
            pl.BlockSpec((tm, d), lambda i, j: (i, 0)),
            pl.BlockSpec((1, d), lambda i, j: (0, 0)),
            pl.BlockSpec((d, tf), lambda i, j: (0, j)),
            pl.BlockSpec((tf, d), lambda i, j: (j, 0)),
        ],
        out_specs=pl.BlockSpec((tm, d), lambda i, j: (i, 0)),
        out_shape=jax.ShapeDtypeStruct((t, d), F32),
        scratch_shapes=[pltpu.VMEM((tm, d), BF16), pltpu.VMEM((tm, d), F32)],
        compiler_params=_cparams(("parallel", "arbitrary")),
        name="mlp",
    )(x2d, g, w1, w2)


def _encoder_layer(x, p):
    bsz, seq, d = x.shape
    x2d = x.reshape(bsz * seq, d)
    hyg = _norm_matmul(x2d, p['g_mix'], p['w_hyg'], BF16)
    outs, lses = [], []
    for gi, (_, dil) in enumerate(DILATION_GROUPS):
        qkv_g = _qkv_group(x2d, p['g_mix'], p['w_qkv'][gi], p['head_gain'], dil)
        o, s = _attn_group(qkv_g, p['rel_bias'], gi, bsz, seq)
        outs.append(o)
        lses.append(s)
    zh = _hyena(hyg.reshape(bsz, seq, HYG_WIDTH), p['w_short'], p['b_short'], p['filt'], p['filt_skip'])
    x1 = _merge(outs, lses, zh, hyg, x2d, p['wa'], p['wh'], p['wo'])
    y = _mlp(x1, p['g_mlp'], p['w1'], p['w2'])
    return y.reshape(bsz, seq, d)


def kernel(x_prompt, x_sample, rel_bias, g_mix, w_in, g_q, g_k, w_attn_branch, w_short, b_short,
           filt_w1, filt_b1, filt_w2, filt_b2, filt_w3, filt_b3, filt_w4, filt_freq, filt_skip,
           w_hyena_branch, w_out, g_mlp, w_ff1, w_ff2):
    y_prompt, y_sample = x_prompt, x_sample
    for l in range(g_mix.shape[0]):
        w_in_b = w_in[l].astype(BF16)
        w_qkv = [jnp.concatenate([w_in_b[:, s * ATTN_WIDTH + gi * GROUP_WIDTH:s * ATTN_WIDTH + (gi + 1) * GROUP_WIDTH]
                                  for s in range(3)], axis=1) for gi in range(N_GROUPS)]
        head_gain = jnp.concatenate([
            jnp.tile(g_q[l].astype(F32) * (HEAD_DIM ** -0.5), HEADS_PER_GROUP),
            jnp.tile(g_k[l].astype(F32), HEADS_PER_GROUP),
            jnp.ones((GROUP_WIDTH,), F32)])[None, :]
        p = dict(
            g_mix=g_mix[l][None, :].astype(F32),
            w_qkv=w_qkv, w_hyg=w_in_b[:, QKV_WIDTH:],
            head_gain=head_gain,
            rel_bias=rel_bias,
            w_short=w_short[l], b_short=b_short[l][None, :],
            filt=(filt_w1[l], filt_b1[l], filt_w2[l], filt_b2[l], filt_w3[l], filt_b3[l], filt_w4[l],
                  filt_freq[l]),
            filt_skip=filt_skip[l],
            wa=w_attn_branch[l].astype(BF16), wh=w_hyena_branch[l].astype(BF16),
            wo=w_out[l].astype(BF16),
            g_mlp=g_mlp[l][None, :].astype(F32), w1=w_ff1[l].astype(BF16), w2=w_ff2[l].astype(BF16))
        y_prompt = _encoder_layer(y_prompt, p)
        y_sample = _encoder_layer(y_sample, p)
    return (y_prompt, y_sample)
```

```python
import functools
import math

import numpy as np
import jax
import jax.numpy as jnp
from jax import lax
from jax.experimental import pallas as pl
from jax.experimental.pallas import tpu as pltpu

F32 = jnp.float32
BF16 = jnp.bfloat16

D_MODEL = 1024
HEAD_DIM = 128
HEADS_PER_GROUP = 4
DILATION_GROUPS = ((128, 1), (512, 4), (2048, 16))
N_GROUPS = len(DILATION_GROUPS)
ATTN_WIDTH = N_GROUPS * HEADS_PER_GROUP * HEAD_DIM
GROUP_WIDTH = HEADS_PER_GROUP * HEAD_DIM
GROUP_QKV = 3 * GROUP_WIDTH
BAND = 64
N_BUCKETS = 32
MAX_DISTANCE = 1024
HYENA_WIDTH = D_MODEL
HYENA_ORDER = 2
FILTER_BANDS = 16
FILTER_HIDDEN = 64
FILTER_OUT_SCALE = 0.02
DECAY_TARGET = 1e-2
FAST_DECAY_PCT = 0.3
SLOW_DECAY_PCT = 1.5
D_FF = 4 * D_MODEL
QKV_WIDTH = 3 * ATTN_WIDTH
HYG_WIDTH = 3 * HYENA_WIDTH + 2 * D_MODEL
NORM_EPS = 1e-6
MASK_VALUE = -1e30

FFT_N = 4096
FFT_N1 = 32
FFT_N2 = 128
SUB = 8
LANES = 128
HALO = 16
N_CHUNK = FFT_N2 // SUB
SPEC_ROWS = 2 * FFT_N
RADIX8 = 8
N_ALPHA = 5

VMEM_LIMIT = 56 * 1024 * 1024
FUSED_VMEM_LIMIT = 60 * 1024 * 1024


def _cparams(sem):
    return pltpu.CompilerParams(dimension_semantics=sem, vmem_limit_bytes=VMEM_LIMIT)


def _norm_matmul_kernel(x_ref, g_ref, w_ref, o_ref, xn_ref):
    @pl.when(pl.program_id(1) == 0)
    def _():
        x = x_ref[...]
        ms = jnp.mean(x * x, axis=-1, keepdims=True)
        xn_ref[...] = (x * lax.rsqrt(ms + NORM_EPS) * g_ref[...]).astype(BF16)

    o_ref[...] = jnp.dot(xn_ref[...], w_ref[...], preferred_element_type=F32).astype(o_ref.dtype)


def _norm_matmul(x2d, g, w_bf16, out_dtype, tm=2048, tn=1280):
    t, d = x2d.shape
    n = w_bf16.shape[1]
    return pl.pallas_call(
        _norm_matmul_kernel,
        grid=(t // tm, n // tn),
        in_specs=[
            pl.BlockSpec((tm, d), lambda i, j: (i, 0)),
            pl.BlockSpec((1, d), lambda i, j: (0, 0)),
            pl.BlockSpec((d, tn), lambda i, j: (0, j)),
        ],
        out_specs=pl.BlockSpec((tm, tn), lambda i, j: (i, j)),
        out_shape=jax.ShapeDtypeStruct((t, n), out_dtype),
        scratch_shapes=[pltpu.VMEM((tm, d), BF16)],
        compiler_params=_cparams(("parallel", "arbitrary")),
        name="norm_matmul",
    )(x2d, g, w_bf16)


def _qkv_group_kernel(x_ref, g_ref, w_ref, hg_ref, o_ref, *scratch, tm, dil):
    rows = tm // dil
    x = x_ref[...]
    ms = jnp.mean(x * x, axis=-1, keepdims=True)
    xn = x * lax.rsqrt(ms + NORM_EPS) * g_ref[...]
    if dil == 1:
        xn = xn.astype(BF16)
    else:
        s_ref, = scratch
        n_slab = xn.shape[1] // LANES
        for c in range(n_slab):
            s_ref[c * tm:(c + 1) * tm, :] = xn[:, c * LANES:(c + 1) * LANES]
        parts = []
        for r in range(dil):
            slabs = [s_ref[pl.ds(c * tm + r, rows, stride=dil), :] for c in range(n_slab)]
            parts.append(jnp.concatenate(slabs, axis=1).astype(BF16))
        xn = jnp.concatenate(parts, axis=0)
    hg = hg_ref[...]
    for jt in range(3):
        acc = jnp.dot(xn, w_ref[:, jt * GROUP_WIDTH:(jt + 1) * GROUP_WIDTH], preferred_element_type=F32)
        if jt < 2:
            heads = []
            for h in range(HEADS_PER_GROUP):
                a = acc[:, h * HEAD_DIM:(h + 1) * HEAD_DIM]
                ms = jnp.mean(a * a, axis=-1, keepdims=True)
                c0 = jt * GROUP_WIDTH + h * HEAD_DIM
                heads.append(a * lax.rsqrt(ms + NORM_EPS) * hg[:, c0:c0 + HEAD_DIM])
            acc = jnp.concatenate(heads, axis=1)
        acc = acc.astype(BF16)
        for r in range(dil):
            c0 = r * GROUP_QKV + jt * GROUP_WIDTH
            o_ref[:, c0:c0 + GROUP_WIDTH] = acc[r * rows:(r + 1) * rows, :]


def _qkv_group(x2d, g, w_g, head_gain_g, dil, tm=1024):
    t, d = x2d.shape
    return pl.pallas_call(
        functools.partial(_qkv_group_kernel, tm=tm, dil=dil),
        grid=(t // tm,),
        in_specs=[
            pl.BlockSpec((tm, d), lambda i: (i, 0)),
            pl.BlockSpec((1, d), lambda i: (0, 0)),
            pl.BlockSpec((d, GROUP_QKV), lambda i: (0, 0)),
            pl.BlockSpec((1, GROUP_QKV), lambda i: (0, 0)),
        ],
        out_specs=pl.BlockSpec((tm // dil, dil * GROUP_QKV), lambda i: (i, 0)),
        out_shape=jax.ShapeDtypeStruct((t // dil, dil * GROUP_QKV), BF16),
        scratch_shapes=[] if dil == 1 else [pltpu.VMEM((tm * d // LANES, LANES), F32)],
        compiler_params=_cparams(("parallel",)),
        name=f"qkv_group_d{dil}",
    )(x2d, g, w_g, head_gain_g)


def _t5_bucket_np(rel):
    nb = N_BUCKETS // 2
    max_exact = nb // 2
    side = np.where(rel > 0, nb, 0)
    n = np.abs(rel)
    nf = np.maximum(n, 1).astype(np.float32)
    large = max_exact + (np.log(nf / np.float32(max_exact)) / np.float32(math.log(MAX_DISTANCE / max_exact))
                         * np.float32(nb - max_exact)).astype(np.int32)
    large = np.minimum(large, nb - 1)
    return side + np.where(n < max_exact, n, large)


def _band_bucket_index(dil):
    qi = np.arange(2 * BAND)[:, None]
    kj = np.arange(4 * BAND)[None, :]
    delta = kj - BAND - qi
    idx = _t5_bucket_np(delta * dil).astype(np.int32)
    return np.where(np.abs(delta) <= BAND, idx, -1).astype(np.int32)


def _attn_kernel(idx_ref, tbl_ref, c_ref, p_ref, n_ref, o_ref, lse_ref, bias_ref, *, tq, m_len, dil):
    qb = 2 * BAND
    wb = 4 * BAND
    nsub = tq // qb
    first = (pl.program_id(0) == 0) & (pl.program_id(1) == 0)

    @pl.when(first)
    def _():
        idx = idx_ref[...]
        for h in range(HEADS_PER_GROUP):
            acc = jnp.full((qb, wb), MASK_VALUE, F32)
            for b in range(N_BUCKETS):
                acc = jnp.where(idx == b, tbl_ref[h, b], acc)
            bias_ref[h] = acc

    i = pl.program_id(1)
    col = lax.broadcasted_iota(jnp.int32, (qb, wb), 1)
    lane = lax.broadcasted_iota(jnp.int32, (qb, HEAD_DIM), 1)

    def window(s, cols):
        lo, hi = s * qb - BAND, s * qb + qb + BAND
        parts = []
        if lo < 0:
            parts.append(p_ref[0, :, cols])
            lo = 0
        parts.append(c_ref[0, lo:min(hi, tq), cols])
        if hi > tq:
            parts.append(n_ref[0, :, cols])
        return parts[0] if len(parts) == 1 else jnp.concatenate(parts, axis=0)

    for s in range(nsub):
        lo = BAND - i * tq - qb * s
        valid = (col >= lo) & (col < m_len + lo)
        for r in range(dil):
            base = r * GROUP_QKV
            lse_tile = jnp.zeros((qb, HEAD_DIM), F32)
            rows = pl.ds(s * qb * dil + r, qb, stride=dil) if dil > 1 else pl.ds(s * qb, qb)
            for h in range(HEADS_PER_GROUP):
                qs = c_ref[0, s * qb:(s + 1) * qb, base + h * HEAD_DIM:base + (h + 1) * HEAD_DIM]
                kk = window(s, slice(base + GROUP_WIDTH + h * HEAD_DIM, base + GROUP_WIDTH + (h + 1) * HEAD_DIM))
                sc = lax.dot_general(qs, kk, (((1,), (1,)), ((), ())), preferred_element_type=F32)
                sc = jnp.where(valid, sc + bias_ref[h], MASK_VALUE)
                mx = jnp.max(sc, axis=-1, keepdims=True)
                p = jnp.exp(sc - mx)
                den = jnp.sum(p, axis=-1, keepdims=True)
                vv = window(s, slice(base + 2 * GROUP_WIDTH + h * HEAD_DIM,
                                     base + 2 * GROUP_WIDTH + (h + 1) * HEAD_DIM))
                o = jnp.dot(p.astype(BF16), vv, preferred_element_type=F32) / den
                o_ref[0, h, rows, :] = o
                lse_tile = jnp.where(lane == h, mx + jnp.log(den), lse_tile)
            lse_ref[0, rows, :] = lse_tile


def _attn_group(qkv_g, rel_bias, gi, bsz, seq):
    _, dil = DILATION_GROUPS[gi]
    m_len = seq // dil
    tq = min(m_len, 512 if dil == 1 else 128)
    nblk = tq // BAND
    n_halo = m_len // BAND
    width = dil * GROUP_QKV
    view = qkv_g.reshape(bsz, m_len, width)
    idx = jnp.asarray(_band_bucket_index(dil))
    tbl = rel_bias[:, gi * HEADS_PER_GROUP:(gi + 1) * HEADS_PER_GROUP].T.astype(F32)
    o, lse = pl.pallas_call(
        functools.partial(_attn_kernel, tq=tq, m_len=m_len, dil=dil),
        grid=(bsz, m_len // tq),
        in_specs=[
            pl.BlockSpec((2 * BAND, 4 * BAND), lambda b, i: (0, 0)),
            pl.BlockSpec(memory_space=pltpu.SMEM),
            pl.BlockSpec((1, tq, width), lambda b, i: (b, i, 0)),
            pl.BlockSpec((1, BAND, width), lambda b, i: (b, jnp.maximum(i * nblk - 1, 0), 0)),
            pl.BlockSpec((1, BAND, width), lambda b, i: (b, jnp.minimum((i + 1) * nblk, n_halo - 1), 0)),
        ],
        out_specs=[
            pl.BlockSpec((1, HEADS_PER_GROUP, tq * dil, HEAD_DIM), lambda b, i: (b, 0, i, 0)),
            pl.BlockSpec((1, tq * dil, HEAD_DIM), lambda b, i: (b, i, 0)),
        ],
        out_shape=[
            jax.ShapeDtypeStruct((bsz, HEADS_PER_GROUP, seq, HEAD_DIM), F32),
            jax.ShapeDtypeStruct((bsz, seq, HEAD_DIM), F32),
        ],
        scratch_shapes=[pltpu.VMEM((HEADS_PER_GROUP, 2 * BAND, 4 * BAND), F32)],
        compiler_params=_cparams(("arbitrary", "arbitrary")),
        name=f"band_attn_g{gi}",
    )(idx, tbl, view, view, view)
    return o, lse.reshape(bsz * seq, HEAD_DIM)


def _short_conv_kernel(c_ref, p_ref, n_ref, w_ref, b_ref, o_ref, *, tl):
    i = pl.program_id(1)
    last = pl.num_programs(1) - 1
    cur = c_ref[0].astype(F32)
    prev_row = jnp.where(i == 0, 0.0, p_ref[0, HALO - 1:HALO, :].astype(F32))
    next_row = jnp.where(i == last, 0.0, n_ref[0, 0:1, :].astype(F32))
    row = lax.broadcasted_iota(jnp.int32, cur.shape, 0)
    up = jnp.where(row == 0, prev_row, pltpu.roll(cur, 1, 0))
    dn = jnp.where(row == tl - 1, next_row, pltpu.roll(cur, tl - 1, 0))
    w = w_ref[...]
    o_ref[0] = up * w[0:1, :] + cur * w[1:2, :] + dn * w[2:3, :] + b_ref[...]


def _short_conv(hyg3, w_short, b_short, tl=512, ct=512):
    bsz, seq, _ = hyg3.shape
    width = w_short.shape[1]
    nsub = tl // HALO
    return pl.pallas_call(
        functools.partial(_short_conv_kernel, tl=tl),
        grid=(bsz, seq // tl, width // ct),
        in_specs=[
            pl.BlockSpec((1, tl, ct), lambda b, i, c: (b, i, c)),
            pl.BlockSpec((1, HALO, ct), lambda b, i, c: (b, jnp.maximum(i * nsub - 1, 0), c)),
            pl.BlockSpec((1, HALO, ct), lambda b, i, c: (b, jnp.minimum((i + 1) * nsub, seq // HALO - 1), c)),
            pl.BlockSpec((3, ct), lambda b, i, c: (0, c)),
            pl.BlockSpec((1, ct), lambda b, i, c: (0, c)),
        ],
        out_specs=pl.BlockSpec((1, tl, ct), lambda b, i, c: (b, i, c)),
        out_shape=jax.ShapeDtypeStruct((bsz, seq, width), F32),
        compiler_params=_cparams(("parallel", "parallel", "parallel")),
        name="short_conv",
    )(hyg3, hyg3, hyg3, w_short, b_short)


FILTER_PACK = 4


def _filter_kernel(w1_ref, t1_ref, b1_ref, w2_ref, b2_ref, w3_ref, b3_ref, fr_ref, w4_ref,
                   ad_ref, o_ref, *, seq, tr):
    hp = lax.Precision.HIGHEST
    rows = tr // FILTER_PACK
    hid = FILTER_HIDDEN
    nfeat = 2 * FILTER_BANDS
    base = pl.program_id(0) * tr

    def lag_of(shape, lanes_per_group):
        r = lax.broadcasted_iota(jnp.int32, shape, 0)
        g = lax.broadcasted_iota(jnp.int32, shape, 1) // lanes_per_group
        n = base + g * rows + r
        return jnp.where(n < seq, n, 2 * seq - n).astype(F32)

    f = lax.broadcasted_iota(jnp.int32, (rows, FILTER_PACK * nfeat), 1) % nfeat
    band_idx = jnp.where(f < FILTER_BANDS, f, f - FILTER_BANDS).astype(F32)
    bands = 1e-4 + band_idx * ((FILTER_BANDS - 1 - 1e-4) / (FILTER_BANDS - 1))
    ang = (2.0 * math.pi) * lag_of((rows, FILTER_PACK * nfeat), nfeat) / seq
    z = jnp.cos(bands * ang + jnp.where(f < FILTER_BANDS, 0.0, 0.5 * math.pi))
    t = lag_of((rows, FILTER_PACK * hid), hid) / (seq - 1)
    fr = fr_ref[...]
    h = jnp.sin(fr * (jnp.dot(z, w1_ref[...], precision=hp, preferred_element_type=F32)
                      + t * t1_ref[...] + b1_ref[...]))
    h = jnp.sin(fr * (jnp.dot(h, w2_ref[...], precision=hp, preferred_element_type=F32) + b2_ref[...]))
    h = jnp.sin(fr * (jnp.dot(h, w3_ref[...], precision=hp, preferred_element_type=F32) + b3_ref[...]))
    w_hi, w_lo = w4_ref[0], w4_ref[1]
    for g in range(FILTER_PACK):
        hg = h[:, g * hid:(g + 1) * hid]
        h_hi = hg.astype(BF16)
        h_lo = (hg - h_hi.astype(F32)).astype(BF16)
        taps = (jnp.dot(h_hi, w_hi, preferred_element_type=F32) + jnp.dot(h_hi, w_lo, preferred_element_type=F32)
                + jnp.dot(h_lo, w_hi, preferred_element_type=F32))
        decay = jnp.exp(-t[:, g * hid:g * hid + 1] * ad_ref[...])
        n = base + g * rows + lax.broadcasted_iota(jnp.int32, (rows, 1), 0)
        o_ref[g * rows:(g + 1) * rows, :] = jnp.where(n == seq, 0.0, taps * decay)


def _filter_taps(seq, w1, b1, w2, b2, w3, b3, w4, freq, tr=1024):
    c = HYENA_WIDTH
    hid = FILTER_HIDDEN
    eye = jnp.eye(FILTER_PACK, dtype=F32)
    tile = lambda v: jnp.tile(v.reshape(1, -1), (1, FILTER_PACK))
    w1bd = jnp.kron(eye, w1[1:])
    w2bd, w3bd = jnp.kron(eye, w2), jnp.kron(eye, w3)
    w4r = w4.reshape(hid, HYENA_ORDER, 2, c)
    w4d = jnp.transpose(w4r, (2, 0, 1, 3)).reshape(2, hid, HYENA_ORDER * c)
    w4_hi = w4d.astype(BF16)
    w4_lo = (w4d - w4_hi.astype(F32)).astype(BF16)
    w4d = jnp.stack([w4_hi, w4_lo], axis=1)
    nblk = 2 * seq // tr
    min_decay = math.log(DECAY_TARGET) / SLOW_DECAY_PCT
    max_decay = math.log(DECAY_TARGET) / FAST_DECAY_PCT
    ad = jnp.abs(jnp.linspace(min_decay, max_decay, c, dtype=F32))
    ad = jnp.tile(ad, HYENA_ORDER)[None, :]
    full = lambda shape: pl.BlockSpec(shape, lambda i: (0,) * len(shape))
    ph = FILTER_PACK * hid
    return pl.pallas_call(
        functools.partial(_filter_kernel, seq=seq, tr=tr),
        grid=(nblk,),
        in_specs=[full(w1bd.shape), full((1, ph)), full((1, ph)), full((ph, ph)), full((1, ph)),
                  full((ph, ph)), full((1, ph)), full((1, ph)),
                  pl.BlockSpec((None, 2, hid, HYENA_ORDER * c), lambda i: (i // (nblk // 2), 0, 0, 0)),
                  full((1, HYENA_ORDER * c))],
        out_specs=pl.BlockSpec((tr, HYENA_ORDER * c), lambda i: (i, 0)),
        out_shape=jax.ShapeDtypeStruct((2 * seq, HYENA_ORDER * c), F32),
        compiler_params=_cparams(("parallel",)),
        name="hyena_filter_taps",
    )(w1bd, tile(w1[0]), tile(b1), w2bd, tile(b2), w3bd, tile(b3), tile(freq), w4d, ad)


@functools.lru_cache(maxsize=None)
def _dft_matrices(n1_in, n1_out, n_alpha):
    big = RADIX8 * FFT_N
    a = np.arange(n_alpha).reshape(-1, 1, 1, 1, 1)
    j = np.arange(N_CHUNK).reshape(1, -1, 1, 1, 1)
    k1 = np.arange(FFT_N1).reshape(1, 1, -1, 1, 1)
    t = np.arange(SUB).reshape(1, 1, 1, 1, -1)
    bf16 = jnp.dtype(BF16)

    def stage_a(n1_count, inverse):
        n1 = np.arange(n1_count).reshape(1, 1, 1, -1, 1)
        n2 = SUB * j + t
        e = (RADIX8 * FFT_N2 * n1 * k1 + RADIX8 * n2 * k1 + a * (FFT_N2 * n1 + n2)) % big
        ang = e * (2.0 * np.pi / big)
        cr = np.cos(ang)
        ci = np.sin(ang) if inverse else -np.sin(ang)
        if inverse:
            cr = cr / FFT_N
            ci = ci / FFT_N
        eye = np.eye(SUB)
        if not inverse:
            blk = lambda m: np.einsum('ajknt,ts->ajksnt', m, eye)
            top = np.concatenate([blk(cr), blk(-ci)], axis=4)
            bot = np.concatenate([blk(ci), blk(cr)], axis=4)
            m = np.stack([top, bot], axis=2)
            return m.reshape(n_alpha, N_CHUNK, 2 * FFT_N1 * SUB, 2 * n1_count * SUB).astype(bf16)
        blk = lambda m: np.einsum('ajknt,ts->ajnskt', m, eye)
        top = np.concatenate([blk(cr), blk(-ci)], axis=4)
        bot = np.concatenate([blk(ci), blk(cr)], axis=4)
        m = np.stack([top, bot], axis=2)
        return m.reshape(n_alpha, N_CHUNK, 2 * n1_count * SUB, 2 * FFT_N1 * SUB).astype(bf16)

    kf = stage_a(n1_in, False)
    ki = stage_a(n1_out, True)
    n2 = np.arange(FFT_N2)
    ang = ((n2[:, None] * n2[None, :]) % FFT_N2) * (2.0 * np.pi / FFT_N2)
    gr, gi = np.cos(ang), -np.sin(ang)
    g = np.block([[gr, -gi], [gi, gr]]).astype(bf16)
    ginv = np.block([[gr, gi], [-gi, gr]]).astype(bf16)
    return kf, g, ginv, ki


def _chunk_loop(body):
    for j in range(N_CHUNK):
        body(j, j * SUB)


def _fft_stage_a(z_ref, kf, s1_ref, n1_in):
    def body(j, off):
        chunks = [z_ref[p, pl.ds(FFT_N2 * n1 + off, SUB), :].astype(F32)
                  for p in range(2) for n1 in range(n1_in)]
        slab = jnp.concatenate(chunks, axis=0).astype(BF16)
        a = jnp.dot(kf[j], slab, preferred_element_type=F32)
        for q in range(2):
            for k1 in range(FFT_N1):
                r = (q * FFT_N1 + k1) * SUB
                s1_ref[pl.ds(k1 * 2 * FFT_N2 + q * FFT_N2 + off, SUB), :] = a[r:r + SUB, :]
    _chunk_loop(body)


def _fft_core_kernel(z_ref, kf_ref, g_ref, gi_ref, ki_ref, h_ref, o_ref, s1_ref, s2_ref, *, n1_in, n1_out):
    _fft_stage_a(z_ref, kf_ref, s1_ref, n1_in)

    for k1 in range(FFT_N1):
        r0 = k1 * 2 * FFT_N2
        slab = s1_ref[pl.ds(r0, 2 * FFT_N2), :].astype(BF16)
        x = jnp.dot(g_ref[...], slab, preferred_element_type=F32)
        xr, xi = x[:FFT_N2], x[FFT_N2:]
        hr = h_ref[pl.ds(r0, FFT_N2), :]
        hi = h_ref[pl.ds(r0 + FFT_N2, FFT_N2), :]
        y = jnp.concatenate([xr * hr - xi * hi, xr * hi + xi * hr], axis=0).astype(BF16)
        s2_ref[pl.ds(r0, 2 * FFT_N2), :] = jnp.dot(gi_ref[...], y, preferred_element_type=F32)

    def stage_c(j, off):
        chunks = [s2_ref[pl.ds(k1 * 2 * FFT_N2 + q * FFT_N2 + off, SUB), :]
                  for q in range(2) for k1 in range(FFT_N1)]
        slab = jnp.concatenate(chunks, axis=0).astype(BF16)
        y = jnp.dot(ki_ref[j], slab, preferred_element_type=F32)
        for p in range(2):
            for n1 in range(n1_out):
                r = (p * n1_out + n1) * SUB
                o_ref[p, pl.ds(FFT_N2 * n1 + off, SUB), :] = y[r:r + SUB, :]
    _chunk_loop(stage_c)


def _fft_core_full_kernel(z_ref, kf_ref, g_ref, gi_ref, ki_ref, h_ref, o_ref, s1_ref):
    half = FFT_N1 // 2
    _fft_stage_a(z_ref, kf_ref, s1_ref, FFT_N1)

    def stage_b(k1, carry):
        r0 = pl.multiple_of(k1 * 2 * FFT_N2, 2 * FFT_N2)
        slab = s1_ref[pl.ds(r0, 2 * FFT_N2), :].astype(BF16)
        x = jnp.dot(g_ref[...], slab, preferred_element_type=F32)
        xr, xi = x[:FFT_N2], x[FFT_N2:]
        hr = h_ref[pl.ds(r0, FFT_N2), :]
        hi = h_ref[pl.ds(r0 + FFT_N2, FFT_N2), :]
        y = jnp.concatenate([xr * hr - xi * hi, xr * hi + xi * hr], axis=0).astype(BF16)
        ro = pl.multiple_of((k1 % half) * 2 * FFT_N2, 2 * FFT_N2)
        o_ref[k1 // half, pl.ds(ro, 2 * FFT_N2), :] = jnp.dot(gi_ref[...], y, preferred_element_type=F32)
        return carry
    lax.fori_loop(0, FFT_N1, stage_b, 0, unroll=8)

    def stage_c(j, off):
        chunks = [o_ref[k1 // half, pl.ds((k1 % half) * 2 * FFT_N2 + q * FFT_N2 + off, SUB), :]
                  for q in range(2) for k1 in range(FFT_N1)]
        slab = jnp.concatenate(chunks, axis=0).astype(BF16)
        y = jnp.dot(ki_ref[j], slab, preferred_element_type=F32)
        for p in range(2):
            for n1 in range(FFT_N1):
                r = (p * FFT_N1 + n1) * SUB
                s1_ref[pl.ds(p * FFT_N + FFT_N2 * n1 + off, SUB), :] = y[r:r + SUB, :]
    _chunk_loop(stage_c)

    def copy_out(i, carry):
        r0 = pl.multiple_of(i * CONV_ROWS, CONV_ROWS)
        for p in range(2):
            o_ref[p, pl.ds(r0, CONV_ROWS), :] = s1_ref[pl.ds(p * FFT_N + r0, CONV_ROWS), :]
        return carry
    lax.fori_loop(0, FFT_N // CONV_ROWS, copy_out, 0)


def _fft_forward_kernel(z_ref, kf_ref, g_ref, o_ref, s1_ref, *, n1_in):
    _fft_stage_a(z_ref, kf_ref, s1_ref, n1_in)

    def stage_b(k1, carry):
        r0 = pl.multiple_of(k1 * 2 * FFT_N2, 2 * FFT_N2)
        slab = s1_ref[pl.ds(r0, 2 * FFT_N2), :].astype(BF16)
        o_ref[pl.ds(r0, 2 * FFT_N2), :] = jnp.dot(g_ref[...], slab, preferred_element_type=F32).astype(o_ref.dtype)
        return carry
    lax.fori_loop(0, FFT_N1, stage_b, 0, unroll=4)


def _const_spec(shape, index_map):
    return pl.BlockSpec(shape, index_map, pipeline_mode=pl.Buffered(1))


def _fft_conv_residues(u, spec, col0, mats, ct=256):
    kf, g, ginv, ki = mats
    c = u.shape[-1]
    cb0 = col0 // ct
    return pl.pallas_call(
        _fft_core_full_kernel,
        grid=(N_ALPHA, c // ct),
        in_specs=[
            _const_spec((None, 2, FFT_N, ct), lambda a, cc: (a, 0, 0, cc)),
            _const_spec((None,) + kf.shape[1:], lambda a, cc: (a, 0, 0, 0)),
            _const_spec(g.shape, lambda a, cc: (0, 0)),
            _const_spec(ginv.shape, lambda a, cc: (0, 0)),
            _const_spec((None,) + ki.shape[1:], lambda a, cc: (a, 0, 0, 0)),
            pl.BlockSpec((None, SPEC_ROWS, ct), lambda a, cc: (a, 0, cb0 + cc)),
        ],
        out_specs=pl.BlockSpec((None, 2, FFT_N, ct), lambda a, cc: (a, 0, 0, cc)),
        out_shape=jax.ShapeDtypeStruct(u.shape, F32),
        scratch_shapes=[pltpu.VMEM((SPEC_ROWS, ct), F32)],
        compiler_params=pltpu.CompilerParams(dimension_semantics=("arbitrary", "arbitrary"),
                                             vmem_limit_bytes=FUSED_VMEM_LIMIT),
        name="fft_conv_residues",
    )(u, kf, g, ginv, ki, spec)


def _fft_forward(u, kf, g, ct):
    n_a, _, _, c = u.shape
    return pl.pallas_call(
        functools.partial(_fft_forward_kernel, n1_in=FFT_N1),
        grid=(n_a, c // ct),
        in_specs=[
            pl.BlockSpec((None, 2, FFT_N, ct), lambda a, cc: (a, 0, 0, cc)),
            _const_spec((None,) + kf.shape[1:], lambda a, cc: (a, 0, 0, 0)),
            _const_spec(g.shape, lambda a, cc: (0, 0)),
        ],
        out_specs=pl.BlockSpec((None, SPEC_ROWS, ct), lambda a, cc: (a, 0, cc)),
        out_shape=jax.ShapeDtypeStruct((n_a, SPEC_ROWS, c), BF16),
        scratch_shapes=[pltpu.VMEM((SPEC_ROWS, ct), F32)],
        compiler_params=_cparams(("arbitrary", "arbitrary")),
        name="fft_forward",
    )(u, kf, g)


def _radix8_coefs(n_blocks):
    a = np.arange(n_blocks)[None, :]
    al = np.arange(N_ALPHA)[:, None]
    ang = 2.0 * np.pi * ((a * al) % RADIX8) / RADIX8
    return np.round(np.cos(ang), 12), np.round(-np.sin(ang), 12)


def _radix8_split_kernel(x_ref, o_ref, *, n_blocks):
    cr, ci = _radix8_coefs(n_blocks)
    xs = [x_ref[a] for a in range(n_blocks)]
    for al in range(N_ALPHA):
        for part, coef in ((0, cr), (1, ci)):
            acc = None
            for a in range(n_blocks):
                w = float(coef[al, a])
                if w == 0.0:
                    continue
                term = xs[a] if w == 1.0 else (-xs[a] if w == -1.0 else xs[a] * w)
                acc = term if acc is None else acc + term
            o_ref[al, part] = jnp.zeros_like(xs[0]) if acc is None else acc


def _radix8_split(x, col0, width, n_blocks, tr=256, ct=512):
    xv = x.reshape(n_blocks, FFT_N, x.shape[1])
    cb0 = col0 // ct
    return pl.pallas_call(
        functools.partial(_radix8_split_kernel, n_blocks=n_blocks),
        grid=(FFT_N // tr, width // ct),
        in_specs=[pl.BlockSpec((n_blocks, tr, ct), lambda i, c: (0, i, cb0 + c))],
        out_specs=pl.BlockSpec((N_ALPHA, 2, tr, ct), lambda i, c: (0, 0, i, c)),
        out_shape=jax.ShapeDtypeStruct((N_ALPHA, 2, FFT_N, width), F32),
        compiler_params=_cparams(("parallel", "parallel")),
        name="radix8_split",
    )(xv)


def _radix8_merge_gate_kernel(v_ref, z_ref, xg_ref, sk_ref, o_ref, *, n_blocks):
    sk = sk_ref[...]
    for a in range(n_blocks):
        acc = None
        for al in range(N_ALPHA):
            cw = (1.0 if al in (0, RADIX8 // 2) else 2.0) / RADIX8
            ang = 2.0 * np.pi * ((a * al) % RADIX8) / RADIX8
            wr = float(np.round(np.cos(ang), 12)) * cw
            wi = float(np.round(np.sin(ang), 12)) * cw
            for w, part in ((wr, 0), (-wi, 1)):
                if w == 0.0:
                    continue
                term = v_ref[al, part] * w
                acc = term if acc is None else acc + term
        o_ref[a] = xg_ref[a] * (acc + sk * z_ref[a])


def _radix8_merge_gate(v, z, zcol0, xg, gcol0, skip_row, n_blocks, tr=256, ct=512):
    c = v.shape[-1]
    zv = z.reshape(n_blocks, FFT_N, z.shape[1])
    gv = xg.reshape(n_blocks, FFT_N, xg.shape[1])
    zb, gb = zcol0 // ct, gcol0 // ct
    out = pl.pallas_call(
        functools.partial(_radix8_merge_gate_kernel, n_blocks=n_blocks),
        grid=(FFT_N // tr, c // ct),
        in_specs=[
            pl.BlockSpec((N_ALPHA, 2, tr, ct), lambda i, cc: (0, 0, i, cc)),
            pl.BlockSpec((n_blocks, tr, ct), lambda i, cc: (0, i, zb + cc)),
            pl.BlockSpec((n_blocks, tr, ct), lambda i, cc: (0, i, gb + cc)),
            pl.BlockSpec((1, ct), lambda i, cc: (0, cc)),
        ],
        out_specs=pl.BlockSpec((n_blocks, tr, ct), lambda i, cc: (0, i, cc)),
        out_shape=jax.ShapeDtypeStruct((n_blocks, FFT_N, c), F32),
        compiler_params=_cparams(("parallel", "parallel")),
        name="radix8_merge_gate",
    )(v, zv, gv, skip_row)
    return out.reshape(n_blocks * FFT_N, c)


CONV_ROWS = 256


def _conv3_chunk(u_ref, p, i, w, b, seq):
    n_chunks = seq // CONV_ROWS
    r0 = pl.multiple_of(i * CONV_ROWS, CONV_ROWS)
    cur = u_ref[p, pl.ds(r0, CONV_ROWS), :].astype(F32)
    pr = pl.multiple_of(jnp.maximum(r0 - HALO, 0), HALO)
    nr = pl.multiple_of(jnp.minimum(r0 + CONV_ROWS, seq - HALO), HALO)
    prev_row = jnp.where(i == 0, 0.0, u_ref[p, pl.ds(pr, HALO), :][HALO - 1:HALO, :].astype(F32))
    next_row = jnp.where(i == n_chunks - 1, 0.0, u_ref[p, pl.ds(nr, HALO), :][0:1, :].astype(F32))
    row = lax.broadcasted_iota(jnp.int32, cur.shape, 0)
    up = jnp.where(row == 0, prev_row, pltpu.roll(cur, 1, 0))
    dn = jnp.where(row == CONV_ROWS - 1, next_row, pltpu.roll(cur, CONV_ROWS - 1, 0))
    return up * w[0:1, :] + cur * w[1:2, :] + dn * w[2:3, :] + b


def _hyena_pairs_kernel(z_ref, x1_ref, x2_ref, wz_ref, bz_ref, w1_ref, b1_ref, w2_ref, b2_ref, sk_ref,
                        kf_ref, g_ref, gi_ref, ki_ref, h0_ref, h1_ref, o_ref, zin_ref, s1_ref, s2_ref,
                        *, seq):
    n1 = seq // FFT_N2
    n_chunks = seq // CONV_ROWS

    def each_chunk(fn):
        for p in range(2):
            def body(i, carry, p=p):
                fn(p, i, pl.ds(pl.multiple_of(i * CONV_ROWS, CONV_ROWS), CONV_ROWS))
                return carry
            lax.fori_loop(0, n_chunks, body, 0)

    def load_z(p, i, rows):
        zin_ref[p, rows, :] = _conv3_chunk(z_ref, p, i, wz_ref[...], bz_ref[...], seq)
    each_chunk(load_z)

    _fft_core_kernel(zin_ref, kf_ref, g_ref, gi_ref, ki_ref, h0_ref, o_ref, s1_ref, s2_ref, n1_in=n1, n1_out=n1)

    def gate1(p, i, rows):
        xg = _conv3_chunk(x1_ref, p, i, w1_ref[...], b1_ref[...], seq)
        zin_ref[p, rows, :] = xg * (o_ref[p, rows, :] + sk_ref[0:1, :] * zin_ref[p, rows, :])
    each_chunk(gate1)

    _fft_core_kernel(zin_ref, kf_ref, g_ref, gi_ref, ki_ref, h1_ref, o_ref, s1_ref, s2_ref, n1_in=n1, n1_out=n1)

    def gate2(p, i, rows):
        xg = _conv3_chunk(x2_ref, p, i, w2_ref[...], b2_ref[...], seq)
        o_ref[p, rows, :] = xg * (o_ref[p, rows, :] + sk_ref[1:2, :] * zin_ref[p, rows, :])
    each_chunk(gate2)


def _hyena_pairs(hyg3, w_short, b_short, spec, filt_skip, mats, ct=256):
    kf, g, ginv, ki = mats
    bsz, seq, _ = hyg3.shape
    c = HYENA_WIDTH
    nc = c // ct
    hy = lambda k: pl.BlockSpec((2, seq, ct), lambda cc, p: (p, 0, k * nc + cc))
    wsp = lambda k: pl.BlockSpec((3, ct), lambda cc, p: (0, k * nc + cc))
    bsp = lambda k: pl.BlockSpec((1, ct), lambda cc, p: (0, k * nc + cc))
    return pl.pallas_call(
        functools.partial(_hyena_pairs_kernel, seq=seq),
        grid=(nc, bsz // 2),
        in_specs=[
            hy(0), hy(1), hy(2), wsp(0), bsp(0), wsp(1), bsp(1), wsp(2), bsp(2),
            pl.BlockSpec((HYENA_ORDER, ct), lambda cc, p: (0, cc)),
            _const_spec((None,) + kf.shape[1:], lambda cc, p: (0, 0, 0, 0)),
            _const_spec(g.shape, lambda cc, p: (0, 0)),
            _const_spec(ginv.shape, lambda cc, p: (0, 0)),
            _const_spec((None,) + ki.shape[1:], lambda cc, p: (0, 0, 0, 0)),
            _const_spec((SPEC_ROWS, ct), lambda cc, p: (0, cc)),
            _const_spec((SPEC_ROWS, ct), lambda cc, p: (0, nc + cc)),
        ],
        out_specs=pl.BlockSpec((2, seq, ct), lambda cc, p: (p, 0, cc)),
        out_shape=jax.ShapeDtypeStruct((bsz, seq, c), F32),
        scratch_shapes=[pltpu.VMEM((2, seq, ct), F32), pltpu.VMEM((SPEC_ROWS, ct), F32),
                        pltpu.VMEM((SPEC_ROWS, ct), F32)],
        compiler_params=pltpu.CompilerParams(dimension_semantics=("arbitrary", "arbitrary"),
                                             vmem_limit_bytes=FUSED_VMEM_LIMIT),
        name="hyena_pairs",
    )(hyg3, hyg3, hyg3, w_short, b_short, w_short, b_short, w_short, b_short, filt_skip,
      kf, g, ginv, ki, spec, spec)


def _hyena(hyg3, w_short, b_short, filt, filt_skip):
    bsz, seq, _ = hyg3.shape
    c = HYENA_WIDTH
    taps = _filter_taps(seq, *filt)
    if 2 * seq == FFT_N:
        mats = _dft_matrices(seq // FFT_N2, seq // FFT_N2, 1)
        mats_full = _dft_matrices(FFT_N1, FFT_N1, 1)
        u = jnp.stack([taps, jnp.zeros_like(taps)], axis=0)[None]
        spec = _fft_forward(u, mats_full[0], mats_full[1], 256)[0]
        zh = _hyena_pairs(hyg3, w_short, b_short, spec, filt_skip, mats)
        return zh.reshape(bsz * seq, c)
    hc = _short_conv(hyg3, w_short, b_short)
    hc2 = hc.reshape(bsz * seq, 3 * c)
    assert bsz == 1 and 2 * seq == RADIX8 * FFT_N
    n_blocks = seq // FFT_N
    mats = _dft_matrices(FFT_N1, FFT_N1, N_ALPHA)
    tap_res = _radix8_split(taps, 0, HYENA_ORDER * c, RADIX8)
    spec = _fft_forward(tap_res, mats[0], mats[1], 256)
    z = hc2
    zcol = 0
    for n in range(HYENA_ORDER):
        u = _radix8_split(z, zcol, c, n_blocks)
        v = _fft_conv_residues(u, spec, n * c, mats)
        z = _radix8_merge_gate(v, z, zcol, hc2, (n + 1) * c, filt_skip[n][None, :], n_blocks)
        zcol = 0
    return z


def _merge_kernel(o0_ref, o1_ref, o2_ref, l0_ref, l1_ref, l2_ref, zh_ref, ga_ref, gh_ref, x_ref,
                  wa_ref, wh_ref, wo_ref, out_ref):
    l0, l1, l2 = l0_ref[...], l1_ref[...], l2_ref[...]
    m = jnp.maximum(jnp.maximum(l0, l1), l2)
    e0, e1, e2 = jnp.exp(l0 - m), jnp.exp(l1 - m), jnp.exp(l2 - m)
    tot = e0 + e1 + e2
    w0, w1, w2 = e0 / tot, e1 / tot, e2 / tot
    heads = []
    for h in range(HEADS_PER_GROUP):
        heads.append(o0_ref[h] * w0[:, h:h + 1] + o1_ref[h] * w1[:, h:h + 1] + o2_ref[h] * w2[:, h:h + 1])
    attn = jnp.concatenate(heads, axis=1).astype(BF16)
    ab = jnp.dot(attn, wa_ref[...], preferred_element_type=F32)
    hb = jnp.dot(zh_ref[...].astype(BF16), wh_ref[...], preferred_element_type=F32)
    merged = (jax.nn.sigmoid(ga_ref[...].astype(F32)) * ab
              + jax.nn.sigmoid(gh_ref[...].astype(F32)) * hb)
    out_ref[...] = x_ref[...] + jnp.dot(merged.astype(BF16), wo_ref[...], preferred_element_type=F32)


def _merge(outs, lses, zh, hyg, x2d, wa, wh, wo, tm=512):
    t, d = x2d.shape
    seq = outs[0].shape[2]
    per_seq = seq // tm
    tok = lambda w: pl.BlockSpec((tm, w), lambda i: (i, 0))
    full = lambda a: _const_spec(a.shape, lambda i: (0, 0))
    heads = pl.BlockSpec((None, HEADS_PER_GROUP, tm, HEAD_DIM),
                         lambda i: (i // per_seq, 0, i % per_seq, 0))
    gcol = 3 * HYENA_WIDTH // d
    return pl.pallas_call(
        _merge_kernel,
        grid=(t // tm,),
        in_specs=[heads] * 3 + [tok(HEAD_DIM)] * 3 + [
            tok(HYENA_WIDTH),
            pl.BlockSpec((tm, d), lambda i: (i, gcol)),
            pl.BlockSpec((tm, d), lambda i: (i, gcol + 1)),
            tok(d), full(wa), full(wh), full(wo)],
        out_specs=tok(d),
        out_shape=jax.ShapeDtypeStruct((t, d), F32),
        compiler_params=_cparams(("parallel",)),
        name="merge_out_proj",
    )(*outs, *lses, zh, hyg, hyg, x2d, wa, wh, wo)


def _mlp_kernel(x_ref, g_ref, w1_ref, w2_ref, o_ref, xn_ref, acc_ref):
    j = pl.program_id(1)

    @pl.when(j == 0)
    def _():
        x = x_ref[...]
        ms = jnp.mean(x * x, axis=-1, keepdims=True)
        xn_ref[...] = (x * lax.rsqrt(ms + NORM_EPS) * g_ref[...]).astype(BF16)
        acc_ref[...] = jnp.zeros_like(acc_ref)

    h = jnp.dot(xn_ref[...], w1_ref[...], preferred_element_type=F32)
    a = jnp.square(jnp.maximum(h, 0.0)).astype(BF16)
    acc_ref[...] += jnp.dot(a, w2_ref[...], preferred_element_type=F32)

    @pl.when(j == pl.num_programs(1) - 1)
    def _():
        o_ref[...] = x_ref[...] + acc_ref[...]


def _mlp(x2d, g, w1, w2, tm=2048, tf=512):
    t, d = x2d.shape
    f = w1.shape[1]
    return pl.pallas_call(
        _mlp_kernel,
        grid=(t // tm, f // tf),
        in_specs=[
            pl.BlockSpec((tm, d), lambda i, j: (i, 0)),
            pl.BlockSpec((1, d), lambda i, j: (0, 0)),
            pl.BlockSpec((d, tf), lambda i, j: (0, j)),
            pl.BlockSpec((tf, d), lambda i, j: (j, 0)),
        ],
        out_specs=pl.BlockSpec((tm, d), lambda i, j: (i, 0)),
        out_shape=jax.ShapeDtypeStruct((t, d), F32),
        scratch_shapes=[pltpu.VMEM((tm, d), BF16), pltpu.VMEM((tm, d), F32)],
        compiler_params=_cparams(("parallel", "arbitrary")),
        name="mlp",
    )(x2d, g, w1, w2)


def _encoder_layer(x, p):
    bsz, seq, d = x.shape
    x2d = x.reshape(bsz * seq, d)
    hyg = _norm_matmul(x2d, p['g_mix'], p['w_hyg'], BF16)
    outs, lses = [], []
    for gi, (_, dil) in enumerate(DILATION_GROUPS):
        qkv_g = _qkv_group(x2d, p['g_mix'], p['w_qkv'][gi], p['head_gain'], dil)
        o, s = _attn_group(qkv_g, p['rel_bias'], gi, bsz, seq)
        outs.append(o)
        lses.append(s)
    zh = _hyena(hyg.reshape(bsz, seq, HYG_WIDTH), p['w_short'], p['b_short'], p['filt'], p['filt_skip'])
    x1 = _merge(outs, lses, zh, hyg, x2d, p['wa'], p['wh'], p['wo'])
    y = _mlp(x1, p['g_mlp'], p['w1'], p['w2'])
    return y.reshape(bsz, seq, d)


def kernel(x_prompt, x_sample, rel_bias, g_mix, w_in, g_q, g_k, w_attn_branch, w_short, b_short,
           filt_w1, filt_b1, filt_w2, filt_b2, filt_w3, filt_b3, filt_w4, filt_freq, filt_skip,
           w_hyena_branch, w_out, g_mlp, w_ff1, w_ff2):
    y_prompt, y_sample = x_prompt, x_sample
    for l in range(g_mix.shape[0]):
        w_in_b = w_in[l].astype(BF16)
        w_qkv = [jnp.concatenate([w_in_b[:, s * ATTN_WIDTH + gi * GROUP_WIDTH:s * ATTN_WIDTH + (gi + 1) * GROUP_WIDTH]
                                  for s in range(3)], axis=1) for gi in range(N_GROUPS)]
        head_gain = jnp.concatenate([
            jnp.tile(g_q[l].astype(F32) * (HEAD_DIM ** -0.5), HEADS_PER_GROUP),
            jnp.tile(g_k[l].astype(F32), HEADS_PER_GROUP),
            jnp.ones((GROUP_WIDTH,), F32)])[None, :]
        p = dict(
            g_mix=g_mix[l][None, :].astype(F32),
            w_qkv=w_qkv, w_hyg=w_in_b[:, QKV_WIDTH:],
            head_gain=head_gain,
            rel_bias=rel_bias,
            w_short=w_short[l], b_short=b_short[l][None, :],
            filt=(filt_w1[l], filt_b1[l], filt_w2[l], filt_b2[l], filt_w3[l], filt_b3[l], filt_w4[l],
                  filt_freq[l]),
            filt_skip=filt_skip[l],
            wa=w_attn_branch[l].astype(BF16), wh=w_hyena_branch[l].astype(BF16),
            wo=w_out[l].astype(BF16),
            g_mlp=g_mlp[l][None, :].astype(F32), w1=w_ff1[l].astype(BF16), w2=w_ff2[l].astype(BF16))
        y_prompt = _encoder_layer(y_prompt, p)
        y_sample = _encoder_layer(y_sample, p)
    return (y_prompt, y_sample)
```

```python
import functools
import math

import numpy as np
import jax
import jax.numpy as jnp
from jax import lax
from jax.experimental import pallas as pl
from jax.experimental.pallas import tpu as pltpu

F32 = jnp.float32
BF16 = jnp.bfloat16

D_MODEL = 1024
HEAD_DIM = 128
HEADS_PER_GROUP = 4
DILATION_GROUPS = ((128, 1), (512, 4), (2048, 16))
N_GROUPS = len(DILATION_GROUPS)
ATTN_WIDTH = N_GROUPS * HEADS_PER_GROUP * HEAD_DIM
GROUP_WIDTH = HEADS_PER_GROUP * HEAD_DIM
GROUP_QKV = 3 * GROUP_WIDTH
BAND = 64
N_BUCKETS = 32
MAX_DISTANCE = 1024
HYENA_WIDTH = D_MODEL
HYENA_ORDER = 2
FILTER_BANDS = 16
FILTER_HIDDEN = 64
DECAY_TARGET = 1e-2
FAST_DECAY_PCT = 0.3
SLOW_DECAY_PCT = 1.5
QKV_WIDTH = 3 * ATTN_WIDTH
HYG_WIDTH = 3 * HYENA_WIDTH + 2 * D_MODEL
NORM_EPS = 1e-6
MASK_VALUE = -1e30

FFT_N = 4096
FFT_N1 = 32
FFT_N2 = 128
SUB = 8
LANES = 128
HALO = 16
N_CHUNK = FFT_N2 // SUB
SPEC_ROWS = 2 * FFT_N
RADIX8 = 8
N_ALPHA = 5

VMEM_LIMIT = 56 * 1024 * 1024
FUSED_VMEM_LIMIT = 62 * 1024 * 1024


def _cparams(sem):
    return pltpu.CompilerParams(dimension_semantics=sem, vmem_limit_bytes=VMEM_LIMIT)


def _norm_matmul_kernel(x_ref, g_ref, w_ref, o_ref, xn_ref):
    @pl.when(pl.program_id(1) == 0)
    def _():
        x = x_ref[...]
        ms = jnp.mean(x * x, axis=-1, keepdims=True)
        xn_ref[...] = (x * lax.rsqrt(ms + NORM_EPS) * g_ref[...]).astype(BF16)

    o_ref[...] = jnp.dot(xn_ref[...], w_ref[...], preferred_element_type=F32).astype(o_ref.dtype)


def _norm_matmul(x2d, g, w_bf16, out_dtype, tm=2048, tn=1280):
    t, d = x2d.shape
    n = w_bf16.shape[1]
    return pl.pallas_call(
        _norm_matmul_kernel,
        grid=(t // tm, n // tn),
        in_specs=[
            pl.BlockSpec((tm, d), lambda i, j: (i, 0)),
            pl.BlockSpec((1, d), lambda i, j: (0, 0)),
            pl.BlockSpec((d, tn), lambda i, j: (0, j)),
        ],
        out_specs=pl.BlockSpec((tm, tn), lambda i, j: (i, j)),
        out_shape=jax.ShapeDtypeStruct((t, n), out_dtype),
        scratch_shapes=[pltpu.VMEM((tm, d), BF16)],
        compiler_params=_cparams(("parallel", "arbitrary")),
        name="norm_matmul",
    )(x2d, g, w_bf16)


def _qkv_group_kernel(x_ref, g_ref, w_ref, hg_ref, o_ref, *scratch, tm, dil):
    rows = tm // dil
    x = x_ref[...]
    ms = jnp.mean(x * x, axis=-1, keepdims=True)
    xn = x * lax.rsqrt(ms + NORM_EPS) * g_ref[...]
    if dil == 1:
        xn = xn.astype(BF16)
    else:
        s_ref, = scratch
        n_slab = xn.shape[1] // LANES
        for c in range(n_slab):
            s_ref[c * tm:(c + 1) * tm, :] = xn[:, c * LANES:(c + 1) * LANES]
        parts = []
        for r in range(dil):
            slabs = [s_ref[pl.ds(c * tm + r, rows, stride=dil), :] for c in range(n_slab)]
            parts.append(jnp.concatenate(slabs, axis=1).astype(BF16))
        xn = jnp.concatenate(parts, axis=0)
    hg = hg_ref[...]
    for jt in range(3):
        acc = jnp.dot(xn, w_ref[:, jt * GROUP_WIDTH:(jt + 1) * GROUP_WIDTH], preferred_element_type=F32)
        if jt < 2:
            heads = []
            for h in range(HEADS_PER_GROUP):
                a = acc[:, h * HEAD_DIM:(h + 1) * HEAD_DIM]
                ms = jnp.mean(a * a, axis=-1, keepdims=True)
                c0 = jt * GROUP_WIDTH + h * HEAD_DIM
                heads.append(a * lax.rsqrt(ms + NORM_EPS) * hg[:, c0:c0 + HEAD_DIM])
            acc = jnp.concatenate(heads, axis=1)
        acc = acc.astype(BF16)
        for r in range(dil):
            c0 = r * GROUP_QKV + jt * GROUP_WIDTH
            o_ref[:, c0:c0 + GROUP_WIDTH] = acc[r * rows:(r + 1) * rows, :]


def _qkv_group(x2d, g, w_g, head_gain_g, dil, tm=1024):
    t, d = x2d.shape
    return pl.pallas_call(
        functools.partial(_qkv_group_kernel, tm=tm, dil=dil),
        grid=(t // tm,),
        in_specs=[
            pl.BlockSpec((tm, d), lambda i: (i, 0)),
            pl.BlockSpec((1, d), lambda i: (0, 0)),
            pl.BlockSpec((d, GROUP_QKV), lambda i: (0, 0)),
            pl.BlockSpec((1, GROUP_QKV), lambda i: (0, 0)),
        ],
        out_specs=pl.BlockSpec((tm // dil, dil * GROUP_QKV), lambda i: (i, 0)),
        out_shape=jax.ShapeDtypeStruct((t // dil, dil * GROUP_QKV), BF16),
        scratch_shapes=[] if dil == 1 else [pltpu.VMEM((tm * d // LANES, LANES), F32)],
        compiler_params=_cparams(("parallel",)),
        name=f"qkv_group_d{dil}",
    )(x2d, g, w_g, head_gain_g)


def _t5_bucket_np(rel):
    nb = N_BUCKETS // 2
    max_exact = nb // 2
    side = np.where(rel > 0, nb, 0)
    n = np.abs(rel)
    nf = np.maximum(n, 1).astype(np.float32)
    large = max_exact + (np.log(nf / np.float32(max_exact)) / np.float32(math.log(MAX_DISTANCE / max_exact))
                         * np.float32(nb - max_exact)).astype(np.int32)
    large = np.minimum(large, nb - 1)
    return side + np.where(n < max_exact, n, large)


def _band_bucket_index(dil):
    qi = np.arange(2 * BAND)[:, None]
    kj = np.arange(4 * BAND)[None, :]
    delta = kj - BAND - qi
    idx = _t5_bucket_np(delta * dil).astype(np.int32)
    return np.where(np.abs(delta) <= BAND, idx, -1).astype(np.int32)


def _attn_kernel(idx_ref, tbl_ref, c_ref, p_ref, n_ref, o_ref, lse_ref, bias_ref, *, tq, m_len, dil):
    qb = 2 * BAND
    wb = 4 * BAND
    nsub = tq // qb
    first = (pl.program_id(0) == 0) & (pl.program_id(1) == 0)

    @pl.when(first)
    def _():
        idx = idx_ref[...]
        for h in range(HEADS_PER_GROUP):
            acc = jnp.full((qb, wb), MASK_VALUE, F32)
            for b in range(N_BUCKETS):
                acc = jnp.where(idx == b, tbl_ref[h, b], acc)
            bias_ref[h] = acc

    i = pl.program_id(1)
    col = lax.broadcasted_iota(jnp.int32, (qb, wb), 1)
    lane = lax.broadcasted_iota(jnp.int32, (qb, HEAD_DIM), 1)

    def window(s, cols):
        lo, hi = s * qb - BAND, s * qb + qb + BAND
        parts = []
        if lo < 0:
            parts.append(p_ref[0, :, cols])
            lo = 0
        parts.append(c_ref[0, lo:min(hi, tq), cols])
        if hi > tq:
            parts.append(n_ref[0, :, cols])
        return parts[0] if len(parts) == 1 else jnp.concatenate(parts, axis=0)

    for s in range(nsub):
        lo = BAND - i * tq - qb * s
        valid = (col >= lo) & (col < m_len + lo)
        for r in range(dil):
            base = r * GROUP_QKV
            lse_tile = jnp.zeros((qb, HEAD_DIM), F32)
            rows = pl.ds(s * qb * dil + r, qb, stride=dil) if dil > 1 else pl.ds(s * qb, qb)
            for h in range(HEADS_PER_GROUP):
                qs = c_ref[0, s * qb:(s + 1) * qb, base + h * HEAD_DIM:base + (h + 1) * HEAD_DIM]
                kk = window(s, slice(base + GROUP_WIDTH + h * HEAD_DIM, base + GROUP_WIDTH + (h + 1) * HEAD_DIM))
                sc = lax.dot_general(qs, kk, (((1,), (1,)), ((), ())), preferred_element_type=F32)
                sc = jnp.where(valid, sc + bias_ref[h], MASK_VALUE)
                mx = jnp.max(sc, axis=-1, keepdims=True)
                p = jnp.exp(sc - mx)
                den = jnp.sum(p, axis=-1, keepdims=True)
                vv = window(s, slice(base + 2 * GROUP_WIDTH + h * HEAD_DIM,
                                     base + 2 * GROUP_WIDTH + (h + 1) * HEAD_DIM))
                o = jnp.dot(p.astype(BF16), vv, preferred_element_type=F32) / den
                o_ref[0, h, rows, :] = o
                lse_tile = jnp.where(lane == h, mx + jnp.log(den), lse_tile)
            lse_ref[0, rows, :] = lse_tile


def _attn_group(qkv_g, rel_bias, gi, bsz, seq):
    _, dil = DILATION_GROUPS[gi]
    m_len = seq // dil
    tq = min(m_len, {1: 2048, 4: 512}.get(dil, 128))
    nblk = tq // BAND
    n_halo = m_len // BAND
    width = dil * GROUP_QKV
    view = qkv_g.reshape(bsz, m_len, width)
    idx = jnp.asarray(_band_bucket_index(dil))
    tbl = rel_bias[:, gi * HEADS_PER_GROUP:(gi + 1) * HEADS_PER_GROUP].T.astype(F32)
    o, lse = pl.pallas_call(
        functools.partial(_attn_kernel, tq=tq, m_len=m_len, dil=dil),
        grid=(bsz, m_len // tq),
        in_specs=[
            pl.BlockSpec((2 * BAND, 4 * BAND), lambda b, i: (0, 0)),
            pl.BlockSpec(memory_space=pltpu.SMEM),
            pl.BlockSpec((1, tq, width), lambda b, i: (b, i, 0)),
            pl.BlockSpec((1, BAND, width), lambda b, i: (b, jnp.maximum(i * nblk - 1, 0), 0)),
            pl.BlockSpec((1, BAND, width), lambda b, i: (b, jnp.minimum((i + 1) * nblk, n_halo - 1), 0)),
        ],
        out_specs=[
            pl.BlockSpec((1, HEADS_PER_GROUP, tq * dil, HEAD_DIM), lambda b, i: (b, 0, i, 0)),
            pl.BlockSpec((1, tq * dil, HEAD_DIM), lambda b, i: (b, i, 0)),
        ],
        out_shape=[
            jax.ShapeDtypeStruct((bsz, HEADS_PER_GROUP, seq, HEAD_DIM), F32),
            jax.ShapeDtypeStruct((bsz, seq, HEAD_DIM), F32),
        ],
        scratch_shapes=[pltpu.VMEM((HEADS_PER_GROUP, 2 * BAND, 4 * BAND), F32)],
        compiler_params=_cparams(("arbitrary", "arbitrary")),
        name=f"band_attn_g{gi}",
    )(idx, tbl, view, view, view)
    return o, lse.reshape(bsz * seq, HEAD_DIM)


def _short_conv_kernel(c_ref, p_ref, n_ref, w_ref, b_ref, o_ref, *, tl):
    i = pl.program_id(1)
    last = pl.num_programs(1) - 1
    cur = c_ref[0].astype(F32)
    prev_row = jnp.where(i == 0, 0.0, p_ref[0, HALO - 1:HALO, :].astype(F32))
    next_row = jnp.where(i == last, 0.0, n_ref[0, 0:1, :].astype(F32))
    row = lax.broadcasted_iota(jnp.int32, cur.shape, 0)
    up = jnp.where(row == 0, prev_row, pltpu.roll(cur, 1, 0))
    dn = jnp.where(row == tl - 1, next_row, pltpu.roll(cur, tl - 1, 0))
    w = w_ref[...]
    o_ref[0] = up * w[0:1, :] + cur * w[1:2, :] + dn * w[2:3, :] + b_ref[...]


def _short_conv(hyg3, w_short, b_short, tl=512, ct=512):
    bsz, seq, _ = hyg3.shape
    width = w_short.shape[1]
    nsub = tl // HALO
    return pl.pallas_call(
        functools.partial(_short_conv_kernel, tl=tl),
        grid=(bsz, seq // tl, width // ct),
        in_specs=[
            pl.BlockSpec((1, tl, ct), lambda b, i, c: (b, i, c)),
            pl.BlockSpec((1, HALO, ct), lambda b, i, c: (b, jnp.maximum(i * nsub - 1, 0), c)),
            pl.BlockSpec((1, HALO, ct), lambda b, i, c: (b, jnp.minimum((i + 1) * nsub, seq // HALO - 1), c)),
            pl.BlockSpec((3, ct), lambda b, i, c: (0, c)),
            pl.BlockSpec((1, ct), lambda b, i, c: (0, c)),
        ],
        out_specs=pl.BlockSpec((1, tl, ct), lambda b, i, c: (b, i, c)),
        out_shape=jax.ShapeDtypeStruct((bsz, seq, width), F32),
        compiler_params=_cparams(("parallel", "parallel", "parallel")),
        name="short_conv",
    )(hyg3, hyg3, hyg3, w_short, b_short)


FILTER_PACK = 4


def _filter_kernel(w1_ref, t1_ref, b1_ref, w2_ref, b2_ref, w3_ref, b3_ref, fr_ref, w4_ref,
                   ad_ref, o_ref, *, seq, tr):
    hp = lax.Precision.HIGHEST
    rows = tr // FILTER_PACK
    hid = FILTER_HIDDEN
    nfeat = 2 * FILTER_BANDS
    base = pl.program_id(0) * tr

    def lag_of(shape, lanes_per_group):
        r = lax.broadcasted_iota(jnp.int32, shape, 0)
        g = lax.broadcasted_iota(jnp.int32, shape, 1) // lanes_per_group
        n = base + g * rows + r
        return jnp.where(n < seq, n, 2 * seq - n).astype(F32)

    f = lax.broadcasted_iota(jnp.int32, (rows, FILTER_PACK * nfeat), 1) % nfeat
    band_idx = jnp.where(f < FILTER_BANDS, f, f - FILTER_BANDS).astype(F32)
    bands = 1e-4 + band_idx * ((FILTER_BANDS - 1 - 1e-4) / (FILTER_BANDS - 1))
    ang = (2.0 * math.pi) * lag_of((rows, FILTER_PACK * nfeat), nfeat) / seq
    z = jnp.cos(bands * ang + jnp.where(f < FILTER_BANDS, 0.0, 0.5 * math.pi))
    t = lag_of((rows, FILTER_PACK * hid), hid) / (seq - 1)
    fr = fr_ref[...]
    h = jnp.sin(fr * (jnp.dot(z, w1_ref[...], precision=hp, preferred_element_type=F32)
                      + t * t1_ref[...] + b1_ref[...]))
    h = jnp.sin(fr * (jnp.dot(h, w2_ref[...], precision=hp, preferred_element_type=F32) + b2_ref[...]))
    h = jnp.sin(fr * (jnp.dot(h, w3_ref[...], precision=hp, preferred_element_type=F32) + b3_ref[...]))
    w_hi, w_lo = w4_ref[0], w4_ref[1]
    for g in range(FILTER_PACK):
        hg = h[:, g * hid:(g + 1) * hid]
        h_hi = hg.astype(BF16)
        h_lo = (hg - h_hi.astype(F32)).astype(BF16)
        taps = (jnp.dot(h_hi, w_hi, preferred_element_type=F32) + jnp.dot(h_hi, w_lo, preferred_element_type=F32)
                + jnp.dot(h_lo, w_hi, preferred_element_type=F32))
        decay = jnp.exp(-t[:, g * hid:g * hid + 1] * ad_ref[...])
        n = base + g * rows + lax.broadcasted_iota(jnp.int32, (rows, 1), 0)
        o_ref[g * rows:(g + 1) * rows, :] = jnp.where(n == seq, 0.0, taps * decay)


def _filter_taps(seq, w1, b1, w2, b2, w3, b3, w4, freq, tr=1024):
    c = HYENA_WIDTH
    hid = FILTER_HIDDEN
    eye = jnp.eye(FILTER_PACK, dtype=F32)
    tile = lambda v: jnp.tile(v.reshape(1, -1), (1, FILTER_PACK))
    w1bd = jnp.kron(eye, w1[1:])
    w2bd, w3bd = jnp.kron(eye, w2), jnp.kron(eye, w3)
    w4r = w4.reshape(hid, HYENA_ORDER, 2, c)
    w4d = jnp.transpose(w4r, (2, 0, 1, 3)).reshape(2, hid, HYENA_ORDER * c)
    w4_hi = w4d.astype(BF16)
    w4_lo = (w4d - w4_hi.astype(F32)).astype(BF16)
    w4d = jnp.stack([w4_hi, w4_lo], axis=1)
    nblk = 2 * seq // tr
    min_decay = math.log(DECAY_TARGET) / SLOW_DECAY_PCT
    max_decay = math.log(DECAY_TARGET) / FAST_DECAY_PCT
    ad = jnp.abs(jnp.linspace(min_decay, max_decay, c, dtype=F32))
    ad = jnp.tile(ad, HYENA_ORDER)[None, :]
    full = lambda shape: pl.BlockSpec(shape, lambda i: (0,) * len(shape))
    ph = FILTER_PACK * hid
    return pl.pallas_call(
        functools.partial(_filter_kernel, seq=seq, tr=tr),
        grid=(nblk,),
        in_specs=[full(w1bd.shape), full((1, ph)), full((1, ph)), full((ph, ph)), full((1, ph)),
                  full((ph, ph)), full((1, ph)), full((1, ph)),
                  pl.BlockSpec((None, 2, hid, HYENA_ORDER * c), lambda i: (i // (nblk // 2), 0, 0, 0)),
                  full((1, HYENA_ORDER * c))],
        out_specs=pl.BlockSpec((tr, HYENA_ORDER * c), lambda i: (i, 0)),
        out_shape=jax.ShapeDtypeStruct((2 * seq, HYENA_ORDER * c), F32),
        compiler_params=_cparams(("parallel",)),
        name="hyena_filter_taps",
    )(w1bd, tile(w1[0]), tile(b1), w2bd, tile(b2), w3bd, tile(b3), tile(freq), w4d, ad)


@functools.lru_cache(maxsize=None)
def _dft_matrices(n1_in, n1_out, n_alpha):
    big = RADIX8 * FFT_N
    a = np.arange(n_alpha).reshape(-1, 1, 1, 1, 1)
    j = np.arange(N_CHUNK).reshape(1, -1, 1, 1, 1)
    k1 = np.arange(FFT_N1).reshape(1, 1, -1, 1, 1)
    t = np.arange(SUB).reshape(1, 1, 1, 1, -1)
    bf16 = jnp.dtype(BF16)

    def stage_a(n1_count, inverse):
        n1 = np.arange(n1_count).reshape(1, 1, 1, -1, 1)
        n2 = SUB * j + t
        e = (RADIX8 * FFT_N2 * n1 * k1 + RADIX8 * n2 * k1 + a * (FFT_N2 * n1 + n2)) % big
        ang = e * (2.0 * np.pi / big)
        cr = np.cos(ang)
        ci = np.sin(ang) if inverse else -np.sin(ang)
        if inverse:
            cr = cr / FFT_N
            ci = ci / FFT_N
        eye = np.eye(SUB)
        if not inverse:
            blk = lambda m: np.einsum('ajknt,ts->ajksnt', m, eye)
            top = np.concatenate([blk(cr), blk(-ci)], axis=4)
            bot = np.concatenate([blk(ci), blk(cr)], axis=4)
            m = np.stack([top, bot], axis=2)
            return m.reshape(n_alpha, N_CHUNK, 2 * FFT_N1 * SUB, 2 * n1_count * SUB).astype(bf16)
        blk = lambda m: np.einsum('ajknt,ts->ajnskt', m, eye)
        top = np.concatenate([blk(cr), blk(-ci)], axis=4)
        bot = np.concatenate([blk(ci), blk(cr)], axis=4)
        m = np.stack([top, bot], axis=2)
        return m.reshape(n_alpha, N_CHUNK, 2 * n1_count * SUB, 2 * FFT_N1 * SUB).astype(bf16)

    kf = stage_a(n1_in, False)
    ki = stage_a(n1_out, True)
    n2 = np.arange(FFT_N2)
    ang = ((n2[:, None] * n2[None, :]) % FFT_N2) * (2.0 * np.pi / FFT_N2)
    gr, gi = np.cos(ang), -np.sin(ang)
    g = np.block([[gr, -gi], [gi, gr]]).astype(bf16)
    ginv = np.block([[gr, gi], [-gi, gr]]).astype(bf16)
    return kf, g, ginv, ki


def _chunk_loop(body):
    for j in range(N_CHUNK):
        body(j, j * SUB)


def _fft_stage_a(z_ref, kf, s1_ref, n1_in):
    def body(j, off):
        chunks = [z_ref[p, pl.ds(FFT_N2 * n1 + off, SUB), :].astype(F32)
                  for p in range(2) for n1 in range(n1_in)]
        slab = jnp.concatenate(chunks, axis=0).astype(BF16)
        a = jnp.dot(kf[j], slab, preferred_element_type=F32)
        for q in range(2):
            for k1 in range(FFT_N1):
                r = (q * FFT_N1 + k1) * SUB
                s1_ref[pl.ds(k1 * 2 * FFT_N2 + q * FFT_N2 + off, SUB), :] = a[r:r + SUB, :]
    _chunk_loop(body)


def _fft_core_kernel(z_ref, kf_ref, g_ref, gi_ref, ki_ref, h_ref, o_ref, s1_ref, s2_ref, *, n1_in, n1_out):
    _fft_stage_a(z_ref, kf_ref, s1_ref, n1_in)

    for k1 in range(FFT_N1):
        r0 = k1 * 2 * FFT_N2
        slab = s1_ref[pl.ds(r0, 2 * FFT_N2), :].astype(BF16)
        x = jnp.dot(g_ref[...], slab, preferred_element_type=F32)
        xr, xi = x[:FFT_N2], x[FFT_N2:]
        hr = h_ref[pl.ds(r0, FFT_N2), :]
        hi = h_ref[pl.ds(r0 + FFT_N2, FFT_N2), :]
        y = jnp.concatenate([xr * hr - xi * hi, xr * hi + xi * hr], axis=0).astype(BF16)
        s2_ref[pl.ds(r0, 2 * FFT_N2), :] = jnp.dot(gi_ref[...], y, preferred_element_type=F32)

    def stage_c(j, off):
        chunks = [s2_ref[pl.ds(k1 * 2 * FFT_N2 + q * FFT_N2 + off, SUB), :]
                  for q in range(2) for k1 in range(FFT_N1)]
        slab = jnp.concatenate(chunks, axis=0).astype(BF16)
        y = jnp.dot(ki_ref[j], slab, preferred_element_type=F32)
        for p in range(2):
            for n1 in range(n1_out):
                r = (p * n1_out + n1) * SUB
                o_ref[p, pl.ds(FFT_N2 * n1 + off, SUB), :] = y[r:r + SUB, :]
    _chunk_loop(stage_c)


def _fft_core_full_kernel(z_ref, kf_ref, g_ref, gi_ref, ki_ref, h_ref, o_ref, s1_ref):
    half = FFT_N1 // 2
    _fft_stage_a(z_ref, kf_ref, s1_ref, FFT_N1)

    for k1 in range(FFT_N1):
        r0 = k1 * 2 * FFT_N2
        slab = s1_ref[pl.ds(r0, 2 * FFT_N2), :].astype(BF16)
        x = jnp.dot(g_ref[...], slab, preferred_element_type=F32)
        xr, xi = x[:FFT_N2], x[FFT_N2:]
        hr = h_ref[pl.ds(r0, FFT_N2), :]
        hi = h_ref[pl.ds(r0 + FFT_N2, FFT_N2), :]
        y = jnp.concatenate([xr * hr - xi * hi, xr * hi + xi * hr], axis=0).astype(BF16)
        o_ref[k1 // half, pl.ds((k1 % half) * 2 * FFT_N2, 2 * FFT_N2), :] = jnp.dot(
            gi_ref[...], y, preferred_element_type=F32)

    def stage_c(j, off):
        chunks = [o_ref[k1 // half, pl.ds((k1 % half) * 2 * FFT_N2 + q * FFT_N2 + off, SUB), :]
                  for q in range(2) for k1 in range(FFT_N1)]
        slab = jnp.concatenate(chunks, axis=0).astype(BF16)
        y = jnp.dot(ki_ref[j], slab, preferred_element_type=F32)
        for p in range(2):
            for n1 in range(FFT_N1):
                r = (p * FFT_N1 + n1) * SUB
                s1_ref[pl.ds(p * FFT_N + FFT_N2 * n1 + off, SUB), :] = y[r:r + SUB, :]
    _chunk_loop(stage_c)

    def copy_out(i, carry):
        r0 = pl.multiple_of(i * CONV_ROWS, CONV_ROWS)
        for p in range(2):
            o_ref[p, pl.ds(r0, CONV_ROWS), :] = s1_ref[pl.ds(p * FFT_N + r0, CONV_ROWS), :]
        return carry
    lax.fori_loop(0, FFT_N // CONV_ROWS, copy_out, 0)


def _fft_forward_kernel(z_ref, kf_ref, g_ref, o_ref, s1_ref, *, n1_in):
    _fft_stage_a(z_ref, kf_ref, s1_ref, n1_in)

    def stage_b(k1, carry):
        r0 = pl.multiple_of(k1 * 2 * FFT_N2, 2 * FFT_N2)
        slab = s1_ref[pl.ds(r0, 2 * FFT_N2), :].astype(BF16)
        o_ref[pl.ds(r0, 2 * FFT_N2), :] = jnp.dot(g_ref[...], slab, preferred_element_type=F32).astype(o_ref.dtype)
        return carry
    lax.fori_loop(0, FFT_N1, stage_b, 0, unroll=4)


def _const_spec(shape, index_map):
    return pl.BlockSpec(shape, index_map, pipeline_mode=pl.Buffered(1))


def _fft_conv_residues(u, spec, col0, mats, ct=256):
    kf, g, ginv, ki = mats
    c = u.shape[-1]
    cb0 = col0 // ct
    return pl.pallas_call(
        _fft_core_full_kernel,
        grid=(N_ALPHA, c // ct),
        in_specs=[
            pl.BlockSpec((None, 2, FFT_N, ct), lambda a, cc: (a, 0, 0, cc)),
            _const_spec((None,) + kf.shape[1:], lambda a, cc: (a, 0, 0, 0)),
            _const_spec(g.shape, lambda a, cc: (0, 0)),
            _const_spec(ginv.shape, lambda a, cc: (0, 0)),
            _const_spec((None,) + ki.shape[1:], lambda a, cc: (a, 0, 0, 0)),
            _const_spec((None, SPEC_ROWS, ct), lambda a, cc: (a, 0, cb0 + cc)),
        ],
        out_specs=pl.BlockSpec((None, 2, FFT_N, ct), lambda a, cc: (a, 0, 0, cc)),
        out_shape=jax.ShapeDtypeStruct(u.shape, F32),
        scratch_shapes=[pltpu.VMEM((SPEC_ROWS, ct), F32)],
        compiler_params=pltpu.CompilerParams(dimension_semantics=("arbitrary", "arbitrary"),
                                             vmem_limit_bytes=FUSED_VMEM_LIMIT),
        name="fft_conv_residues",
    )(u, kf, g, ginv, ki, spec)


def _fft_forward(u, kf, g, ct):
    n_a, _, _, c = u.shape
    return pl.pallas_call(
        functools.partial(_fft_forward_kernel, n1_in=FFT_N1),
        grid=(n_a, c // ct),
        in_specs=[
            pl.BlockSpec((None, 2, FFT_N, ct), lambda a, cc: (a, 0, 0, cc)),
            _const_spec((None,) + kf.shape[1:], lambda a, cc: (a, 0, 0, 0)),
            _const_spec(g.shape, lambda a, cc: (0, 0)),
        ],
        out_specs=pl.BlockSpec((None, SPEC_ROWS, ct), lambda a, cc: (a, 0, cc)),
        out_shape=jax.ShapeDtypeStruct((n_a, SPEC_ROWS, c), BF16),
        scratch_shapes=[pltpu.VMEM((SPEC_ROWS, ct), F32)],
        compiler_params=_cparams(("arbitrary", "arbitrary")),
        name="fft_forward",
    )(u, kf, g)


def _radix8_coefs(n_blocks):
    a = np.arange(n_blocks)[None, :]
    al = np.arange(N_ALPHA)[:, None]
    ang = 2.0 * np.pi * ((a * al) % RADIX8) / RADIX8
    return np.round(np.cos(ang), 12), np.round(-np.sin(ang), 12)


def _radix8_split_kernel(x_ref, o_ref, *, n_blocks):
    cr, ci = _radix8_coefs(n_blocks)
    xs = [x_ref[a] for a in range(n_blocks)]
    for al in range(N_ALPHA):
        for part, coef in ((0, cr), (1, ci)):
            acc = None
            for a in range(n_blocks):
                w = float(coef[al, a])
                if w == 0.0:
                    continue
                term = xs[a] if w == 1.0 else (-xs[a] if w == -1.0 else xs[a] * w)
                acc = term if acc is None else acc + term
            o_ref[al, part] = jnp.zeros_like(xs[0]) if acc is None else acc


def _radix8_split(x, col0, width, n_blocks, tr=256, ct=512):
    xv = x.reshape(n_blocks, FFT_N, x.shape[1])
    cb0 = col0 // ct
    return pl.pallas_call(
        functools.partial(_radix8_split_kernel, n_blocks=n_blocks),
        grid=(FFT_N // tr, width // ct),
        in_specs=[pl.BlockSpec((n_blocks, tr, ct), lambda i, c: (0, i, cb0 + c))],
        out_specs=pl.BlockSpec((N_ALPHA, 2, tr, ct), lambda i, c: (0, 0, i, c)),
        out_shape=jax.ShapeDtypeStruct((N_ALPHA, 2, FFT_N, width), F32),
        compiler_params=_cparams(("parallel", "parallel")),
        name="radix8_split",
    )(xv)


def _radix8_merge_gate_kernel(v_ref, z_ref, xg_ref, sk_ref, o_ref, *, n_blocks):
    sk = sk_ref[...]
    for a in range(n_blocks):
        acc = None
        for al in range(N_ALPHA):
            cw = (1.0 if al in (0, RADIX8 // 2) else 2.0) / RADIX8
            ang = 2.0 * np.pi * ((a * al) % RADIX8) / RADIX8
            wr = float(np.round(np.cos(ang), 12)) * cw
            wi = float(np.round(np.sin(ang), 12)) * cw
            for w, part in ((wr, 0), (-wi, 1)):
                if w == 0.0:
                    continue
                term = v_ref[al, part] * w
                acc = term if acc is None else acc + term
        o_ref[a] = xg_ref[a] * (acc + sk * z_ref[a])


def _radix8_merge_gate(v, z, zcol0, xg, gcol0, skip_row, n_blocks, tr=256, ct=512):
    c = v.shape[-1]
    zv = z.reshape(n_blocks, FFT_N, z.shape[1])
    gv = xg.reshape(n_blocks, FFT_N, xg.shape[1])
    zb, gb = zcol0 // ct, gcol0 // ct
    out = pl.pallas_call(
        functools.partial(_radix8_merge_gate_kernel, n_blocks=n_blocks),
        grid=(FFT_N // tr, c // ct),
        in_specs=[
            pl.BlockSpec((N_ALPHA, 2, tr, ct), lambda i, cc: (0, 0, i, cc)),
            pl.BlockSpec((n_blocks, tr, ct), lambda i, cc: (0, i, zb + cc)),
            pl.BlockSpec((n_blocks, tr, ct), lambda i, cc: (0, i, gb + cc)),
            pl.BlockSpec((1, ct), lambda i, cc: (0, cc)),
        ],
        out_specs=pl.BlockSpec((n_blocks, tr, ct), lambda i, cc: (0, i, cc)),
        out_shape=jax.ShapeDtypeStruct((n_blocks, FFT_N, c), F32),
        compiler_params=_cparams(("parallel", "parallel")),
        name="radix8_merge_gate",
    )(v, zv, gv, skip_row)
    return out.reshape(n_blocks * FFT_N, c)


CONV_ROWS = 256


def _conv3_chunk(u_ref, p, i, w, b, seq):
    n_chunks = seq // CONV_ROWS
    r0 = pl.multiple_of(i * CONV_ROWS, CONV_ROWS)
    cur = u_ref[p, pl.ds(r0, CONV_ROWS), :].astype(F32)
    pr = pl.multiple_of(jnp.maximum(r0 - HALO, 0), HALO)
    nr = pl.multiple_of(jnp.minimum(r0 + CONV_ROWS, seq - HALO), HALO)
    prev_row = jnp.where(i == 0, 0.0, u_ref[p, pl.ds(pr, HALO), :][HALO - 1:HALO, :].astype(F32))
    next_row = jnp.where(i == n_chunks - 1, 0.0, u_ref[p, pl.ds(nr, HALO), :][0:1, :].astype(F32))
    row = lax.broadcasted_iota(jnp.int32, cur.shape, 0)
    up = jnp.where(row == 0, prev_row, pltpu.roll(cur, 1, 0))
    dn = jnp.where(row == CONV_ROWS - 1, next_row, pltpu.roll(cur, CONV_ROWS - 1, 0))
    return up * w[0:1, :] + cur * w[1:2, :] + dn * w[2:3, :] + b


def _hyena_pairs_kernel(z_ref, x1_ref, x2_ref, wz_ref, bz_ref, w1_ref, b1_ref, w2_ref, b2_ref, sk_ref,
                        kf_ref, g_ref, gi_ref, ki_ref, h0_ref, h1_ref, o_ref, zin_ref, s1_ref, s2_ref,
                        *, seq):
    n1 = seq // FFT_N2
    n_chunks = seq // CONV_ROWS

    def each_chunk(fn):
        for p in range(2):
            def body(i, carry, p=p):
                fn(p, i, pl.ds(pl.multiple_of(i * CONV_ROWS, CONV_ROWS), CONV_ROWS))
                return carry
            lax.fori_loop(0, n_chunks, body, 0)

    def load_z(p, i, rows):
        zin_ref[p, rows, :] = _conv3_chunk(z_ref, p, i, wz_ref[...], bz_ref[...], seq)
    each_chunk(load_z)

    _fft_core_kernel(zin_ref, kf_ref, g_ref, gi_ref, ki_ref, h0_ref, o_ref, s1_ref, s2_ref, n1_in=n1, n1_out=n1)

    def gate1(p, i, rows):
        xg = _conv3_chunk(x1_ref, p, i, w1_ref[...], b1_ref[...], seq)
        zin_ref[p, rows, :] = xg * (o_ref[p, rows, :] + sk_ref[0:1, :] * zin_ref[p, rows, :])
    each_chunk(gate1)

    _fft_core_kernel(zin_ref, kf_ref, g_ref, gi_ref, ki_ref, h1_ref, o_ref, s1_ref, s2_ref, n1_in=n1, n1_out=n1)

    def gate2(p, i, rows):
        xg = _conv3_chunk(x2_ref, p, i, w2_ref[...], b2_ref[...], seq)
        o_ref[p, rows, :] = xg * (o_ref[p, rows, :] + sk_ref[1:2, :] * zin_ref[p, rows, :])
    each_chunk(gate2)


def _hyena_pairs(hyg3, w_short, b_short, spec, filt_skip, mats, ct=256):
    kf, g, ginv, ki = mats
    bsz, seq, _ = hyg3.shape
    c = HYENA_WIDTH
    nc = c // ct
    hy = lambda k: pl.BlockSpec((2, seq, ct), lambda cc, p: (p, 0, k * nc + cc))
    wsp = lambda k: pl.BlockSpec((3, ct), lambda cc, p: (0, k * nc + cc))
    bsp = lambda k: pl.BlockSpec((1, ct), lambda cc, p: (0, k * nc + cc))
    return pl.pallas_call(
        functools.partial(_hyena_pairs_kernel, seq=seq),
        grid=(nc, bsz // 2),
        in_specs=[
            hy(0), hy(1), hy(2), wsp(0), bsp(0), wsp(1), bsp(1), wsp(2), bsp(2),
            pl.BlockSpec((HYENA_ORDER, ct), lambda cc, p: (0, cc)),
            _const_spec((None,) + kf.shape[1:], lambda cc, p: (0, 0, 0, 0)),
            _const_spec(g.shape, lambda cc, p: (0, 0)),
            _const_spec(ginv.shape, lambda cc, p: (0, 0)),
            _const_spec((None,) + ki.shape[1:], lambda cc, p: (0, 0, 0, 0)),
            _const_spec((SPEC_ROWS, ct), lambda cc, p: (0, cc)),
            _const_spec((SPEC_ROWS, ct), lambda cc, p: (0, nc + cc)),
        ],
        out_specs=pl.BlockSpec((2, seq, ct), lambda cc, p: (p, 0, cc)),
        out_shape=jax.ShapeDtypeStruct((bsz, seq, c), F32),
        scratch_shapes=[pltpu.VMEM((2, seq, ct), F32), pltpu.VMEM((SPEC_ROWS, ct), F32),
                        pltpu.VMEM((SPEC_ROWS, ct), F32)],
        compiler_params=pltpu.CompilerParams(dimension_semantics=("arbitrary", "arbitrary"),
                                             vmem_limit_bytes=FUSED_VMEM_LIMIT),
        name="hyena_pairs",
    )(hyg3, hyg3, hyg3, w_short, b_short, w_short, b_short, w_short, b_short, filt_skip,
      kf, g, ginv, ki, spec, spec)


def _hyena(hyg3, w_short, b_short, filt, filt_skip):
    bsz, seq, _ = hyg3.shape
    c = HYENA_WIDTH
    taps = _filter_taps(seq, *filt)
    if 2 * seq == FFT_N:
        mats = _dft_matrices(seq // FFT_N2, seq // FFT_N2, 1)
        mats_full = _dft_matrices(FFT_N1, FFT_N1, 1)
        u = jnp.stack([taps, jnp.zeros_like(taps)], axis=0)[None]
        spec = _fft_forward(u, mats_full[0], mats_full[1], 256)[0]
        zh = _hyena_pairs(hyg3, w_short, b_short, spec, filt_skip, mats)
        return zh.reshape(bsz * seq, c)
    assert bsz == 1 and 2 * seq == RADIX8 * FFT_N, (bsz, seq)
    hc = _short_conv(hyg3, w_short, b_short).reshape(seq, 3 * c)
    n_blocks = seq // FFT_N
    mats = _dft_matrices(FFT_N1, FFT_N1, N_ALPHA)
    tap_res = _radix8_split(taps, 0, HYENA_ORDER * c, RADIX8)
    spec = _fft_forward(tap_res, mats[0], mats[1], 256)
    z = hc
    for n in range(HYENA_ORDER):
        u = _radix8_split(z, 0, c, n_blocks)
        v = _fft_conv_residues(u, spec, n * c, mats)
        z = _radix8_merge_gate(v, z, 0, hc, (n + 1) * c, filt_skip[n][None, :], n_blocks)
    return z


MERGE_SUBTILES = 2


def _merge_kernel(o0_ref, o1_ref, o2_ref, l0_ref, l1_ref, l2_ref, zh_ref, ga_ref, gh_ref, x_ref,
                  wa_ref, wh_ref, wo_ref, out_ref):
    sub = out_ref.shape[0] // MERGE_SUBTILES
    for s in range(MERGE_SUBTILES):
        rs = slice(s * sub, (s + 1) * sub)
        l0, l1, l2 = l0_ref[rs, :], l1_ref[rs, :], l2_ref[rs, :]
        m = jnp.maximum(jnp.maximum(l0, l1), l2)
        e0, e1, e2 = jnp.exp(l0 - m), jnp.exp(l1 - m), jnp.exp(l2 - m)
        tot = e0 + e1 + e2
        w0, w1, w2 = e0 / tot, e1 / tot, e2 / tot
        heads = []
        for h in range(HEADS_PER_GROUP):
            heads.append(o0_ref[h, rs, :] * w0[:, h:h + 1] + o1_ref[h, rs, :] * w1[:, h:h + 1]
                         + o2_ref[h, rs, :] * w2[:, h:h + 1])
        attn = jnp.concatenate(heads, axis=1).astype(BF16)
        ab = jnp.dot(attn, wa_ref[...], preferred_element_type=F32)
        hb = jnp.dot(zh_ref[rs, :].astype(BF16), wh_ref[...], preferred_element_type=F32)
        merged = (jax.nn.sigmoid(ga_ref[rs, :].astype(F32)) * ab
                  + jax.nn.sigmoid(gh_ref[rs, :].astype(F32)) * hb)
        out_ref[rs, :] = x_ref[rs, :] + jnp.dot(merged.astype(BF16), wo_ref[...], preferred_element_type=F32)


def _merge(outs, lses, zh, hyg, x2d, wa, wh, wo, tm=512):
    t, d = x2d.shape
    seq = outs[0].shape[2]
    per_seq = seq // tm
    tok = lambda w: pl.BlockSpec((tm, w), lambda i: (i, 0))
    full = lambda a: _const_spec(a.shape, lambda i: (0, 0))
    heads = pl.BlockSpec((None, HEADS_PER_GROUP, tm, HEAD_DIM),
                         lambda i: (i // per_seq, 0, i % per_seq, 0))
    gcol = 3 * HYENA_WIDTH // d
    return pl.pallas_call(
        _merge_kernel,
        grid=(t // tm,),
        in_specs=[heads] * 3 + [tok(HEAD_DIM)] * 3 + [
            tok(HYENA_WIDTH),
            pl.BlockSpec((tm, d), lambda i: (i, gcol)),
            pl.BlockSpec((tm, d), lambda i: (i, gcol + 1)),
            tok(d), full(wa), full(wh), full(wo)],
        out_specs=tok(d),
        out_shape=jax.ShapeDtypeStruct((t, d), F32),
        compiler_params=_cparams(("parallel",)),
        name="merge_out_proj",
    )(*outs, *lses, zh, hyg, hyg, x2d, wa, wh, wo)


def _mlp_kernel(x_ref, g_ref, w1_ref, w2_ref, o_ref, xn_ref, acc_ref):
    j = pl.program_id(1)

    @pl.when(j == 0)
    def _():
        x = x_ref[...]
        ms = jnp.mean(x * x, axis=-1, keepdims=True)
        xn_ref[...] = (x * lax.rsqrt(ms + NORM_EPS) * g_ref[...]).astype(BF16)
        acc_ref[...] = jnp.zeros_like(acc_ref)

    h = jnp.dot(xn_ref[...], w1_ref[...], preferred_element_type=F32)
    a = jnp.square(jnp.maximum(h, 0.0)).astype(BF16)
    acc_ref[...] += jnp.dot(a, w2_ref[...], preferred_element_type=F32)

    @pl.when(j == pl.num_programs(1) - 1)
    def _():
        o_ref[...] = x_ref[...] + acc_ref[...]


def _mlp(x2d, g, w1, w2, tm=1024, tf=1024):
    t, d = x2d.shape
    f = w1.shape[1]
    return pl.pallas_call(
        _mlp_kernel,
        grid=(t // tm, f // tf),
        in_specs=[
            pl.BlockSpec((tm, d), lambda i, j: (i, 0)),
            pl.BlockSpec((1, d), lambda i, j: (0, 0)),
            pl.BlockSpec((d, tf), lambda i, j: (0, j)),
            pl.BlockSpec((tf, d), lambda i, j: (j, 0)),
        ],
        out_specs=pl.BlockSpec((tm, d), lambda i, j: (i, 0)),
        out_shape=jax.ShapeDtypeStruct((t, d), F32),
        scratch_shapes=[pltpu.VMEM((tm, d), BF16), pltpu.VMEM((tm, d), F32)],
        compiler_params=_cparams(("parallel", "arbitrary")),
        name="mlp",
    )(x2d, g, w1, w2)


def _encoder_layer(x, p):
    bsz, seq, d = x.shape
    x2d = x.reshape(bsz * seq, d)
    hyg = _norm_matmul(x2d, p['g_mix'], p['w_hyg'], BF16)
    outs, lses = [], []
    for gi, (_, dil) in enumerate(DILATION_GROUPS):
        qkv_g = _qkv_group(x2d, p['g_mix'], p['w_qkv'][gi], p['head_gain'], dil)
        o, s = _attn_group(qkv_g, p['rel_bias'], gi, bsz, seq)
        outs.append(o)
        lses.append(s)
    zh = _hyena(hyg.reshape(bsz, seq, HYG_WIDTH), p['w_short'], p['b_short'], p['filt'], p['filt_skip'])
    x1 = _merge(outs, lses, zh, hyg, x2d, p['wa'], p['wh'], p['wo'])
    y = _mlp(x1, p['g_mlp'], p['w1'], p['w2'])
    return y.reshape(bsz, seq, d)


def kernel(x_prompt, x_sample, rel_bias, g_mix, w_in, g_q, g_k, w_attn_branch, w_short, b_short,
           filt_w1, filt_b1, filt_w2, filt_b2, filt_w3, filt_b3, filt_w4, filt_freq, filt_skip,
           w_hyena_branch, w_out, g_mlp, w_ff1, w_ff2):
    y_prompt, y_sample = x_prompt, x_sample
    for l in range(g_mix.shape[0]):
        w_in_b = w_in[l].astype(BF16)
        w_qkv = [jnp.concatenate([w_in_b[:, s * ATTN_WIDTH + gi * GROUP_WIDTH:s * ATTN_WIDTH + (gi + 1) * GROUP_WIDTH]
                                  for s in range(3)], axis=1) for gi in range(N_GROUPS)]
        head_gain = jnp.concatenate([
            jnp.tile(g_q[l].astype(F32) * (HEAD_DIM ** -0.5), HEADS_PER_GROUP),
            jnp.tile(g_k[l].astype(F32), HEADS_PER_GROUP),
            jnp.ones((GROUP_WIDTH,), F32)])[None, :]
        p = dict(
            g_mix=g_mix[l][None, :].astype(F32),
            w_qkv=w_qkv, w_hyg=w_in_b[:, QKV_WIDTH:],
            head_gain=head_gain,
            rel_bias=rel_bias,
            w_short=w_short[l], b_short=b_short[l][None, :],
            filt=(filt_w1[l], filt_b1[l], filt_w2[l], filt_b2[l], filt_w3[l], filt_b3[l], filt_w4[l],
                  filt_freq[l]),
            filt_skip=filt_skip[l],
            wa=w_attn_branch[l].astype(BF16), wh=w_hyena_branch[l].astype(BF16),
            wo=w_out[l].astype(BF16),
            g_mlp=g_mlp[l][None, :].astype(F32), w1=w_ff1[l].astype(BF16), w2=w_ff2[l].astype(BF16))
        y_prompt = _encoder_layer(y_prompt, p)
        y_sample = _encoder_layer(y_sample, p)
    return (y_prompt, y_sample)
```

```python
import functools
import math

import numpy as np
import jax
import jax.numpy as jnp
from jax import lax
from jax.experimental import pallas as pl
from jax.experimental.pallas import tpu as pltpu

F32 = jnp.float32
BF16 = jnp.bfloat16

D_MODEL = 1024
HEAD_DIM = 128
HEADS_PER_GROUP = 4
DILATION_GROUPS = ((128, 1), (512, 4), (2048, 16))
N_GROUPS = len(DILATION_GROUPS)
ATTN_WIDTH = N_GROUPS * HEADS_PER_GROUP * HEAD_DIM
GROUP_WIDTH = HEADS_PER_GROUP * HEAD_DIM
GROUP_QKV = 3 * GROUP_WIDTH
BAND = 64
N_BUCKETS = 32
MAX_DISTANCE = 1024
HYENA_WIDTH = D_MODEL
HYENA_ORDER = 2
FILTER_BANDS = 16
FILTER_HIDDEN = 64
DECAY_TARGET = 1e-2
FAST_DECAY_PCT = 0.3
SLOW_DECAY_PCT = 1.5
QKV_WIDTH = 3 * ATTN_WIDTH
HYG_WIDTH = 3 * HYENA_WIDTH + 2 * D_MODEL
NORM_EPS = 1e-6
MASK_VALUE = -1e30

FFT_N = 4096
FFT_N1 = 32
FFT_N2 = 128
SUB = 8
LANES = 128
HALO = 16
N_CHUNK = FFT_N2 // SUB
SPEC_ROWS = 2 * FFT_N
RADIX8 = 8
N_ALPHA = 5

VMEM_LIMIT = 56 * 1024 * 1024
FUSED_VMEM_LIMIT = 62 * 1024 * 1024


def _cparams(sem):
    return pltpu.CompilerParams(dimension_semantics=sem, vmem_limit_bytes=VMEM_LIMIT)


def _norm_matmul_kernel(x_ref, g_ref, w_ref, o_ref, xn_ref):
    @pl.when(pl.program_id(1) == 0)
    def _():
        x = x_ref[...]
        ms = jnp.mean(x * x, axis=-1, keepdims=True)
        xn_ref[...] = (x * lax.rsqrt(ms + NORM_EPS) * g_ref[...]).astype(BF16)

    o_ref[...] = jnp.dot(xn_ref[...], w_ref[...], preferred_element_type=F32).astype(o_ref.dtype)


def _norm_matmul(x2d, g, w_bf16, out_dtype, tm=2048, tn=1280):
    t, d = x2d.shape
    n = w_bf16.shape[1]
    return pl.pallas_call(
        _norm_matmul_kernel,
        grid=(t // tm, n // tn),
        in_specs=[
            pl.BlockSpec((tm, d), lambda i, j: (i, 0)),
            pl.BlockSpec((1, d), lambda i, j: (0, 0)),
            pl.BlockSpec((d, tn), lambda i, j: (0, j)),
        ],
        out_specs=pl.BlockSpec((tm, tn), lambda i, j: (i, j)),
        out_shape=jax.ShapeDtypeStruct((t, n), out_dtype),
        scratch_shapes=[pltpu.VMEM((tm, d), BF16)],
        compiler_params=_cparams(("parallel", "arbitrary")),
        name="norm_matmul",
    )(x2d, g, w_bf16)


def _qkv_group_kernel(x_ref, g_ref, w_ref, hg_ref, o_ref, *scratch, tm, dil):
    rows = tm // dil
    x = x_ref[...]
    ms = jnp.mean(x * x, axis=-1, keepdims=True)
    xn = x * lax.rsqrt(ms + NORM_EPS) * g_ref[...]
    if dil == 1:
        xn = xn.astype(BF16)
    else:
        s_ref, = scratch
        n_slab = xn.shape[1] // LANES
        for c in range(n_slab):
            s_ref[c * tm:(c + 1) * tm, :] = xn[:, c * LANES:(c + 1) * LANES]
        parts = []
        for r in range(dil):
            slabs = [s_ref[pl.ds(c * tm + r, rows, stride=dil), :] for c in range(n_slab)]
            parts.append(jnp.concatenate(slabs, axis=1).astype(BF16))
        xn = jnp.concatenate(parts, axis=0)
    hg = hg_ref[...]
    for jt in range(3):
        acc = jnp.dot(xn, w_ref[:, jt * GROUP_WIDTH:(jt + 1) * GROUP_WIDTH], preferred_element_type=F32)
        if jt < 2:
            heads = []
            for h in range(HEADS_PER_GROUP):
                a = acc[:, h * HEAD_DIM:(h + 1) * HEAD_DIM]
                ms = jnp.mean(a * a, axis=-1, keepdims=True)
                c0 = jt * GROUP_WIDTH + h * HEAD_DIM
                heads.append(a * lax.rsqrt(ms + NORM_EPS) * hg[:, c0:c0 + HEAD_DIM])
            acc = jnp.concatenate(heads, axis=1)
        acc = acc.astype(BF16)
        for r in range(dil):
            c0 = r * GROUP_QKV + jt * GROUP_WIDTH
            o_ref[:, c0:c0 + GROUP_WIDTH] = acc[r * rows:(r + 1) * rows, :]


def _qkv_group(x2d, g, w_g, head_gain_g, dil, tm=1024):
    t, d = x2d.shape
    return pl.pallas_call(
        functools.partial(_qkv_group_kernel, tm=tm, dil=dil),
        grid=(t // tm,),
        in_specs=[
            pl.BlockSpec((tm, d), lambda i: (i, 0)),
            pl.BlockSpec((1, d), lambda i: (0, 0)),
            pl.BlockSpec((d, GROUP_QKV), lambda i: (0, 0)),
            pl.BlockSpec((1, GROUP_QKV), lambda i: (0, 0)),
        ],
        out_specs=pl.BlockSpec((tm // dil, dil * GROUP_QKV), lambda i: (i, 0)),
        out_shape=jax.ShapeDtypeStruct((t // dil, dil * GROUP_QKV), BF16),
        scratch_shapes=[] if dil == 1 else [pltpu.VMEM((tm * d // LANES, LANES), F32)],
        compiler_params=_cparams(("parallel",)),
        name=f"qkv_group_d{dil}",
    )(x2d, g, w_g, head_gain_g)


def _t5_bucket_np(rel):
    nb = N_BUCKETS // 2
    max_exact = nb // 2
    side = np.where(rel > 0, nb, 0)
    n = np.abs(rel)
    nf = np.maximum(n, 1).astype(np.float32)
    large = max_exact + (np.log(nf / np.float32(max_exact)) / np.float32(math.log(MAX_DISTANCE / max_exact))
                         * np.float32(nb - max_exact)).astype(np.int32)
    large = np.minimum(large, nb - 1)
    return side + np.where(n < max_exact, n, large)


def _band_bucket_index(dil):
    qi = np.arange(2 * BAND)[:, None]
    kj = np.arange(4 * BAND)[None, :]
    delta = kj - BAND - qi
    idx = _t5_bucket_np(delta * dil).astype(np.int32)
    return np.where(np.abs(delta) <= BAND, idx, -1).astype(np.int32)


def _attn_kernel(idx_ref, tbl_ref, c_ref, p_ref, n_ref, o_ref, lse_ref, bias_ref, *, tq, m_len, dil):
    qb = 2 * BAND
    wb = 4 * BAND
    nsub = tq // qb
    first = (pl.program_id(0) == 0) & (pl.program_id(1) == 0)

    @pl.when(first)
    def _():
        idx = idx_ref[...]
        for h in range(HEADS_PER_GROUP):
            acc = jnp.full((qb, wb), MASK_VALUE, F32)
            for b in range(N_BUCKETS):
                acc = jnp.where(idx == b, tbl_ref[h, b], acc)
            bias_ref[h] = acc

    i = pl.program_id(1)
    col = lax.broadcasted_iota(jnp.int32, (qb, wb), 1)
    lane = lax.broadcasted_iota(jnp.int32, (qb, HEAD_DIM), 1)

    def window(s, cols):
        lo, hi = s * qb - BAND, s * qb + qb + BAND
        parts = []
        if lo < 0:
            parts.append(p_ref[0, :, cols])
            lo = 0
        parts.append(c_ref[0, lo:min(hi, tq), cols])
        if hi > tq:
            parts.append(n_ref[0, :, cols])
        return parts[0] if len(parts) == 1 else jnp.concatenate(parts, axis=0)

    for s in range(nsub):
        lo = BAND - i * tq - qb * s
        valid = (col >= lo) & (col < m_len + lo)
        for r in range(dil):
            base = r * GROUP_QKV
            lse_tile = jnp.zeros((qb, HEAD_DIM), F32)
            rows = pl.ds(s * qb * dil + r, qb, stride=dil) if dil > 1 else pl.ds(s * qb, qb)
            for h in range(HEADS_PER_GROUP):
                qs = c_ref[0, s * qb:(s + 1) * qb, base + h * HEAD_DIM:base + (h + 1) * HEAD_DIM]
                kk = window(s, slice(base + GROUP_WIDTH + h * HEAD_DIM, base + GROUP_WIDTH + (h + 1) * HEAD_DIM))
                sc = lax.dot_general(qs, kk, (((1,), (1,)), ((), ())), preferred_element_type=F32)
                sc = jnp.where(valid, sc + bias_ref[h], MASK_VALUE)
                mx = jnp.max(sc, axis=-1, keepdims=True)
                p = jnp.exp(sc - mx)
                den = jnp.sum(p, axis=-1, keepdims=True)
                vv = window(s, slice(base + 2 * GROUP_WIDTH + h * HEAD_DIM,
                                     base + 2 * GROUP_WIDTH + (h + 1) * HEAD_DIM))
                o = jnp.dot(p.astype(BF16), vv, preferred_element_type=F32) / den
                o_ref[0, h, rows, :] = o
                lse_tile = jnp.where(lane == h, mx + jnp.log(den), lse_tile)
            lse_ref[0, rows, :] = lse_tile


def _attn_group(qkv_g, rel_bias, gi, bsz, seq):
    _, dil = DILATION_GROUPS[gi]
    m_len = seq // dil
    tq = min(m_len, {1: 2048, 4: 512}.get(dil, 128))
    nblk = tq // BAND
    n_halo = m_len // BAND
    width = dil * GROUP_QKV
    view = qkv_g.reshape(bsz, m_len, width)
    idx = jnp.asarray(_band_bucket_index(dil))
    tbl = rel_bias[:, gi * HEADS_PER_GROUP:(gi + 1) * HEADS_PER_GROUP].T.astype(F32)
    o, lse = pl.pallas_call(
        functools.partial(_attn_kernel, tq=tq, m_len=m_len, dil=dil),
        grid=(bsz, m_len // tq),
        in_specs=[
            pl.BlockSpec((2 * BAND, 4 * BAND), lambda b, i: (0, 0)),
            pl.BlockSpec(memory_space=pltpu.SMEM),
            pl.BlockSpec((1, tq, width), lambda b, i: (b, i, 0)),
            pl.BlockSpec((1, BAND, width), lambda b, i: (b, jnp.maximum(i * nblk - 1, 0), 0)),
            pl.BlockSpec((1, BAND, width), lambda b, i: (b, jnp.minimum((i + 1) * nblk, n_halo - 1), 0)),
        ],
        out_specs=[
            pl.BlockSpec((1, HEADS_PER_GROUP, tq * dil, HEAD_DIM), lambda b, i: (b, 0, i, 0)),
            pl.BlockSpec((1, tq * dil, HEAD_DIM), lambda b, i: (b, i, 0)),
        ],
        out_shape=[
            jax.ShapeDtypeStruct((bsz, HEADS_PER_GROUP, seq, HEAD_DIM), F32),
            jax.ShapeDtypeStruct((bsz, seq, HEAD_DIM), F32),
        ],
        scratch_shapes=[pltpu.VMEM((HEADS_PER_GROUP, 2 * BAND, 4 * BAND), F32)],
        compiler_params=_cparams(("arbitrary", "arbitrary")),
        name=f"band_attn_g{gi}",
    )(idx, tbl, view, view, view)
    return o, lse.reshape(bsz * seq, HEAD_DIM)


def _short_conv_kernel(c_ref, p_ref, n_ref, w_ref, b_ref, o_ref, *, tl):
    i = pl.program_id(1)
    last = pl.num_programs(1) - 1
    cur = c_ref[0].astype(F32)
    prev_row = jnp.where(i == 0, 0.0, p_ref[0, HALO - 1:HALO, :].astype(F32))
    next_row = jnp.where(i == last, 0.0, n_ref[0, 0:1, :].astype(F32))
    row = lax.broadcasted_iota(jnp.int32, cur.shape, 0)
    up = jnp.where(row == 0, prev_row, pltpu.roll(cur, 1, 0))
    dn = jnp.where(row == tl - 1, next_row, pltpu.roll(cur, tl - 1, 0))
    w = w_ref[...]
    o_ref[0] = up * w[0:1, :] + cur * w[1:2, :] + dn * w[2:3, :] + b_ref[...]


def _short_conv(hyg3, w_short, b_short, tl=2048, ct=512):
    bsz, seq, _ = hyg3.shape
    width = w_short.shape[1]
    nsub = tl // HALO
    return pl.pallas_call(
        functools.partial(_short_conv_kernel, tl=tl),
        grid=(bsz, seq // tl, width // ct),
        in_specs=[
            pl.BlockSpec((1, tl, ct), lambda b, i, c: (b, i, c)),
            pl.BlockSpec((1, HALO, ct), lambda b, i, c: (b, jnp.maximum(i * nsub - 1, 0), c)),
            pl.BlockSpec((1, HALO, ct), lambda b, i, c: (b, jnp.minimum((i + 1) * nsub, seq // HALO - 1), c)),
            pl.BlockSpec((3, ct), lambda b, i, c: (0, c)),
            pl.BlockSpec((1, ct), lambda b, i, c: (0, c)),
        ],
        out_specs=pl.BlockSpec((1, tl, ct), lambda b, i, c: (b, i, c)),
        out_shape=jax.ShapeDtypeStruct((bsz, seq, width), F32),
        compiler_params=_cparams(("parallel", "parallel", "parallel")),
        name="short_conv",
    )(hyg3, hyg3, hyg3, w_short, b_short)


FILTER_PACK = 4


def _filter_kernel(w1_ref, t1_ref, b1_ref, w2_ref, b2_ref, w3_ref, b3_ref, fr_ref, w4_ref,
                   ad_ref, o_ref, *, seq, tr):
    hp = lax.Precision.HIGHEST
    rows = tr // FILTER_PACK
    hid = FILTER_HIDDEN
    nfeat = 2 * FILTER_BANDS
    base = pl.program_id(0) * tr

    def lag_of(shape, lanes_per_group):
        r = lax.broadcasted_iota(jnp.int32, shape, 0)
        g = lax.broadcasted_iota(jnp.int32, shape, 1) // lanes_per_group
        n = base + g * rows + r
        return jnp.where(n < seq, n, 2 * seq - n).astype(F32)

    f = lax.broadcasted_iota(jnp.int32, (rows, FILTER_PACK * nfeat), 1) % nfeat
    band_idx = jnp.where(f < FILTER_BANDS, f, f - FILTER_BANDS).astype(F32)
    bands = 1e-4 + band_idx * ((FILTER_BANDS - 1 - 1e-4) / (FILTER_BANDS - 1))
    ang = (2.0 * math.pi) * lag_of((rows, FILTER_PACK * nfeat), nfeat) / seq
    z = jnp.cos(bands * ang + jnp.where(f < FILTER_BANDS, 0.0, 0.5 * math.pi))
    t = lag_of((rows, FILTER_PACK * hid), hid) / (seq - 1)
    fr = fr_ref[...]
    h = jnp.sin(fr * (jnp.dot(z, w1_ref[...], precision=hp, preferred_element_type=F32)
                      + t * t1_ref[...] + b1_ref[...]))
    h = jnp.sin(fr * (jnp.dot(h, w2_ref[...], precision=hp, preferred_element_type=F32) + b2_ref[...]))
    h = jnp.sin(fr * (jnp.dot(h, w3_ref[...], precision=hp, preferred_element_type=F32) + b3_ref[...]))
    w_hi, w_lo = w4_ref[0], w4_ref[1]
    for g in range(FILTER_PACK):
        hg = h[:, g * hid:(g + 1) * hid]
        h_hi = hg.astype(BF16)
        h_lo = (hg - h_hi.astype(F32)).astype(BF16)
        taps = (jnp.dot(h_hi, w_hi, preferred_element_type=F32) + jnp.dot(h_hi, w_lo, preferred_element_type=F32)
                + jnp.dot(h_lo, w_hi, preferred_element_type=F32))
        decay = jnp.exp(-t[:, g * hid:g * hid + 1] * ad_ref[...])
        n = base + g * rows + lax.broadcasted_iota(jnp.int32, (rows, 1), 0)
        o_ref[g * rows:(g + 1) * rows, :] = jnp.where(n == seq, 0.0, taps * decay)


def _filter_taps(seq, w1, b1, w2, b2, w3, b3, w4, freq, tr=1024):
    c = HYENA_WIDTH
    hid = FILTER_HIDDEN
    eye = jnp.eye(FILTER_PACK, dtype=F32)
    tile = lambda v: jnp.tile(v.reshape(1, -1), (1, FILTER_PACK))
    w1bd = jnp.kron(eye, w1[1:])
    w2bd, w3bd = jnp.kron(eye, w2), jnp.kron(eye, w3)
    w4r = w4.reshape(hid, HYENA_ORDER, 2, c)
    w4d = jnp.transpose(w4r, (2, 0, 1, 3)).reshape(2, hid, HYENA_ORDER * c)
    w4_hi = w4d.astype(BF16)
    w4_lo = (w4d - w4_hi.astype(F32)).astype(BF16)
    w4d = jnp.stack([w4_hi, w4_lo], axis=1)
    nblk = 2 * seq // tr
    min_decay = math.log(DECAY_TARGET) / SLOW_DECAY_PCT
    max_decay = math.log(DECAY_TARGET) / FAST_DECAY_PCT
    ad = jnp.abs(jnp.linspace(min_decay, max_decay, c, dtype=F32))
    ad = jnp.tile(ad, HYENA_ORDER)[None, :]
    full = lambda shape: pl.BlockSpec(shape, lambda i: (0,) * len(shape))
    ph = FILTER_PACK * hid
    return pl.pallas_call(
        functools.partial(_filter_kernel, seq=seq, tr=tr),
        grid=(nblk,),
        in_specs=[full(w1bd.shape), full((1, ph)), full((1, ph)), full((ph, ph)), full((1, ph)),
                  full((ph, ph)), full((1, ph)), full((1, ph)),
                  pl.BlockSpec((None, 2, hid, HYENA_ORDER * c), lambda i: (i // (nblk // 2), 0, 0, 0)),
                  full((1, HYENA_ORDER * c))],
        out_specs=pl.BlockSpec((tr, HYENA_ORDER * c), lambda i: (i, 0)),
        out_shape=jax.ShapeDtypeStruct((2 * seq, HYENA_ORDER * c), F32),
        compiler_params=_cparams(("parallel",)),
        name="hyena_filter_taps",
    )(w1bd, tile(w1[0]), tile(b1), w2bd, tile(b2), w3bd, tile(b3), tile(freq), w4d, ad)


@functools.lru_cache(maxsize=None)
def _dft_matrices(n1_in, n1_out, n_alpha):
    big = RADIX8 * FFT_N
    a = np.arange(n_alpha).reshape(-1, 1, 1, 1, 1)
    j = np.arange(N_CHUNK).reshape(1, -1, 1, 1, 1)
    k1 = np.arange(FFT_N1).reshape(1, 1, -1, 1, 1)
    t = np.arange(SUB).reshape(1, 1, 1, 1, -1)
    bf16 = jnp.dtype(BF16)

    def stage_a(n1_count, inverse):
        n1 = np.arange(n1_count).reshape(1, 1, 1, -1, 1)
        n2 = SUB * j + t
        e = (RADIX8 * FFT_N2 * n1 * k1 + RADIX8 * n2 * k1 + a * (FFT_N2 * n1 + n2)) % big
        ang = e * (2.0 * np.pi / big)
        cr = np.cos(ang)
        ci = np.sin(ang) if inverse else -np.sin(ang)
        if inverse:
            cr = cr / FFT_N
            ci = ci / FFT_N
        eye = np.eye(SUB)
        if not inverse:
            blk = lambda m: np.einsum('ajknt,ts->ajksnt', m, eye)
            top = np.concatenate([blk(cr), blk(-ci)], axis=4)
            bot = np.concatenate([blk(ci), blk(cr)], axis=4)
            m = np.stack([top, bot], axis=2)
            return m.reshape(n_alpha, N_CHUNK, 2 * FFT_N1 * SUB, 2 * n1_count * SUB).astype(bf16)
        blk = lambda m: np.einsum('ajknt,ts->ajnskt', m, eye)
        top = np.concatenate([blk(cr), blk(-ci)], axis=4)
        bot = np.concatenate([blk(ci), blk(cr)], axis=4)
        m = np.stack([top, bot], axis=2)
        return m.reshape(n_alpha, N_CHUNK, 2 * n1_count * SUB, 2 * FFT_N1 * SUB).astype(bf16)

    kf = stage_a(n1_in, False)
    ki = stage_a(n1_out, True)
    n2 = np.arange(FFT_N2)
    ang = ((n2[:, None] * n2[None, :]) % FFT_N2) * (2.0 * np.pi / FFT_N2)
    gr, gi = np.cos(ang), -np.sin(ang)
    g = np.block([[gr, -gi], [gi, gr]]).astype(bf16)
    ginv = np.block([[gr, gi], [-gi, gr]]).astype(bf16)
    return kf, g, ginv, ki


def _chunk_loop(body):
    for j in range(N_CHUNK):
        body(j, j * SUB)


def _fft_stage_a(z_ref, kf, s1_ref, n1_in):
    def body(j, off):
        chunks = [z_ref[p, pl.ds(FFT_N2 * n1 + off, SUB), :].astype(F32)
                  for p in range(2) for n1 in range(n1_in)]
        slab = jnp.concatenate(chunks, axis=0).astype(BF16)
        a = jnp.dot(kf[j], slab, preferred_element_type=F32)
        for q in range(2):
            for k1 in range(FFT_N1):
                r = (q * FFT_N1 + k1) * SUB
                s1_ref[pl.ds(k1 * 2 * FFT_N2 + q * FFT_N2 + off, SUB), :] = a[r:r + SUB, :]
    _chunk_loop(body)


def _fft_core_kernel(z_ref, kf_ref, g_ref, gi_ref, ki_ref, h_ref, o_ref, s1_ref, s2_ref, *, n1_in, n1_out):
    _fft_stage_a(z_ref, kf_ref, s1_ref, n1_in)

    for k1 in range(FFT_N1):
        r0 = k1 * 2 * FFT_N2
        slab = s1_ref[pl.ds(r0, 2 * FFT_N2), :].astype(BF16)
        x = jnp.dot(g_ref[...], slab, preferred_element_type=F32)
        xr, xi = x[:FFT_N2], x[FFT_N2:]
        hr = h_ref[pl.ds(r0, FFT_N2), :]
        hi = h_ref[pl.ds(r0 + FFT_N2, FFT_N2), :]
        y = jnp.concatenate([xr * hr - xi * hi, xr * hi + xi * hr], axis=0).astype(BF16)
        s2_ref[pl.ds(r0, 2 * FFT_N2), :] = jnp.dot(gi_ref[...], y, preferred_element_type=F32)

    def stage_c(j, off):
        chunks = [s2_ref[pl.ds(k1 * 2 * FFT_N2 + q * FFT_N2 + off, SUB), :]
                  for q in range(2) for k1 in range(FFT_N1)]
        slab = jnp.concatenate(chunks, axis=0).astype(BF16)
        y = jnp.dot(ki_ref[j], slab, preferred_element_type=F32)
        for p in range(2):
            for n1 in range(n1_out):
                r = (p * n1_out + n1) * SUB
                o_ref[p, pl.ds(FFT_N2 * n1 + off, SUB), :] = y[r:r + SUB, :]
    _chunk_loop(stage_c)


def _fft_core_full_kernel(z_ref, kf_ref, g_ref, gi_ref, ki_ref, h_ref, o_ref, s1_ref):
    half = FFT_N1 // 2
    _fft_stage_a(z_ref, kf_ref, s1_ref, FFT_N1)

    for k1 in range(FFT_N1):
        r0 = k1 * 2 * FFT_N2
        slab = s1_ref[pl.ds(r0, 2 * FFT_N2), :].astype(BF16)
        x = jnp.dot(g_ref[...], slab, preferred_element_type=F32)
        xr, xi = x[:FFT_N2], x[FFT_N2:]
        hr = h_ref[pl.ds(r0, FFT_N2), :]
        hi = h_ref[pl.ds(r0 + FFT_N2, FFT_N2), :]
        y = jnp.concatenate([xr * hr - xi * hi, xr * hi + xi * hr], axis=0).astype(BF16)
        o_ref[k1 // half, pl.ds((k1 % half) * 2 * FFT_N2, 2 * FFT_N2), :] = jnp.dot(
            gi_ref[...], y, preferred_element_type=F32)

    def stage_c(j, off):
        chunks = [o_ref[k1 // half, pl.ds((k1 % half) * 2 * FFT_N2 + q * FFT_N2 + off, SUB), :]
                  for q in range(2) for k1 in range(FFT_N1)]
        slab = jnp.concatenate(chunks, axis=0).astype(BF16)
        y = jnp.dot(ki_ref[j], slab, preferred_element_type=F32)
        for p in range(2):
            for n1 in range(FFT_N1):
                r = (p * FFT_N1 + n1) * SUB
                s1_ref[pl.ds(p * FFT_N + FFT_N2 * n1 + off, SUB), :] = y[r:r + SUB, :]
    _chunk_loop(stage_c)

    def copy_out(i, carry):
        r0 = pl.multiple_of(i * CONV_ROWS, CONV_ROWS)
        for p in range(2):
            o_ref[p, pl.ds(r0, CONV_ROWS), :] = s1_ref[pl.ds(p * FFT_N + r0, CONV_ROWS), :]
        return carry
    lax.fori_loop(0, FFT_N // CONV_ROWS, copy_out, 0)


def _fft_forward_kernel(z_ref, kf_ref, g_ref, o_ref, s1_ref, *, n1_in):
    _fft_stage_a(z_ref, kf_ref, s1_ref, n1_in)

    def stage_b(k1, carry):
        r0 = pl.multiple_of(k1 * 2 * FFT_N2, 2 * FFT_N2)
        slab = s1_ref[pl.ds(r0, 2 * FFT_N2), :].astype(BF16)
        o_ref[pl.ds(r0, 2 * FFT_N2), :] = jnp.dot(g_ref[...], slab, preferred_element_type=F32).astype(o_ref.dtype)
        return carry
    lax.fori_loop(0, FFT_N1, stage_b, 0, unroll=4)


def _const_spec(shape, index_map):
    return pl.BlockSpec(shape, index_map, pipeline_mode=pl.Buffered(1))


def _fft_conv_residues(u, spec, col0, mats, ct=256):
    kf, g, ginv, ki = mats
    c = u.shape[-1]
    cb0 = col0 // ct
    return pl.pallas_call(
        _fft_core_full_kernel,
        grid=(N_ALPHA, c // ct),
        in_specs=[
            pl.BlockSpec((None, 2, FFT_N, ct), lambda a, cc: (a, 0, 0, cc)),
            _const_spec((None,) + kf.shape[1:], lambda a, cc: (a, 0, 0, 0)),
            _const_spec(g.shape, lambda a, cc: (0, 0)),
            _const_spec(ginv.shape, lambda a, cc: (0, 0)),
            _const_spec((None,) + ki.shape[1:], lambda a, cc: (a, 0, 0, 0)),
            _const_spec((None, SPEC_ROWS, ct), lambda a, cc: (a, 0, cb0 + cc)),
        ],
        out_specs=pl.BlockSpec((None, 2, FFT_N, ct), lambda a, cc: (a, 0, 0, cc)),
        out_shape=jax.ShapeDtypeStruct(u.shape, F32),
        scratch_shapes=[pltpu.VMEM((SPEC_ROWS, ct), F32)],
        compiler_params=pltpu.CompilerParams(dimension_semantics=("arbitrary", "arbitrary"),
                                             vmem_limit_bytes=FUSED_VMEM_LIMIT),
        name="fft_conv_residues",
    )(u, kf, g, ginv, ki, spec)


def _fft_forward(u, kf, g, ct):
    n_a, _, _, c = u.shape
    return pl.pallas_call(
        functools.partial(_fft_forward_kernel, n1_in=FFT_N1),
        grid=(n_a, c // ct),
        in_specs=[
            pl.BlockSpec((None, 2, FFT_N, ct), lambda a, cc: (a, 0, 0, cc)),
            _const_spec((None,) + kf.shape[1:], lambda a, cc: (a, 0, 0, 0)),
            _const_spec(g.shape, lambda a, cc: (0, 0)),
        ],
        out_specs=pl.BlockSpec((None, SPEC_ROWS, ct), lambda a, cc: (a, 0, cc)),
        out_shape=jax.ShapeDtypeStruct((n_a, SPEC_ROWS, c), BF16),
        scratch_shapes=[pltpu.VMEM((SPEC_ROWS, ct), F32)],
        compiler_params=_cparams(("arbitrary", "arbitrary")),
        name="fft_forward",
    )(u, kf, g)


def _radix8_coefs(n_blocks):
    a = np.arange(n_blocks)[None, :]
    al = np.arange(N_ALPHA)[:, None]
    ang = 2.0 * np.pi * ((a * al) % RADIX8) / RADIX8
    return np.round(np.cos(ang), 12), np.round(-np.sin(ang), 12)


def _radix8_split_kernel(x_ref, o_ref, *, n_blocks):
    cr, ci = _radix8_coefs(n_blocks)
    xs = [x_ref[a] for a in range(n_blocks)]
    for al in range(N_ALPHA):
        for part, coef in ((0, cr), (1, ci)):
            acc = None
            for a in range(n_blocks):
                w = float(coef[al, a])
                if w == 0.0:
                    continue
                term = xs[a] if w == 1.0 else (-xs[a] if w == -1.0 else xs[a] * w)
                acc = term if acc is None else acc + term
            o_ref[al, part] = jnp.zeros_like(xs[0]) if acc is None else acc


def _radix8_split(x, col0, width, n_blocks, tr=256, ct=512):
    xv = x.reshape(n_blocks, FFT_N, x.shape[1])
    cb0 = col0 // ct
    return pl.pallas_call(
        functools.partial(_radix8_split_kernel, n_blocks=n_blocks),
        grid=(FFT_N // tr, width // ct),
        in_specs=[pl.BlockSpec((n_blocks, tr, ct), lambda i, c: (0, i, cb0 + c))],
        out_specs=pl.BlockSpec((N_ALPHA, 2, tr, ct), lambda i, c: (0, 0, i, c)),
        out_shape=jax.ShapeDtypeStruct((N_ALPHA, 2, FFT_N, width), F32),
        compiler_params=_cparams(("parallel", "parallel")),
        name="radix8_split",
    )(xv)


def _radix8_merge_gate_kernel(v_ref, z_ref, xg_ref, sk_ref, o_ref, *, n_blocks):
    sk = sk_ref[...]
    for a in range(n_blocks):
        acc = None
        for al in range(N_ALPHA):
            cw = (1.0 if al in (0, RADIX8 // 2) else 2.0) / RADIX8
            ang = 2.0 * np.pi * ((a * al) % RADIX8) / RADIX8
            wr = float(np.round(np.cos(ang), 12)) * cw
            wi = float(np.round(np.sin(ang), 12)) * cw
            for w, part in ((wr, 0), (-wi, 1)):
                if w == 0.0:
                    continue
                term = v_ref[al, part] * w
                acc = term if acc is None else acc + term
        o_ref[a] = xg_ref[a] * (acc + sk * z_ref[a])


def _radix8_merge_gate(v, z, zcol0, xg, gcol0, skip_row, n_blocks, tr=256, ct=512):
    c = v.shape[-1]
    zv = z.reshape(n_blocks, FFT_N, z.shape[1])
    gv = xg.reshape(n_blocks, FFT_N, xg.shape[1])
    zb, gb = zcol0 // ct, gcol0 // ct
    out = pl.pallas_call(
        functools.partial(_radix8_merge_gate_kernel, n_blocks=n_blocks),
        grid=(FFT_N // tr, c // ct),
        in_specs=[
            pl.BlockSpec((N_ALPHA, 2, tr, ct), lambda i, cc: (0, 0, i, cc)),
            pl.BlockSpec((n_blocks, tr, ct), lambda i, cc: (0, i, zb + cc)),
            pl.BlockSpec((n_blocks, tr, ct), lambda i, cc: (0, i, gb + cc)),
            pl.BlockSpec((1, ct), lambda i, cc: (0, cc)),
        ],
        out_specs=pl.BlockSpec((n_blocks, tr, ct), lambda i, cc: (0, i, cc)),
        out_shape=jax.ShapeDtypeStruct((n_blocks, FFT_N, c), F32),
        compiler_params=_cparams(("parallel", "parallel")),
        name="radix8_merge_gate",
    )(v, zv, gv, skip_row)
    return out.reshape(n_blocks * FFT_N, c)


CONV_ROWS = 256


def _conv3_chunk(u_ref, p, i, w, b, seq):
    n_chunks = seq // CONV_ROWS
    r0 = pl.multiple_of(i * CONV_ROWS, CONV_ROWS)
    cur = u_ref[p, pl.ds(r0, CONV_ROWS), :].astype(F32)
    pr = pl.multiple_of(jnp.maximum(r0 - HALO, 0), HALO)
    nr = pl.multiple_of(jnp.minimum(r0 + CONV_ROWS, seq - HALO), HALO)
    prev_row = jnp.where(i == 0, 0.0, u_ref[p, pl.ds(pr, HALO), :][HALO - 1:HALO, :].astype(F32))
    next_row = jnp.where(i == n_chunks - 1, 0.0, u_ref[p, pl.ds(nr, HALO), :][0:1, :].astype(F32))
    row = lax.broadcasted_iota(jnp.int32, cur.shape, 0)
    up = jnp.where(row == 0, prev_row, pltpu.roll(cur, 1, 0))
    dn = jnp.where(row == CONV_ROWS - 1, next_row, pltpu.roll(cur, CONV_ROWS - 1, 0))
    return up * w[0:1, :] + cur * w[1:2, :] + dn * w[2:3, :] + b


def _hyena_pairs_kernel(z_ref, x1_ref, x2_ref, wz_ref, bz_ref, w1_ref, b1_ref, w2_ref, b2_ref, sk_ref,
                        kf_ref, g_ref, gi_ref, ki_ref, h0_ref, h1_ref, o_ref, zin_ref, s1_ref, s2_ref,
                        *, seq):
    n1 = seq // FFT_N2
    n_chunks = seq // CONV_ROWS

    def each_chunk(fn):
        for p in range(2):
            def body(i, carry, p=p):
                fn(p, i, pl.ds(pl.multiple_of(i * CONV_ROWS, CONV_ROWS), CONV_ROWS))
                return carry
            lax.fori_loop(0, n_chunks, body, 0)

    def load_z(p, i, rows):
        zin_ref[p, rows, :] = _conv3_chunk(z_ref, p, i, wz_ref[...], bz_ref[...], seq)
    each_chunk(load_z)

    _fft_core_kernel(zin_ref, kf_ref, g_ref, gi_ref, ki_ref, h0_ref, o_ref, s1_ref, s2_ref, n1_in=n1, n1_out=n1)

    def gate1(p, i, rows):
        xg = _conv3_chunk(x1_ref, p, i, w1_ref[...], b1_ref[...], seq)
        zin_ref[p, rows, :] = xg * (o_ref[p, rows, :] + sk_ref[0:1, :] * zin_ref[p, rows, :])
    each_chunk(gate1)

    _fft_core_kernel(zin_ref, kf_ref, g_ref, gi_ref, ki_ref, h1_ref, o_ref, s1_ref, s2_ref, n1_in=n1, n1_out=n1)

    def gate2(p, i, rows):
        xg = _conv3_chunk(x2_ref, p, i, w2_ref[...], b2_ref[...], seq)
        o_ref[p, rows, :] = xg * (o_ref[p, rows, :] + sk_ref[1:2, :] * zin_ref[p, rows, :])
    each_chunk(gate2)


def _hyena_pairs(hyg3, w_short, b_short, spec, filt_skip, mats, ct=256):
    kf, g, ginv, ki = mats
    bsz, seq, _ = hyg3.shape
    c = HYENA_WIDTH
    nc = c // ct
    hy = lambda k: pl.BlockSpec((2, seq, ct), lambda cc, p: (p, 0, k * nc + cc))
    wsp = lambda k: pl.BlockSpec((3, ct), lambda cc, p: (0, k * nc + cc))
    bsp = lambda k: pl.BlockSpec((1, ct), lambda cc, p: (0, k * nc + cc))
    return pl.pallas_call(
        functools.partial(_hyena_pairs_kernel, seq=seq),
        grid=(nc, bsz // 2),
        in_specs=[
            hy(0), hy(1), hy(2), wsp(0), bsp(0), wsp(1), bsp(1), wsp(2), bsp(2),
            pl.BlockSpec((HYENA_ORDER, ct), lambda cc, p: (0, cc)),
            _const_spec((None,) + kf.shape[1:], lambda cc, p: (0, 0, 0, 0)),
            _const_spec(g.shape, lambda cc, p: (0, 0)),
            _const_spec(ginv.shape, lambda cc, p: (0, 0)),
            _const_spec((None,) + ki.shape[1:], lambda cc, p: (0, 0, 0, 0)),
            _const_spec((SPEC_ROWS, ct), lambda cc, p: (0, cc)),
            _const_spec((SPEC_ROWS, ct), lambda cc, p: (0, nc + cc)),
        ],
        out_specs=pl.BlockSpec((2, seq, ct), lambda cc, p: (p, 0, cc)),
        out_shape=jax.ShapeDtypeStruct((bsz, seq, c), F32),
        scratch_shapes=[pltpu.VMEM((2, seq, ct), F32), pltpu.VMEM((SPEC_ROWS, ct), F32),
                        pltpu.VMEM((SPEC_ROWS, ct), F32)],
        compiler_params=pltpu.CompilerParams(dimension_semantics=("arbitrary", "arbitrary"),
                                             vmem_limit_bytes=FUSED_VMEM_LIMIT),
        name="hyena_pairs",
    )(hyg3, hyg3, hyg3, w_short, b_short, w_short, b_short, w_short, b_short, filt_skip,
      kf, g, ginv, ki, spec, spec)


def _hyena(hyg3, w_short, b_short, filt, filt_skip):
    bsz, seq, _ = hyg3.shape
    c = HYENA_WIDTH
    taps = _filter_taps(seq, *filt)
    if 2 * seq == FFT_N:
        mats = _dft_matrices(seq // FFT_N2, seq // FFT_N2, 1)
        mats_full = _dft_matrices(FFT_N1, FFT_N1, 1)
        u = jnp.stack([taps, jnp.zeros_like(taps)], axis=0)[None]
        spec = _fft_forward(u, mats_full[0], mats_full[1], 256)[0]
        zh = _hyena_pairs(hyg3, w_short, b_short, spec, filt_skip, mats)
        return zh.reshape(bsz * seq, c)
    assert bsz == 1 and 2 * seq == RADIX8 * FFT_N, (bsz, seq)
    hc = _short_conv(hyg3, w_short, b_short).reshape(seq, 3 * c)
    n_blocks = seq // FFT_N
    mats = _dft_matrices(FFT_N1, FFT_N1, N_ALPHA)
    tap_res = _radix8_split(taps, 0, HYENA_ORDER * c, RADIX8)
    spec = _fft_forward(tap_res, mats[0], mats[1], 256)
    z = hc
    for n in range(HYENA_ORDER):
        u = _radix8_split(z, 0, c, n_blocks)
        v = _fft_conv_residues(u, spec, n * c, mats)
        z = _radix8_merge_gate(v, z, 0, hc, (n + 1) * c, filt_skip[n][None, :], n_blocks)
    return z


MERGE_SUBTILES = 2


def _merge_kernel(o0_ref, o1_ref, o2_ref, l0_ref, l1_ref, l2_ref, zh_ref, ga_ref, gh_ref, x_ref,
                  wa_ref, wh_ref, wo_ref, out_ref):
    sub = out_ref.shape[0] // MERGE_SUBTILES
    for s in range(MERGE_SUBTILES):
        rs = slice(s * sub, (s + 1) * sub)
        l0, l1, l2 = l0_ref[rs, :], l1_ref[rs, :], l2_ref[rs, :]
        m = jnp.maximum(jnp.maximum(l0, l1), l2)
        e0, e1, e2 = jnp.exp(l0 - m), jnp.exp(l1 - m), jnp.exp(l2 - m)
        tot = e0 + e1 + e2
        w0, w1, w2 = e0 / tot, e1 / tot, e2 / tot
        heads = []
        for h in range(HEADS_PER_GROUP):
            heads.append(o0_ref[h, rs, :] * w0[:, h:h + 1] + o1_ref[h, rs, :] * w1[:, h:h + 1]
                         + o2_ref[h, rs, :] * w2[:, h:h + 1])
        attn = jnp.concatenate(heads, axis=1).astype(BF16)
        ab = jnp.dot(attn, wa_ref[...], preferred_element_type=F32)
        hb = jnp.dot(zh_ref[rs, :].astype(BF16), wh_ref[...], preferred_element_type=F32)
        merged = (jax.nn.sigmoid(ga_ref[rs, :].astype(F32)) * ab
                  + jax.nn.sigmoid(gh_ref[rs, :].astype(F32)) * hb)
        out_ref[rs, :] = x_ref[rs, :] + jnp.dot(merged.astype(BF16), wo_ref[...], preferred_element_type=F32)


def _merge(outs, lses, zh, hyg, x2d, wa, wh, wo, tm=512):
    t, d = x2d.shape
    seq = outs[0].shape[2]
    per_seq = seq // tm
    tok = lambda w: pl.BlockSpec((tm, w), lambda i: (i, 0))
    full = lambda a: _const_spec(a.shape, lambda i: (0, 0))
    heads = pl.BlockSpec((None, HEADS_PER_GROUP, tm, HEAD_DIM),
                         lambda i: (i // per_seq, 0, i % per_seq, 0))
    gcol = 3 * HYENA_WIDTH // d
    return pl.pallas_call(
        _merge_kernel,
        grid=(t // tm,),
        in_specs=[heads] * 3 + [tok(HEAD_DIM)] * 3 + [
            tok(HYENA_WIDTH),
            pl.BlockSpec((tm, d), lambda i: (i, gcol)),
            pl.BlockSpec((tm, d), lambda i: (i, gcol + 1)),
            tok(d), full(wa), full(wh), full(wo)],
        out_specs=tok(d),
        out_shape=jax.ShapeDtypeStruct((t, d), F32),
        compiler_params=_cparams(("parallel",)),
        name="merge_out_proj",
    )(*outs, *lses, zh, hyg, hyg, x2d, wa, wh, wo)


def _mlp_kernel(x_ref, g_ref, w1_ref, w2_ref, o_ref, xn_ref, acc_ref):
    j = pl.program_id(1)

    @pl.when(j == 0)
    def _():
        x = x_ref[...]
        ms = jnp.mean(x * x, axis=-1, keepdims=True)
        xn_ref[...] = (x * lax.rsqrt(ms + NORM_EPS) * g_ref[...]).astype(BF16)
        acc_ref[...] = jnp.zeros_like(acc_ref)

    h = jnp.dot(xn_ref[...], w1_ref[...], preferred_element_type=F32)
    a = jnp.square(jnp.maximum(h, 0.0)).astype(BF16)
    acc_ref[...] += jnp.dot(a, w2_ref[...], preferred_element_type=F32)

    @pl.when(j == pl.num_programs(1) - 1)
    def _():
        o_ref[...] = x_ref[...] + acc_ref[...]


def _mlp(x2d, g, w1, w2, tm=1024, tf=1024):
    t, d = x2d.shape
    f = w1.shape[1]
    return pl.pallas_call(
        _mlp_kernel,
        grid=(t // tm, f // tf),
        in_specs=[
            pl.BlockSpec((tm, d), lambda i, j: (i, 0)),
            pl.BlockSpec((1, d), lambda i, j: (0, 0)),
            pl.BlockSpec((d, tf), lambda i, j: (0, j)),
            pl.BlockSpec((tf, d), lambda i, j: (j, 0)),
        ],
        out_specs=pl.BlockSpec((tm, d), lambda i, j: (i, 0)),
        out_shape=jax.ShapeDtypeStruct((t, d), F32),
        scratch_shapes=[pltpu.VMEM((tm, d), BF16), pltpu.VMEM((tm, d), F32)],
        compiler_params=_cparams(("parallel", "arbitrary")),
        name="mlp",
    )(x2d, g, w1, w2)


def _encoder_layer(x, p):
    bsz, seq, d = x.shape
    x2d = x.reshape(bsz * seq, d)
    hyg = _norm_matmul(x2d, p['g_mix'], p['w_hyg'], BF16)
    outs, lses = [], []
    for gi, (_, dil) in enumerate(DILATION_GROUPS):
        qkv_g = _qkv_group(x2d, p['g_mix'], p['w_qkv'][gi], p['head_gain'], dil)
        o, s = _attn_group(qkv_g, p['rel_bias'], gi, bsz, seq)
        outs.append(o)
        lses.append(s)
    zh = _hyena(hyg.reshape(bsz, seq, HYG_WIDTH), p['w_short'], p['b_short'], p['filt'], p['filt_skip'])
    x1 = _merge(outs, lses, zh, hyg, x2d, p['wa'], p['wh'], p['wo'])
    y = _mlp(x1, p['g_mlp'], p['w1'], p['w2'])
    return y.reshape(bsz, seq, d)


def kernel(x_prompt, x_sample, rel_bias, g_mix, w_in, g_q, g_k, w_attn_branch, w_short, b_short,
           filt_w1, filt_b1, filt_w2, filt_b2, filt_w3, filt_b3, filt_w4, filt_freq, filt_skip,
           w_hyena_branch, w_out, g_mlp, w_ff1, w_ff2):
    y_prompt, y_sample = x_prompt, x_sample
    for l in range(g_mix.shape[0]):
        w_in_b = w_in[l].astype(BF16)
        w_qkv = [jnp.concatenate([w_in_b[:, s * ATTN_WIDTH + gi * GROUP_WIDTH:s * ATTN_WIDTH + (gi + 1) * GROUP_WIDTH]
                                  for s in range(3)], axis=1) for gi in range(N_GROUPS)]
        head_gain = jnp.concatenate([
            jnp.tile(g_q[l].astype(F32) * (HEAD_DIM ** -0.5), HEADS_PER_GROUP),
            jnp.tile(g_k[l].astype(F32), HEADS_PER_GROUP),
            jnp.ones((GROUP_WIDTH,), F32)])[None, :]
        p = dict(
            g_mix=g_mix[l][None, :].astype(F32),
            w_qkv=w_qkv, w_hyg=w_in_b[:, QKV_WIDTH:],
            head_gain=head_gain,
            rel_bias=rel_bias,
            w_short=w_short[l], b_short=b_short[l][None, :],
            filt=(filt_w1[l], filt_b1[l], filt_w2[l], filt_b2[l], filt_w3[l], filt_b3[l], filt_w4[l],
                  filt_freq[l]),
            filt_skip=filt_skip[l],
            wa=w_attn_branch[l].astype(BF16), wh=w_hyena_branch[l].astype(BF16),
            wo=w_out[l].astype(BF16),
            g_mlp=g_mlp[l][None, :].astype(F32), w1=w_ff1[l].astype(BF16), w2=w_ff2[l].astype(BF16))
        y_prompt = _encoder_layer(y_prompt, p)
        y_sample = _encoder_layer(y_sample, p)
    return (y_prompt, y_sample)
```
